```python
import jax, jax.numpy as jnp
from jax import lax
import numpy as np

D_MODEL = 1024
BATCH = 16
SEQ = 256
DEPTH = 4
DEC_BATCH = 4
DEC_SEQ = 2048
PAST_LEN = 512

GRID_W = 64
HEAD_DIM = 64
NA_HEADS = 4
NA_WIN_ROWS = 8
NA_WIN_COLS = 16
NA_QCOLS = 16
NA_KCOLS = 32
POOL_GROUPS = 4
POOL_GW = 64
POOL_WINDOWS = (2, 4, 8, 16)
MLA_HEADS = 4
MLA_Q_RANK = 256
MLA_KV_RANK = 128
MLA_NOPE = 64
MLA_ROPE = 32
MLA_V = 64
SWA_HEADS = 4
SWA_KV_HEADS = 2
SWA_WINDOW = 128
SWA_BLOCK = 128
D_FF = 4 * D_MODEL
Q_BLOCK = 128
ROPE_THETA = 10000.0
EPS = 1e-6
NEG = -1e30
IN_SIZES = (NA_HEADS * HEAD_DIM, NA_HEADS * HEAD_DIM, NA_HEADS * HEAD_DIM,
            POOL_GROUPS * POOL_GW,
            MLA_Q_RANK, MLA_KV_RANK, MLA_ROPE,
            SWA_HEADS * HEAD_DIM, SWA_KV_HEADS * HEAD_DIM, SWA_KV_HEADS * HEAD_DIM)
IN_COLS = sum(IN_SIZES)
MIX_WIDTH = NA_HEADS * HEAD_DIM + POOL_GROUPS * POOL_GW + MLA_HEADS * MLA_V + SWA_HEADS * HEAD_DIM

kernel_name = 'hybrid_dit_parallel_heads_step'


def rms_norm(x, g):
    xf = x.astype(jnp.float32)
    xf = xf * lax.rsqrt(jnp.mean(xf * xf, axis=-1, keepdims=True) + EPS)
    return (xf * g.astype(jnp.float32)).astype(x.dtype)


def adaln(cond, w, b):
    return jnp.split(jax.nn.silu(cond) @ w + b, 6, axis=-1)


def axial_rope(x, rows, cols):
    half = x.shape[-1] // 2
    quarter = half // 2
    freqs = ROPE_THETA ** (-jnp.arange(quarter, dtype=jnp.float32) / quarter)

    def rotate(xa, pos):
        ang = pos.astype(jnp.float32)[:, None] * freqs
        cos = jnp.cos(ang)[:, None, :]
        sin = jnp.sin(ang)[:, None, :]
        x1, x2 = xa[..., :quarter], xa[..., quarter:]
        return jnp.concatenate([x1 * cos - x2 * sin, x1 * sin + x2 * cos], axis=-1)

    xf = x.astype(jnp.float32)
    out = jnp.concatenate([rotate(xf[..., :half], rows), rotate(xf[..., half:], cols)], axis=-1)
    return out.astype(x.dtype)


def rope_tail(x, start, rows, cols):
    return jnp.concatenate([x[..., :start], axial_rope(x[..., start:], rows, cols)], axis=-1)


def dense_attention(q, k, v, sink=None):
    b, nq, h, dk = q.shape
    scale = dk ** -0.5
    qb = q.reshape(b, nq // Q_BLOCK, Q_BLOCK, h, dk).swapaxes(0, 1)

    def block(qi):
        s = jnp.einsum('bqhd,bkhd->bhqk', qi, k).astype(jnp.float32) * scale
        if sink is not None:
            s_sink = jnp.broadcast_to(sink.astype(jnp.float32)[None, :, None, None], s.shape[:-1] + (1,))
            p = jax.nn.softmax(jnp.concatenate([s, s_sink], axis=-1), axis=-1)[..., :-1]
        else:
            p = jax.nn.softmax(s, axis=-1)
        return jnp.einsum('bhqk,bkhd->bqhd', p.astype(v.dtype), v)

    out = lax.map(block, qb)
    return out.swapaxes(0, 1).reshape(b, nq, h, v.shape[-1])


def neighbourhood_attention(q, k, v, ctx_k, ctx_v, rpb):
    b, n, h, d = q.shape
    rows = n // GRID_W
    wr = min(NA_WIN_ROWS, rows)
    ncb = GRID_W // NA_QCOLS
    qr = jnp.arange(rows)
    key_rows = jnp.clip(qr - wr // 2, 0, rows - wr)[:, None] + jnp.arange(wr)[None, :]
    qcol = jnp.arange(GRID_W).reshape(ncb, NA_QCOLS)
    col_start = jnp.clip(qcol - NA_WIN_COLS // 2, 0, GRID_W - NA_WIN_COLS)
    key_cols = jnp.clip(col_start[:, :1], 0, GRID_W - NA_KCOLS) + jnp.arange(NA_KCOLS)[None, :]
    nk = wr * NA_KCOLS

    def gather(x):
        xg = x.reshape(b, rows, GRID_W, h, d)
        blk = xg[:, key_rows[:, None, :, None], key_cols[None, :, None, :]]
        return blk.reshape(b, rows, ncb, nk, h, d)

    kb, vb = gather(k), gather(v)
    qg = q.reshape(b, rows, ncb, NA_QCOLS, h, d)
    inside = (key_cols[:, None, :] >= col_start[:, :, None]) & (key_cols[:, None, :] < col_start[:, :, None] + NA_WIN_COLS)
    mask = jnp.broadcast_to(inside[:, :, None, :], (ncb, NA_QCOLS, wr, NA_KCOLS)).reshape(ncb, NA_QCOLS, nk)
    dr = key_rows - qr[:, None] + NA_WIN_ROWS - 1
    dc = jnp.clip(key_cols[:, None, :] - qcol[:, :, None] + NA_WIN_COLS - 1, 0, 2 * NA_WIN_COLS - 2)
    bias = rpb[:, dr[:, None, None, :, None], dc[None, :, :, None, :]]
    bias = bias.transpose(1, 2, 0, 3, 4, 5).reshape(rows, ncb, h, NA_QCOLS, nk).astype(jnp.float32)
    scale = d ** -0.5
    s_loc = jnp.einsum('brcqhd,brckhd->brchqk', qg, kb).astype(jnp.float32) * scale + bias
    s_loc = jnp.where(mask[None, None, :, None, :, :], s_loc, NEG)
    s_ctx = jnp.einsum('brcqhd,blhd->brchql', qg, ctx_k).astype(jnp.float32) * scale
    p = jax.nn.softmax(jnp.concatenate([s_loc, s_ctx], axis=-1), axis=-1).astype(v.dtype)
    out = (jnp.einsum('brchqk,brckhd->brcqhd', p[..., :nk], vb)
           + jnp.einsum('brchql,blhd->brcqhd', p[..., nk:], ctx_v))
    return out.reshape(b, n, h, d)


def window_attention(q, k, v, ctx_k, ctx_v, sink):
    b, n, hq, d = q.shape
    nb = n // SWA_BLOCK
    g = hq // SWA_KV_HEADS
    qb = q.reshape(b, nb, SWA_BLOCK, SWA_KV_HEADS, g, d)

    def band(x):
        xp = jnp.pad(x, ((0, 0), (SWA_BLOCK, SWA_BLOCK), (0, 0), (0, 0)))
        xp = xp.reshape(b, nb + 2, SWA_BLOCK, SWA_KV_HEADS, d)
        return jnp.concatenate([xp[:, :nb], xp[:, 1:nb + 1], xp[:, 2:]], axis=2)

    kb, vb = band(k), band(v)
    kl = 3 * SWA_BLOCK
    qpos = jnp.arange(n).reshape(nb, SWA_BLOCK)
    kpos = (jnp.arange(nb)[:, None] - 1) * SWA_BLOCK + jnp.arange(kl)[None, :]
    valid = ((kpos[:, None, :] >= 0) & (kpos[:, None, :] < n)
             & (jnp.abs(kpos[:, None, :] - qpos[:, :, None]) <= SWA_WINDOW))
    scale = d ** -0.5
    s_loc = jnp.einsum('bnqhgd,bnjhd->bnhgqj', qb, kb).astype(jnp.float32) * scale
    s_loc = jnp.where(valid[None, :, None, None, :, :], s_loc, NEG)
    s_ctx = jnp.einsum('bnqhgd,blhd->bnhgql', qb, ctx_k).astype(jnp.float32) * scale
    s_sink = jnp.broadcast_to(sink.astype(jnp.float32).reshape(SWA_KV_HEADS, g)[None, None, :, :, None, None],
                              s_loc.shape[:-1] + (1,))
    p = jax.nn.softmax(jnp.concatenate([s_loc, s_ctx, s_sink], axis=-1), axis=-1).astype(v.dtype)
    out = (jnp.einsum('bnhgqj,bnjhd->bnqhgd', p[..., :kl], vb)
           + jnp.einsum('bnhgql,blhd->bnqhgd', p[..., kl:-1], ctx_v))
    return out.reshape(b, n, hq, d)


def multiscale_pool(x, w_pool, scale):
    b, n, _ = x.shape
    xf = x.astype(jnp.float32)
    t = jnp.arange(n)
    outs = []
    for gi, w in enumerate(POOL_WINDOWS):
        xg = xf[..., gi * POOL_GW:(gi + 1) * POOL_GW]
        cs = jnp.pad(jnp.cumsum(xg, axis=1), ((0, 0), (1, 0), (0, 0)))
        lo = jnp.clip(t - w // 2, 0, n)
        hi = jnp.clip(t - w // 2 + w, 0, n)
        mean = (cs[:, hi] - cs[:, lo]) / (hi - lo).astype(jnp.float32)[None, :, None]
        outs.append(mean - xg)
    pooled = jnp.stack(outs, axis=2).astype(x.dtype)
    y = jnp.einsum('bngc,gce->bnge', pooled, w_pool).reshape(b, n, POOL_GROUPS * POOL_GW)
    return y * scale


def project_mixers(h, p):
    b, n, _ = h.shape
    offs = np.cumsum(IN_SIZES)[:-1].tolist()
    (na_q, na_k, na_v, pool_in, mla_cq, mla_ckv, mla_kpe,
     swa_q, swa_k, swa_v) = jnp.split(h @ p['w_in'], offs, axis=-1)
    heads = lambda z, nh: z.reshape(b, n, nh, -1)
    na = (rms_norm(heads(na_q, NA_HEADS), p['na_q_norm']),
          rms_norm(heads(na_k, NA_HEADS), p['na_k_norm']),
          heads(na_v, NA_HEADS))
    mla_q = rms_norm(heads(rms_norm(mla_cq, p['mla_q_a_norm']) @ p['mla_w_uq'], MLA_HEADS), p['mla_q_norm'])
    ckv = rms_norm(mla_ckv, p['mla_kv_norm'])
    swa = (rms_norm(heads(swa_q, SWA_HEADS), p['swa_q_norm']),
           rms_norm(heads(swa_k, SWA_KV_HEADS), p['swa_k_norm']),
           heads(swa_v, SWA_KV_HEADS))
    return na, pool_in, mla_q, ckv, mla_kpe, swa


def mla_expand_kv(ckv, kpe, p):
    b, n, _ = ckv.shape
    kv = (ckv @ p['mla_w_ukv']).reshape(b, n, MLA_HEADS, MLA_NOPE + MLA_V)
    k_pe = jnp.broadcast_to(kpe[:, :, None, :], (b, n, MLA_HEADS, MLA_ROPE))
    k = rms_norm(jnp.concatenate([kv[..., :MLA_NOPE], k_pe], axis=-1), p['mla_k_norm'])
    return k, kv[..., MLA_NOPE:]


def channel_mlp(x, shift, scale, gate, p):
    h = rms_norm(x, p['norm2']) * (1 + scale) + shift
    z = jax.nn.relu(h @ p['mlp_w1'])
    return x + gate * ((z * z) @ p['mlp_w2'])


def context_layer(x, mod, p):
    sh1, sc1, g1, sh2, sc2, g2 = mod
    h = rms_norm(x, p['norm1']) * (1 + sc1) + sh1
    b, n, _ = h.shape
    (na_q, na_k, na_v), pool_in, mla_q, ckv, kpe, (sq, sk, sv) = project_mixers(h, p)
    o_na = dense_attention(na_q, na_k, na_v)
    o_pool = multiscale_pool(pool_in, p['pool_w'], p['pool_scale'])
    mk, mv = mla_expand_kv(ckv, kpe, p)
    o_mla = dense_attention(mla_q, mk, mv)
    rep = SWA_HEADS // SWA_KV_HEADS
    o_swa = dense_attention(sq, jnp.repeat(sk, rep, axis=2), jnp.repeat(sv, rep, axis=2), sink=p['swa_sink'])
    mixed = jnp.concatenate([o_na.reshape(b, n, -1), o_pool, o_mla.reshape(b, n, -1), o_swa.reshape(b, n, -1)], axis=-1)
    x = x + g1 * (mixed @ p['w_out'])
    return channel_mlp(x, sh2, sc2, g2, p), (na_k, na_v, ckv, kpe, sk, sv)


def latent_layer(x, mod, cache, rows, cols, p):
    sh1, sc1, g1, sh2, sc2, g2 = mod
    c_na_k, c_na_v, c_ckv, c_kpe, c_swa_k, c_swa_v = cache
    h = rms_norm(x, p['norm1']) * (1 + sc1) + sh1
    b, n, _ = h.shape
    (na_q, na_k, na_v), pool_in, mla_q, ckv, kpe, (sq, sk, sv) = project_mixers(h, p)
    o_na = neighbourhood_attention(na_q, na_k, na_v, c_na_k, c_na_v, p['na_rpb'])
    o_pool = multiscale_pool(pool_in, p['pool_w'], p['pool_scale'])
    mk, mv = mla_expand_kv(ckv, kpe, p)
    ck, cv = mla_expand_kv(c_ckv, c_kpe, p)
    q = rope_tail(mla_q, MLA_NOPE, rows, cols)
    mk = rope_tail(mk, MLA_NOPE, rows, cols)
    o_mla = dense_attention(q, jnp.concatenate([mk, ck], axis=1), jnp.concatenate([mv, cv], axis=1))
    o_swa = window_attention(axial_rope(sq, rows, cols), axial_rope(sk, rows, cols), sv, c_swa_k, c_swa_v, p['swa_sink'])
    mixed = jnp.concatenate([o_na.reshape(b, n, -1), o_pool, o_mla.reshape(b, n, -1), o_swa.reshape(b, n, -1)], axis=-1)
    x = x + g1 * (mixed @ p['w_out'])
    return channel_mlp(x, sh2, sc2, g2, p)


def setup_inputs(seed: int = 0) -> dict:
    key = jax.random.key(seed)
    keys = jax.random.split(key, 40)
    ki = iter(range(40))

    def normal(shape, scale):
        return jax.random.normal(keys[next(ki)], shape, jnp.float32) * scale

    def gain(shape):
        return 1.0 + normal(shape, 0.05)

    d = D_MODEL
    return {
        'x_prompt': normal((BATCH, SEQ, d), 1.0),
        'x_sample': normal((DEC_BATCH, DEC_SEQ, d), 1.0),
        'cache_na_k': normal((DEC_BATCH, DEPTH, PAST_LEN, NA_HEADS, HEAD_DIM), 1.0),
        'cache_na_v': normal((DEC_BATCH, DEPTH, PAST_LEN, NA_HEADS, HEAD_DIM), 1.0),
        'cache_mla_ckv': normal((DEC_BATCH, DEPTH, PAST_LEN, MLA_KV_RANK), 1.0),
        'cache_mla_kpe': normal((DEC_BATCH, DEPTH, PAST_LEN, MLA_ROPE), 1.0),
        'cache_swa_k': normal((DEC_BATCH, DEPTH, PAST_LEN, SWA_KV_HEADS, HEAD_DIM), 1.0),
        'cache_swa_v': normal((DEC_BATCH, DEPTH, PAST_LEN, SWA_KV_HEADS, HEAD_DIM), 1.0),
        'c': normal((DEC_BATCH, d), 1.0),
        'c_ctx': normal((d,), 1.0),
        'norm1': gain((DEPTH, d)),
        'norm2': gain((DEPTH, d)),
        'w_ada': normal((DEPTH, d, 6 * d), 0.5 * d ** -0.5),
        'b_ada': normal((DEPTH, 6 * d), 0.02),
        'w_in': normal((DEPTH, d, IN_COLS), d ** -0.5),
        'w_out': normal((DEPTH, MIX_WIDTH, d), MIX_WIDTH ** -0.5),
        'na_q_norm': gain((DEPTH, HEAD_DIM)),
        'na_k_norm': gain((DEPTH, HEAD_DIM)),
        'na_rpb': normal((DEPTH, NA_HEADS, 2 * NA_WIN_ROWS - 1, 2 * NA_WIN_COLS - 1), 0.5),
        'pool_w': normal((DEPTH, POOL_GROUPS, POOL_GW, POOL_GW), POOL_GW ** -0.5),
        'pool_scale': gain((DEPTH, POOL_GROUPS * POOL_GW)),
        'mla_q_a_norm': gain((DEPTH, MLA_Q_RANK)),
        'mla_w_uq': normal((DEPTH, MLA_Q_RANK, MLA_HEADS * (MLA_NOPE + MLA_ROPE)), MLA_Q_RANK ** -0.5),
        'mla_kv_norm': gain((DEPTH, MLA_KV_RANK)),
        'mla_w_ukv': normal((DEPTH, MLA_KV_RANK, MLA_HEADS * (MLA_NOPE + MLA_V)), MLA_KV_RANK ** -0.5),
        'mla_q_norm': gain((DEPTH, MLA_NOPE + MLA_ROPE)),
        'mla_k_norm': gain((DEPTH, MLA_NOPE + MLA_ROPE)),
        'swa_q_norm': gain((DEPTH, HEAD_DIM)),
        'swa_k_norm': gain((DEPTH, HEAD_DIM)),
        'swa_sink': normal((DEPTH, SWA_HEADS), 1.0),
        'mlp_w1': normal((DEPTH, d, D_FF), d ** -0.5),
        'mlp_w2': normal((DEPTH, D_FF, d), D_FF ** -0.5),
    }


def reference(x_prompt, x_sample, cache_na_k, cache_na_v, cache_mla_ckv, cache_mla_kpe, cache_swa_k, cache_swa_v,
              c, c_ctx, norm1, norm2, w_ada, b_ada, w_in, w_out, na_q_norm, na_k_norm, na_rpb, pool_w, pool_scale,
              mla_q_a_norm, mla_w_uq, mla_kv_norm, mla_w_ukv, mla_q_norm, mla_k_norm, swa_q_norm, swa_k_norm,
              swa_sink, mlp_w1, mlp_w2):
    n_lat = x_sample.shape[1]
    t = jnp.arange(n_lat, dtype=jnp.int32)
    rows, cols = t // GRID_W, t % GRID_W
    xp, xs = x_prompt, x_sample
    st_na_k, st_na_v, st_ckv, st_kpe, st_swa_k, st_swa_v = [], [], [], [], [], []
    for l in range(DEPTH):
        p = {
            'norm1': norm1[l], 'norm2': norm2[l], 'w_in': w_in[l], 'w_out': w_out[l],
            'na_q_norm': na_q_norm[l], 'na_k_norm': na_k_norm[l], 'na_rpb': na_rpb[l],
            'pool_w': pool_w[l], 'pool_scale': pool_scale[l],
            'mla_q_a_norm': mla_q_a_norm[l], 'mla_w_uq': mla_w_uq[l], 'mla_kv_norm': mla_kv_norm[l],
            'mla_w_ukv': mla_w_ukv[l], 'mla_q_norm': mla_q_norm[l], 'mla_k_norm': mla_k_norm[l],
            'swa_q_norm': swa_q_norm[l], 'swa_k_norm': swa_k_norm[l], 'swa_sink': swa_sink[l],
            'mlp_w1': mlp_w1[l], 'mlp_w2': mlp_w2[l],
        }
        mod_ctx = adaln(c_ctx, w_ada[l], b_ada[l])
        xp, (k_na, v_na, ckv, kpe, k_swa, v_swa) = context_layer(xp, mod_ctx, p)
        st_na_k.append(k_na)
        st_na_v.append(v_na)
        st_ckv.append(ckv)
        st_kpe.append(kpe)
        st_swa_k.append(k_swa)
        st_swa_v.append(v_swa)
        mod_lat = adaln(c[:, None, :], w_ada[l], b_ada[l])
        cache_l = (cache_na_k[:, l], cache_na_v[:, l], cache_mla_ckv[:, l], cache_mla_kpe[:, l],
                   cache_swa_k[:, l], cache_swa_v[:, l])
        xs = latent_layer(xs, mod_lat, cache_l, rows, cols, p)
    na_k_new = jnp.stack(st_na_k, axis=1)
    na_v_new = jnp.stack(st_na_v, axis=1)
    mla_ckv_new = jnp.stack(st_ckv, axis=1)
    mla_kpe_new = jnp.stack(st_kpe, axis=1)
    swa_k_new = jnp.stack(st_swa_k, axis=1)
    swa_v_new = jnp.stack(st_swa_v, axis=1)
    return (xp, xs, na_k_new, na_v_new, mla_ckv_new, mla_kpe_new, swa_k_new, swa_v_new)
```

```python
import functools

import numpy as np
import jax
import jax.numpy as jnp
from jax import lax
from jax.experimental import pallas as pl
from jax.experimental.pallas import tpu as pltpu

F32, BF16 = jnp.float32, jnp.bfloat16

D_MODEL = 1024
DEPTH = 4
GRID_W = 64
HEAD_DIM = 64
NA_HEADS = 4
NA_WIN_ROWS = 8
NA_WIN_COLS = 16
POOL_GROUPS = 4
POOL_GW = 64
POOL_WINDOWS = (2, 4, 8, 16)
MLA_HEADS = 4
MLA_Q_RANK = 256
MLA_KV_RANK = 128
MLA_NOPE = 64
MLA_ROPE = 32
MLA_V = 64
MLA_QK = MLA_NOPE + MLA_ROPE
MLA_PAD = 128
SWA_HEADS = 4
SWA_KV_HEADS = 2
SWA_WINDOW = 128
D_FF = 4 * D_MODEL
ROPE_THETA = 10000.0
EPS = 1e-6
NEG = -1e30

TOKEN_TILE = 256
Q_TILE = 256
NA_TILE_ROWS = Q_TILE // GRID_W
NA_KEY_ROWS = NA_TILE_ROWS + NA_WIN_ROWS
SWA_KEYS = Q_TILE + 2 * SWA_WINDOW
ADA_COLS = 2048
FF_CHUNK = 1024
W_IN_COLS = 2048
VMEM_LIMIT = 48 * 1024 * 1024

_C_NA_Q, _C_NA_K, _C_NA_V, _C_POOL, _C_CQ, _C_CKV, _C_SWA_Q, _C_SWA_K, _C_SWA_V, _C_KPE = (
    0, 256, 512, 768, 1024, 1280, 1408, 1664, 1792, 1920)


def _params(n_grid):
    return pltpu.CompilerParams(dimension_semantics=("arbitrary",) * n_grid, vmem_limit_bytes=VMEM_LIMIT)


def _layer_spec(arr, l, n_grid):
    nd = arr.ndim
    shape = (None,) + tuple(arr.shape[1:])
    if n_grid == 1:
        return pl.BlockSpec(shape, lambda i: (l,) + (0,) * (nd - 1))
    return pl.BlockSpec(shape, lambda b, i: (l,) + (0,) * (nd - 1))


def _const_spec(arr, n_grid):
    nd = arr.ndim
    if n_grid == 1:
        return pl.BlockSpec(arr.shape, lambda i: (0,) * nd)
    return pl.BlockSpec(arr.shape, lambda b, i: (0,) * nd)


def _rms_full(x, g):
    return x * lax.rsqrt(jnp.mean(x * x, axis=-1, keepdims=True) + EPS) * g


def _split_dot(s, m):
    hi = s.astype(BF16)
    lo = (s - hi.astype(F32)).astype(BF16)
    return jnp.dot(hi, m, preferred_element_type=F32) + jnp.dot(lo, m, preferred_element_type=F32)


def _rms_heads(x, g, m, dim):
    ss = _split_dot(x * x, m)
    return x * lax.rsqrt(ss * (1.0 / dim) + EPS) * g


def _rope(x, c_ref, sa_ref, sb_ref, quarter):
    w = x.shape[-1]
    return (x * c_ref[...] + pltpu.roll(x, w - quarter, 1) * sa_ref[...]
            + pltpu.roll(x, quarter, 1) * sb_ref[...])


def _adaln_kernel(cond_ref, w_ref, b_ref, o_ref):
    s = jax.nn.silu(cond_ref[...]).astype(BF16)
    o_ref[...] = jnp.dot(s, w_ref[...].astype(BF16), preferred_element_type=F32) + b_ref[...]


def _adaln(cond, w_ada, b_ada):
    n_out = w_ada.shape[-1]
    return pl.pallas_call(
        _adaln_kernel,
        grid=(DEPTH, n_out // ADA_COLS),
        in_specs=[pl.BlockSpec((8, D_MODEL), lambda l, j: (0, 0)),
                  pl.BlockSpec((None, D_MODEL, ADA_COLS), lambda l, j: (l, 0, j)),
                  pl.BlockSpec((None, 1, ADA_COLS), lambda l, j: (l, 0, j))],
        out_specs=pl.BlockSpec((None, 8, ADA_COLS), lambda l, j: (l, 0, j)),
        out_shape=jax.ShapeDtypeStruct((DEPTH, 8, n_out), F32),
        compiler_params=_params(2),
        name="adaln",
    )(cond, w_ada, b_ada.reshape(DEPTH, 1, n_out))


def _pre_kernel(*refs, rope, states):
    (x_ref, mod_ref, n1_ref, win_ref, wuq_ref, wuk_ref, wuv_ref, e_ref, m256_ref, m128_ref, m512_ref,
     g_naq, g_nak, g_qa, g_kv, g_mq, g_mk, g_sq, g_sk) = refs[:19]
    pos = 19
    if rope:
        cm, sam, sbm, cs, sas, sbs = refs[pos:pos + 6]
        pos += 6
    (o_naq, o_nak, o_nav, o_pool, o_mq, o_mk, o_mv, o_sq, o_sk, o_sv) = refs[pos:pos + 10]
    pos += 10
    if states:
        s_nak, s_nav, s_ckv, s_kpe, s_sk, s_sv = refs[pos:pos + 6]

    x = x_ref[...]
    mod = mod_ref[...]
    sh1, sc1 = mod[:, 0:D_MODEL], mod[:, D_MODEL:2 * D_MODEL]
    h = _rms_full(x, n1_ref[...]) * (1.0 + sc1) + sh1
    y = jnp.dot(h.astype(BF16), win_ref[...], preferred_element_type=F32)

    m256 = m256_ref[...]
    na_q = _rms_heads(y[:, _C_NA_Q:_C_NA_Q + 256], g_naq[...], m256, HEAD_DIM)
    na_k = _rms_heads(y[:, _C_NA_K:_C_NA_K + 256], g_nak[...], m256, HEAD_DIM)
    na_v = y[:, _C_NA_V:_C_NA_V + 256]
    o_naq[...] = na_q.astype(BF16)
    o_nak[...] = na_k.astype(BF16)
    o_nav[...] = na_v.astype(BF16)
    o_pool[...] = y[:, _C_POOL:_C_POOL + 256]

    m512 = m512_ref[...]
    cq = _rms_full(y[:, _C_CQ:_C_CQ + MLA_Q_RANK], g_qa[...])
    mq = jnp.dot(cq.astype(BF16), wuq_ref[...], preferred_element_type=F32)
    mq = _rms_heads(mq, g_mq[...], m512, MLA_QK)
    ckv = _rms_full(y[:, _C_CKV:_C_CKV + MLA_KV_RANK], g_kv[...])
    kpe = y[:, _C_KPE:_C_KPE + MLA_ROPE]
    ckv_b = ckv.astype(BF16)
    mk = jnp.dot(ckv_b, wuk_ref[...], preferred_element_type=F32) + _split_dot(kpe, e_ref[...])
    mk = _rms_heads(mk, g_mk[...], m512, MLA_QK)
    mv = jnp.dot(ckv_b, wuv_ref[...], preferred_element_type=F32)

    sq = _rms_heads(y[:, _C_SWA_Q:_C_SWA_Q + 256], g_sq[...], m256, HEAD_DIM)
    sk = _rms_heads(y[:, _C_SWA_K:_C_SWA_K + 128], g_sk[...], m128_ref[...], HEAD_DIM)
    sv = y[:, _C_SWA_V:_C_SWA_V + 128]
    if states:
        s_nak[...] = na_k
        s_nav[...] = na_v
        s_ckv[...] = ckv
        s_kpe[...] = kpe
        s_sk[...] = sk
        s_sv[...] = sv
    if rope:
        mq = _rope(mq, cm, sam, sbm, MLA_ROPE // 4)
        mk = _rope(mk, cm, sam, sbm, MLA_ROPE // 4)
        sq = _rope(sq, cs, sas, sbs, HEAD_DIM // 4)
        sk = (sk * cs[:, 0:128] + pltpu.roll(sk, 128 - HEAD_DIM // 4, 1) * sas[:, 0:128]
              + pltpu.roll(sk, HEAD_DIM // 4, 1) * sbs[:, 0:128])
    o_mq[...] = mq.astype(BF16)
    o_mk[...] = mk.astype(BF16)
    o_mv[...] = mv.astype(BF16)
    o_sq[...] = sq.astype(BF16)
    o_sk[...] = sk.astype(BF16)
    o_sv[...] = sv.astype(BF16)


def _pre(x, mod, l, wts, consts, rope_tabs, *, tokens_per_mod, first_mod_row, states, name):
    t = x.shape[0]
    tm = TOKEN_TILE
    rope = rope_tabs is not None
    tiles_per_mod = tokens_per_mod // tm
    n_pos = rope_tabs[0].shape[0] // tm if rope else 1

    def row(width, dtype):
        return jax.ShapeDtypeStruct((t, width), dtype)

    def tok(width):
        return pl.BlockSpec((tm, width), lambda i: (i, 0))

    in_specs = [tok(D_MODEL),
                pl.BlockSpec((None, None, 1, 6 * D_MODEL), lambda i: (l, first_mod_row + i // tiles_per_mod, 0, 0)),
                _layer_spec(wts["norm1"], l, 1), _layer_spec(wts["w_in"], l, 1), _layer_spec(wts["w_uq"], l, 1),
                _layer_spec(wts["w_uk"], l, 1), _layer_spec(wts["w_uv"], l, 1),
                _const_spec(consts["e_kpe"], 1), _const_spec(consts["m256"], 1), _const_spec(consts["m128"], 1),
                _const_spec(consts["m512"], 1)]
    args = [x, mod, wts["norm1"], wts["w_in"], wts["w_uq"], wts["w_uk"], wts["w_uv"],
            consts["e_kpe"], consts["m256"], consts["m128"], consts["m512"]]
    for nm in ("g_naq", "g_nak", "g_qa", "g_kv", "g_mq", "g_mk", "g_sq", "g_sk"):
        in_specs.append(_layer_spec(wts[nm], l, 1))
        args.append(wts[nm])
    if rope:
        for tab in rope_tabs:
            in_specs.append(pl.BlockSpec((tm, tab.shape[1]), lambda i: (i % n_pos, 0)))
            args.append(tab)
    widths = (256, 256, 256, 256, 512, 512, 256, 256, 128, 128)
    dtypes = (BF16, BF16, BF16, F32, BF16, BF16, BF16, BF16, BF16, BF16)
    out_shape = [row(w, d) for w, d in zip(widths, dtypes)]
    out_specs = [tok(w) for w in widths]
    if states:
        for w in (256, 256, MLA_KV_RANK, MLA_ROPE, 128, 128):
            out_shape.append(row(w, F32))
            out_specs.append(tok(w))
    return pl.pallas_call(
        functools.partial(_pre_kernel, rope=rope, states=states),
        grid=(t // tm,), in_specs=in_specs, out_specs=out_specs, out_shape=out_shape,
        compiler_params=_params(1), name=name,
    )(*args)


def _mla_cache_kernel(ckv_ref, kpe_ref, wuk_ref, wuv_ref, e_ref, m512_ref, g_mk, ok_ref, ov_ref):
    ckv_b = ckv_ref[...].astype(BF16)
    mk = jnp.dot(ckv_b, wuk_ref[...], preferred_element_type=F32) + _split_dot(kpe_ref[...], e_ref[...])
    ok_ref[...] = _rms_heads(mk, g_mk[...], m512_ref[...], MLA_QK).astype(BF16)
    ov_ref[...] = jnp.dot(ckv_b, wuv_ref[...], preferred_element_type=F32).astype(BF16)


def _mla_cache(c_ckv, c_kpe, wts, consts):
    b, _, n, _ = c_ckv.shape

    def lay(arr):
        nd = arr.ndim
        return pl.BlockSpec((None,) + tuple(arr.shape[1:]), lambda l, i: (l,) + (0,) * (nd - 1))

    return pl.pallas_call(
        _mla_cache_kernel,
        grid=(DEPTH, b),
        in_specs=[pl.BlockSpec((None, None, n, MLA_KV_RANK), lambda l, i: (i, l, 0, 0)),
                  pl.BlockSpec((None, None, n, MLA_ROPE), lambda l, i: (i, l, 0, 0)),
                  lay(wts["w_uk"]), lay(wts["w_uv"]), _const_spec(consts["e_kpe"], 2),
                  _const_spec(consts["m512"], 2), lay(wts["g_mk"])],
        out_specs=[pl.BlockSpec((None, None, n, MLA_HEADS * MLA_PAD), lambda l, i: (l, i, 0, 0)),
                   pl.BlockSpec((None, None, n, MLA_HEADS * MLA_V), lambda l, i: (l, i, 0, 0))],
        out_shape=[jax.ShapeDtypeStruct((DEPTH, b, n, MLA_HEADS * MLA_PAD), BF16),
                   jax.ShapeDtypeStruct((DEPTH, b, n, MLA_HEADS * MLA_V), BF16)],
        compiler_params=_params(2), name="mla_cache",
    )(c_ckv, c_kpe, wts["w_uk"], wts["w_uv"], consts["e_kpe"], consts["m512"], wts["g_mk"])


_NA_VARIANTS = ((0, 0), (4, 0), (24, 20), (28, 20))


def _na_bias_kernel(rpb_ref, o_ref, *, grid_rows):
    l, hd = pl.program_id(0), pl.program_id(1)
    n_dr, n_dc = 2 * NA_WIN_ROWS - 1, 2 * NA_WIN_COLS - 1
    base = (l * NA_HEADS + hd) * (n_dr * n_dc)
    c = lax.broadcasted_iota(jnp.int32, (GRID_W, GRID_W), 0)
    kc = lax.broadcasted_iota(jnp.int32, (GRID_W, GRID_W), 1)
    dcv = kc - c + (NA_WIN_COLS - 1)
    col_start = jnp.clip(c - NA_WIN_COLS // 2, 0, GRID_W - NA_WIN_COLS)
    col_ok = (kc >= col_start) & (kc < col_start + NA_WIN_COLS)
    neg = jnp.full((GRID_W, GRID_W), NEG, F32)
    blocks = {}

    def block(dr):
        if dr not in blocks:
            t = jnp.zeros((GRID_W, GRID_W), F32)
            for dc in range(n_dc):
                t = jnp.where(dcv == dc, rpb_ref[base + dr * n_dc + dc], t)
            blocks[dr] = jnp.where(col_ok, t, NEG)
        return blocks[dr]

    for v, (r0, w0) in enumerate(_NA_VARIANTS):
        for rl in range(NA_TILE_ROWS):
            r = r0 + rl
            kr0 = min(max(r - NA_WIN_ROWS // 2, 0), grid_rows - NA_WIN_ROWS)
            for j in range(NA_KEY_ROWS):
                kr = w0 + j
                inside = kr0 <= kr < kr0 + NA_WIN_ROWS
                o_ref[v, rl * GRID_W:(rl + 1) * GRID_W, j * GRID_W:(j + 1) * GRID_W] = (
                    block(kr - r + NA_WIN_ROWS - 1) if inside else neg)


def _na_bias(na_rpb, grid_rows):
    n_k = NA_KEY_ROWS * GRID_W
    return pl.pallas_call(
        functools.partial(_na_bias_kernel, grid_rows=grid_rows),
        grid=(DEPTH, NA_HEADS),
        in_specs=[pl.BlockSpec(memory_space=pltpu.SMEM)],
        out_specs=pl.BlockSpec((None, len(_NA_VARIANTS), None, Q_TILE, n_k), lambda l, h: (l, 0, h, 0, 0)),
        out_shape=jax.ShapeDtypeStruct((DEPTH, len(_NA_VARIANTS), NA_HEADS, Q_TILE, n_k), F32),
        compiler_params=_params(2), name="na_bias",
    )(na_rpb.reshape(-1))


def _attn_kernel(*refs, tq, wk, wv, dl, n_slabs, scale, mode, has_ctx, has_sink, n_keys):
    q_ref, k_ref, v_ref = refs[:3]
    pos = 3
    if has_ctx:
        ck_ref, cv_ref = refs[pos:pos + 2]
        pos += 2
    if mode == "na":
        bias_ref = refs[pos]
        pos += 1
    if has_sink:
        sink_ref = refs[pos]
        pos += 1
    o_ref = refs[pos]
    i = pl.program_id(1)
    hs = wk // dl

    valid = None
    if mode == "full":
        kl, vl = k_ref[...], v_ref[...]
    elif mode == "swa":
        w0 = pl.multiple_of(jnp.clip(i * tq - SWA_WINDOW, 0, n_keys - SWA_KEYS), SWA_WINDOW)
        kl, vl = k_ref[pl.ds(w0, SWA_KEYS), :], v_ref[pl.ds(w0, SWA_KEYS), :]
        qpos = i * tq + lax.broadcasted_iota(jnp.int32, (tq, SWA_KEYS), 0)
        kpos = w0 + lax.broadcasted_iota(jnp.int32, (tq, SWA_KEYS), 1)
        valid = jnp.abs(kpos - qpos) <= SWA_WINDOW
    else:
        last = n_keys // GRID_W - NA_KEY_ROWS
        w0 = pl.multiple_of(jnp.clip(i * NA_TILE_ROWS - NA_WIN_ROWS // 2, 0, last) * GRID_W, GRID_W)
        kl, vl = k_ref[pl.ds(w0, NA_KEY_ROWS * GRID_W), :], v_ref[pl.ds(w0, NA_KEY_ROWS * GRID_W), :]
    if has_ctx:
        ck, cv = ck_ref[...], cv_ref[...]

    lane_k = lax.broadcasted_iota(jnp.int32, (1, wk), 1)
    lane_v = lax.broadcasted_iota(jnp.int32, (1, wv), 1)
    nt = (((1,), (1,)), ((), ()))
    for s in range(n_slabs):
        qs = q_ref[:, s * wk:(s + 1) * wk]
        acc = jnp.zeros((tq, wv), F32)
        for j in range(hs):
            head = j * n_slabs + s
            qm = jnp.where((lane_k >= j * dl) & (lane_k < (j + 1) * dl), qs, jnp.zeros_like(qs))
            s_loc = lax.dot_general(qm, kl, nt, preferred_element_type=F32) * scale
            if mode == "na":
                s_loc = s_loc + bias_ref[head]
            if valid is not None:
                s_loc = jnp.where(valid, s_loc, NEG)
            m = jnp.max(s_loc, axis=-1, keepdims=True)
            if has_ctx:
                s_ctx = lax.dot_general(qm, ck, nt, preferred_element_type=F32) * scale
                m = jnp.maximum(m, jnp.max(s_ctx, axis=-1, keepdims=True))
            if has_sink:
                m = jnp.maximum(m, sink_ref[head])
            p_loc = jnp.exp(s_loc - m)
            den = jnp.sum(p_loc, axis=-1, keepdims=True)
            o = jnp.dot(p_loc.astype(BF16), vl, preferred_element_type=F32)
            if has_ctx:
                p_ctx = jnp.exp(s_ctx - m)
                den = den + jnp.sum(p_ctx, axis=-1, keepdims=True)
                o = o + jnp.dot(p_ctx.astype(BF16), cv, preferred_element_type=F32)
            if has_sink:
                den = den + jnp.exp(sink_ref[head] - m)
            out_lanes = (lane_v >= j * HEAD_DIM) & (lane_v < (j + 1) * HEAD_DIM)
            acc = acc + jnp.where(out_lanes, o * (1.0 / den), 0.0)
        o_ref[:, s * wv:(s + 1) * wv] = acc.astype(o_ref.dtype)


def _attention(q, k, v, *, dl, scale, mode="full", ck=None, cv=None, ctx_layer=None, bias=None, bias_layer=None,
               sink=None, name):
    b, n, wq = q.shape
    _, nk, wk = k.shape
    wv = v.shape[-1]
    n_slabs = wq // wk
    tq = min(Q_TILE, n)
    in_specs = [pl.BlockSpec((None, tq, wq), lambda bi, i: (bi, i, 0)),
                pl.BlockSpec((None, nk, wk), lambda bi, i: (bi, 0, 0)),
                pl.BlockSpec((None, nk, wv), lambda bi, i: (bi, 0, 0))]
    args = [q, k, v]
    if ck is not None:
        n_ctx = ck.shape[2]
        in_specs += [pl.BlockSpec((None, None, n_ctx, wk), lambda bi, i: (ctx_layer, bi, 0, 0)),
                     pl.BlockSpec((None, None, n_ctx, wv), lambda bi, i: (ctx_layer, bi, 0, 0))]
        args += [ck, cv]
    if mode == "na":
        n_tiles = n // tq

        def variant(bi, i):
            v_idx = jnp.where(i == 0, 0, jnp.where(i <= n_tiles - 3, 1, i - (n_tiles - 4)))
            return (bias_layer, v_idx, 0, 0, 0)

        in_specs.append(pl.BlockSpec((None, None) + tuple(bias.shape[2:]), variant))
        args.append(bias)
    if sink is not None:
        in_specs.append(pl.BlockSpec(memory_space=pltpu.SMEM))
        args.append(sink)
    return pl.pallas_call(
        functools.partial(_attn_kernel, tq=tq, wk=wk, wv=wv, dl=dl, n_slabs=n_slabs, scale=scale, mode=mode,
                          has_ctx=ck is not None, has_sink=sink is not None, n_keys=nk),
        grid=(b, n // tq), in_specs=in_specs,
        out_specs=pl.BlockSpec((None, tq, n_slabs * wv), lambda bi, i: (bi, i, 0)),
        out_shape=jax.ShapeDtypeStruct((b, n, n_slabs * wv), BF16),
        compiler_params=_params(2), name=name,
    )(*args)


def _pool_kernel(x_ref, w_ref, s_ref, o_ref, *, n):
    x = x_ref[...]
    t = lax.broadcasted_iota(jnp.int32, x.shape, 0)
    lane = lax.broadcasted_iota(jnp.int32, x.shape, 1)

    def later(a, d):
        return jnp.where(t < n - d, pltpu.roll(a, n - d, 0), 0.0)

    def earlier(a, d):
        return jnp.where(t >= d, pltpu.roll(a, d, 0), 0.0)

    fwd, bwd = {1: x}, {1: x}
    for k in (1, 2, 4):
        fwd[2 * k] = fwd[k] + later(fwd[k], k)
        bwd[2 * k] = bwd[k] + earlier(bwd[k], k)
    pooled = jnp.zeros_like(x)
    for gi, w in enumerate(POOL_WINDOWS):
        total = fwd[w // 2] + earlier(bwd[w // 2], 1)
        count = (jnp.clip(t + w // 2, 0, n) - jnp.clip(t - w // 2, 0, n)).astype(F32)
        in_group = (lane >= gi * POOL_GW) & (lane < (gi + 1) * POOL_GW)
        pooled = jnp.where(in_group, total / count - x, pooled)
    y = jnp.dot(pooled.astype(BF16), w_ref[...], preferred_element_type=F32)
    o_ref[...] = (y * s_ref[...]).astype(o_ref.dtype)


def _pool(x, wts, l, name):
    b, n, w = x.shape
    return pl.pallas_call(
        functools.partial(_pool_kernel, n=n),
        grid=(b,),
        in_specs=[pl.BlockSpec((None, n, w), lambda i: (i, 0, 0)),
                  _layer_spec(wts["pool_w"], l, 1), _layer_spec(wts["pool_scale"], l, 1)],
        out_specs=pl.BlockSpec((None, n, w), lambda i: (i, 0, 0)),
        out_shape=jax.ShapeDtypeStruct((b, n, w), BF16),
        compiler_params=_params(1), name=name,
    )(x, wts["pool_w"], wts["pool_scale"])


def _post_kernel(x_ref, p0, p1, p2, p3, mod_ref, n2_ref, wo_ref, w1_ref, w2_ref, o_ref):
    x = x_ref[...]
    mod = mod_ref[...]
    g1 = mod[:, 2 * D_MODEL:3 * D_MODEL]
    sh2, sc2, g2 = (mod[:, 3 * D_MODEL:4 * D_MODEL], mod[:, 4 * D_MODEL:5 * D_MODEL],
                    mod[:, 5 * D_MODEL:6 * D_MODEL])
    a = jnp.zeros(x.shape, F32)
    for i, p in enumerate((p0, p1, p2, p3)):
        a = a + jnp.dot(p[...], wo_ref[i * 256:(i + 1) * 256, :], preferred_element_type=F32)
    x = x + g1 * a
    h = (_rms_full(x, n2_ref[...]) * (1.0 + sc2) + sh2).astype(BF16)
    acc = jnp.zeros(x.shape, F32)
    for c in range(D_FF // FF_CHUNK):
        z = jnp.maximum(jnp.dot(h, w1_ref[:, c * FF_CHUNK:(c + 1) * FF_CHUNK], preferred_element_type=F32), 0.0)
        acc = acc + jnp.dot((z * z).astype(BF16), w2_ref[c * FF_CHUNK:(c + 1) * FF_CHUNK, :],
                            preferred_element_type=F32)
    o_ref[...] = x + g2 * acc


def _post(x, parts, mod, l, wts, *, tokens_per_mod, first_mod_row, name):
    t = x.shape[0]
    tm = TOKEN_TILE
    tiles_per_mod = tokens_per_mod // tm

    def tok(width):
        return pl.BlockSpec((tm, width), lambda i: (i, 0))

    def resident(arr):
        nd = arr.ndim
        return pl.BlockSpec((None,) + tuple(arr.shape[1:]), lambda i: (l,) + (0,) * (nd - 1),
                            pipeline_mode=pl.Buffered(1))

    return pl.pallas_call(
        _post_kernel,
        grid=(t // tm,),
        in_specs=[tok(D_MODEL), tok(256), tok(256), tok(256), tok(256),
                  pl.BlockSpec((None, None, 1, 6 * D_MODEL), lambda i: (l, first_mod_row + i // tiles_per_mod, 0, 0)),
                  _layer_spec(wts["norm2"], l, 1), resident(wts["w_out"]), resident(wts["w1"]), resident(wts["w2"])],
        out_specs=tok(D_MODEL),
        out_shape=jax.ShapeDtypeStruct((t, D_MODEL), F32),
        compiler_params=_params(1), name=name,
    )(x, *parts, mod, wts["norm2"], wts["w_out"], wts["w1"], wts["w2"])


def _block_ones(width, group):
    idx = np.arange(width) // group
    return jnp.asarray(idx[:, None] == idx[None, :], BF16)


def _rope_tables(n_tokens, width, head_lanes, first, rope_dims):
    half, quarter = rope_dims // 2, rope_dims // 4
    t = jnp.arange(n_tokens, dtype=jnp.int32)
    rows, cols = t // GRID_W, t % GRID_W
    freqs = ROPE_THETA ** (-jnp.arange(quarter, dtype=F32) / quarter)
    lane = np.arange(width) % head_lanes - first
    active = (lane >= 0) & (lane < rope_dims)
    lane = np.where(active, lane, 0)
    use_cols = lane >= half
    within = lane % half
    upper = within >= quarter
    f_idx = within % quarter
    pos = jnp.where(jnp.asarray(use_cols)[None, :], cols[:, None], rows[:, None]).astype(F32)
    ang = pos * freqs[jnp.asarray(f_idx)][None, :]
    act = jnp.asarray(active)[None, :]
    up = jnp.asarray(upper)[None, :]
    cos = jnp.where(act, jnp.cos(ang), 1.0)
    sin = jnp.where(act, jnp.sin(ang), 0.0)
    return cos, jnp.where(up, 0.0, -sin), jnp.where(up, sin, 0.0)


def _prepare(norm1, norm2, w_in, w_out, na_q_norm, na_k_norm, pool_w, pool_scale, mla_q_a_norm, mla_w_uq,
             mla_kv_norm, mla_w_ukv, mla_q_norm, mla_k_norm, swa_q_norm, swa_k_norm, mlp_w1, mlp_w2):
    d = DEPTH
    offs = np.cumsum((256, 256, 256, 256, MLA_Q_RANK, MLA_KV_RANK, MLA_ROPE, 256, 128, 128))[:-1]
    (w_naq, w_nak, w_nav, w_pool, w_cq, w_ckv, w_kpe, w_sq, w_sk, w_sv) = jnp.split(w_in, offs, axis=-1)
    w_sq = w_sq.reshape(d, D_MODEL, 2, 2, HEAD_DIM).swapaxes(2, 3).reshape(d, D_MODEL, 256)
    pad = jnp.zeros((d, D_MODEL, W_IN_COLS - _C_KPE - MLA_ROPE), w_in.dtype)
    w_in_r = jnp.concatenate([w_naq, w_nak, w_nav, w_pool, w_cq, w_ckv, w_sq, w_sk, w_sv, w_kpe, pad], axis=-1)

    def pad_heads(a):
        a = a.reshape(a.shape[:-1] + (MLA_HEADS, MLA_QK))
        a = jnp.pad(a, [(0, 0)] * (a.ndim - 1) + [(0, MLA_PAD - MLA_QK)])
        return a.reshape(a.shape[:-2] + (MLA_HEADS * MLA_PAD,))

    ukv = mla_w_ukv.reshape(d, MLA_KV_RANK, MLA_HEADS, MLA_NOPE + MLA_V)
    w_uk = jnp.pad(ukv[..., :MLA_NOPE], ((0, 0), (0, 0), (0, 0), (0, MLA_PAD - MLA_NOPE)))
    w_out_s = w_out[:, 768:1024].reshape(d, 2, 2, HEAD_DIM, D_MODEL).swapaxes(1, 2).reshape(d, 256, D_MODEL)
    eye = jnp.eye(POOL_GROUPS, dtype=pool_w.dtype)
    pool_bd = jnp.einsum("lgce,gh->lgche", pool_w, eye).reshape(d, 256, 256)

    def gain(g, reps):
        return jnp.tile(g, (1, reps)).reshape(d, 1, -1)

    return {
        "norm1": norm1.reshape(d, 1, D_MODEL), "norm2": norm2.reshape(d, 1, D_MODEL),
        "w_in": w_in_r.astype(BF16),
        "w_uq": pad_heads(mla_w_uq).astype(BF16),
        "w_uk": w_uk.reshape(d, MLA_KV_RANK, MLA_HEADS * MLA_PAD).astype(BF16),
        "w_uv": ukv[..., MLA_NOPE:].reshape(d, MLA_KV_RANK, MLA_HEADS * MLA_V).astype(BF16),
        "w_out": jnp.concatenate([w_out[:, :768], w_out_s], axis=1).astype(BF16),
        "w1": mlp_w1.astype(BF16), "w2": mlp_w2.astype(BF16),
        "pool_w": pool_bd.astype(BF16), "pool_scale": pool_scale.reshape(d, 1, 256),
        "g_naq": gain(na_q_norm, NA_HEADS), "g_nak": gain(na_k_norm, NA_HEADS),
        "g_qa": mla_q_a_norm.reshape(d, 1, MLA_Q_RANK), "g_kv": mla_kv_norm.reshape(d, 1, MLA_KV_RANK),
        "g_mq": pad_heads(jnp.tile(mla_q_norm, (1, MLA_HEADS))).reshape(d, 1, -1),
        "g_mk": pad_heads(jnp.tile(mla_k_norm, (1, MLA_HEADS))).reshape(d, 1, -1),
        "g_sq": gain(swa_q_norm, SWA_HEADS), "g_sk": gain(swa_k_norm, SWA_KV_HEADS),
    }


def _constants():
    e = np.zeros((MLA_ROPE, MLA_HEADS * MLA_PAD), np.float32)
    for h in range(MLA_HEADS):
        e[np.arange(MLA_ROPE), h * MLA_PAD + MLA_NOPE + np.arange(MLA_ROPE)] = 1.0
    return {"e_kpe": jnp.asarray(e, BF16), "m256": _block_ones(256, HEAD_DIM), "m128": _block_ones(128, HEAD_DIM),
            "m512": _block_ones(MLA_HEADS * MLA_PAD, MLA_PAD)}


def kernel(x_prompt, x_sample, cache_na_k, cache_na_v, cache_mla_ckv, cache_mla_kpe, cache_swa_k, cache_swa_v, c, c_ctx, norm1, norm2, w_ada, b_ada, w_in, w_out, na_q_norm, na_k_norm, na_rpb, pool_w, pool_scale, mla_q_a_norm, mla_w_uq, mla_kv_norm, mla_w_ukv, mla_q_norm, mla_k_norm, swa_q_norm, swa_k_norm, swa_sink, mlp_w1, mlp_w2):
    bc, nc, _ = x_prompt.shape
    bl, nl, _ = x_sample.shape
    n_past = cache_na_k.shape[2]
    wts = _prepare(norm1, norm2, w_in, w_out, na_q_norm, na_k_norm, pool_w, pool_scale, mla_q_a_norm, mla_w_uq,
                   mla_kv_norm, mla_w_ukv, mla_q_norm, mla_k_norm, swa_q_norm, swa_k_norm, mlp_w1, mlp_w2)
    consts = _constants()
    rope_tabs = (_rope_tables(nl, MLA_HEADS * MLA_PAD, MLA_PAD, MLA_NOPE, MLA_ROPE)
                 + _rope_tables(nl, SWA_HEADS * HEAD_DIM, HEAD_DIM, 0, HEAD_DIM))

    cond = jnp.concatenate([c_ctx[None, :], c, jnp.zeros((8 - 1 - bl, D_MODEL), c.dtype)], axis=0)
    mod = _adaln(cond, w_ada, b_ada).reshape(DEPTH, 8, 1, 6 * D_MODEL)

    def cache(a):
        return a.reshape(bl, DEPTH, n_past, -1).swapaxes(0, 1).astype(BF16)

    c_nak, c_nav, c_sk, c_sv = cache(cache_na_k), cache(cache_na_v), cache(cache_swa_k), cache(cache_swa_v)
    c_mk, c_mv = _mla_cache(cache_mla_ckv, cache_mla_kpe, wts, consts)
    na_bias = _na_bias(na_rpb, nl // GRID_W)

    s_na, s_mla, s_swa = HEAD_DIM ** -0.5, MLA_QK ** -0.5, HEAD_DIM ** -0.5
    xp = x_prompt.reshape(bc * nc, D_MODEL)
    xs = x_sample.reshape(bl * nl, D_MODEL)
    states = [[] for _ in range(6)]
    for l in range(DEPTH):
        sink = swa_sink[l]
        outs = _pre(xp, mod, l, wts, consts, None, tokens_per_mod=bc * nc, first_mod_row=0, states=True,
                    name="pre_ctx")
        naq, nak, nav, pool_in, mq, mk, mv, sq, sk, sv = [o.reshape(bc, nc, -1) for o in outs[:10]]
        for acc, st in zip(states, outs[10:]):
            acc.append(st)
        parts = [
            _attention(naq, nak, nav, dl=HEAD_DIM, scale=s_na, name="ctx_na"),
            _pool(pool_in, wts, l, "ctx_pool"),
            _attention(mq, mk, mv, dl=MLA_PAD, scale=s_mla, name="ctx_mla"),
            _attention(sq, sk, sv, dl=HEAD_DIM, scale=s_swa, sink=sink, name="ctx_swa"),
        ]
        xp = _post(xp, [p.reshape(bc * nc, 256) for p in parts], mod, l, wts, tokens_per_mod=bc * nc,
                   first_mod_row=0, name="post_ctx")
        outs = _pre(xs, mod, l, wts, consts, rope_tabs, tokens_per_mod=nl, first_mod_row=1, states=False,
                    name="pre_lat")
        naq, nak, nav, pool_in, mq, mk, mv, sq, sk, sv = [o.reshape(bl, nl, -1) for o in outs]
        parts = [
            _attention(naq, nak, nav, dl=HEAD_DIM, scale=s_na, mode="na", ck=c_nak, cv=c_nav, ctx_layer=l,
                       bias=na_bias, bias_layer=l, name="lat_na"),
            _pool(pool_in, wts, l, "lat_pool"),
            _attention(mq, mk, mv, dl=MLA_PAD, scale=s_mla, ck=c_mk, cv=c_mv, ctx_layer=l, name="lat_mla"),
            _attention(sq, sk, sv, dl=HEAD_DIM, scale=s_swa, mode="swa", ck=c_sk, cv=c_sv, ctx_layer=l,
                       sink=sink, name="lat_swa"),
        ]
        xs = _post(xs, [p.reshape(bl * nl, 256) for p in parts], mod, l, wts, tokens_per_mod=nl,
                   first_mod_row=1, name="post_lat")

    def stack(acc, *tail):
        return jnp.stack([a.reshape(bc, nc, *tail) for a in acc], axis=1)

    return (xp.reshape(bc, nc, D_MODEL), xs.reshape(bl, nl, D_MODEL),
            stack(states[0], NA_HEADS, HEAD_DIM), stack(states[1], NA_HEADS, HEAD_DIM),
            stack(states[2], MLA_KV_RANK), stack(states[3], MLA_ROPE),
            stack(states[4], SWA_KV_HEADS, HEAD_DIM), stack(states[5], SWA_KV_HEADS, HEAD_DIM))
```

```python
import functools

import numpy as np
import jax
import jax.numpy as jnp
from jax import lax
from jax.experimental import pallas as pl
from jax.experimental.pallas import tpu as pltpu

F32, BF16 = jnp.float32, jnp.bfloat16

D_MODEL = 1024
DEPTH = 4
GRID_W = 64
HEAD_DIM = 64
NA_HEADS = 4
NA_WIN_ROWS = 8
NA_WIN_COLS = 16
POOL_GROUPS = 4
POOL_GW = 64
POOL_WINDOWS = (2, 4, 8, 16)
MLA_HEADS = 4
MLA_Q_RANK = 256
MLA_KV_RANK = 128
MLA_NOPE = 64
MLA_ROPE = 32
MLA_V = 64
MLA_QK = MLA_NOPE + MLA_ROPE
MLA_PAD = 128
SWA_HEADS = 4
SWA_KV_HEADS = 2
SWA_WINDOW = 128
D_FF = 4 * D_MODEL
ROPE_THETA = 10000.0
EPS = 1e-6
NEG = -1e30
LOG2E = 1.4426950408889634

TOKEN_TILE = 256
Q_TILE = 256
KEY_BLOCK = 256
NA_TILE_ROWS = Q_TILE // GRID_W
NA_KEY_ROWS = NA_TILE_ROWS + NA_WIN_ROWS
SWA_KEYS = Q_TILE + 2 * SWA_WINDOW
ADA_COLS = 2048
FF_CHUNK = 1024
W_IN_COLS = 2048
VMEM_LIMIT = 48 * 1024 * 1024

_C_NA_Q, _C_NA_K, _C_NA_V, _C_POOL, _C_CQ, _C_CKV, _C_SWA_Q, _C_SWA_K, _C_SWA_V, _C_KPE_BLOCK = (
    0, 256, 512, 768, 1024, 1280, 1408, 1664, 1792, 1920)
_C_KPE = _C_KPE_BLOCK + MLA_NOPE


def _params(n_grid):
    return pltpu.CompilerParams(dimension_semantics=("arbitrary",) * n_grid, vmem_limit_bytes=VMEM_LIMIT)


def _layer_spec(arr, l, n_grid):
    nd = arr.ndim
    shape = (None,) + tuple(arr.shape[1:])
    if n_grid == 1:
        return pl.BlockSpec(shape, lambda i: (l,) + (0,) * (nd - 1))
    return pl.BlockSpec(shape, lambda b, i: (l,) + (0,) * (nd - 1))


def _const_spec(arr, n_grid):
    nd = arr.ndim
    if n_grid == 1:
        return pl.BlockSpec(arr.shape, lambda i: (0,) * nd)
    return pl.BlockSpec(arr.shape, lambda b, i: (0,) * nd)


def _rms_full(x, g):
    return x * lax.rsqrt(jnp.mean(x * x, axis=-1, keepdims=True) + EPS) * g


def _split_dot(s, m):
    hi = s.astype(BF16)
    lo = (s - hi.astype(F32)).astype(BF16)
    return jnp.dot(hi, m, preferred_element_type=F32) + jnp.dot(lo, m, preferred_element_type=F32)


def _rms_heads(x, g, m, dim):
    ss = jnp.dot((x * x).astype(BF16), m, preferred_element_type=F32)
    return x * lax.rsqrt(ss * (1.0 / dim) + EPS) * g


def _rms_lanes(x, g, dim):
    ss = jnp.sum(x * x, axis=-1, keepdims=True)
    return x * lax.rsqrt(ss * (1.0 / dim) + EPS) * g


def _rope(x, c_ref, sa_ref, sb_ref, quarter):
    return (x * c_ref[...] + pltpu.roll(x, 128 - quarter, 1) * sa_ref[...]
            + pltpu.roll(x, quarter, 1) * sb_ref[...])


def _adaln_kernel(cond_ref, w_ref, b_ref, o_ref):
    s = jax.nn.silu(cond_ref[...]).astype(BF16)
    o_ref[...] = jnp.dot(s, w_ref[...].astype(BF16), preferred_element_type=F32) + b_ref[...]


def _adaln(cond, w_ada, b_ada):
    n_out = w_ada.shape[-1]
    return pl.pallas_call(
        _adaln_kernel,
        grid=(DEPTH, n_out // ADA_COLS),
        in_specs=[pl.BlockSpec((8, D_MODEL), lambda l, j: (0, 0)),
                  pl.BlockSpec((None, D_MODEL, ADA_COLS), lambda l, j: (l, 0, j)),
                  pl.BlockSpec((None, 1, ADA_COLS), lambda l, j: (l, 0, j))],
        out_specs=pl.BlockSpec((None, 8, ADA_COLS), lambda l, j: (l, 0, j)),
        out_shape=jax.ShapeDtypeStruct((DEPTH, 8, n_out), F32),
        compiler_params=_params(2),
        name="adaln",
    )(cond, w_ada, b_ada.reshape(DEPTH, 1, n_out))


def _pre_kernel(*refs, rope, states):
    (x_ref, mod_ref, n1_ref, win_ref, wuq_ref, wuk_ref, wuv_ref, ones_ref, m256_ref, m128_ref,
     g_naq, g_nak, g_qa, g_kv, g_mq, g_mk, g_sq, g_sk) = refs[:18]
    pos = 18
    if rope:
        cm, sam, sbm, cs, sas, sbs = refs[pos:pos + 6]
        pos += 6
    (o_naq, o_nak, o_nav, o_pool, o_mq, o_mk, o_mv, o_sq, o_sk, o_sv) = refs[pos:pos + 10]
    pos += 10
    if states:
        s_nak, s_nav, s_ckv, s_kpe, s_sk, s_sv = refs[pos:pos + 6]

    x = x_ref[...]
    mod = mod_ref[...]
    sh1, sc1 = mod[:, 0:D_MODEL], mod[:, D_MODEL:2 * D_MODEL]
    h = _rms_full(x, n1_ref[...]) * (1.0 + sc1) + sh1
    y = jnp.dot(h.astype(BF16), win_ref[...], preferred_element_type=F32)

    m256 = m256_ref[...]
    na_k = _rms_heads(y[:, _C_NA_K:_C_NA_K + 256], g_nak[...], m256, HEAD_DIM)
    na_v = y[:, _C_NA_V:_C_NA_V + 256]
    o_naq[...] = _rms_heads(y[:, _C_NA_Q:_C_NA_Q + 256], g_naq[...], m256, HEAD_DIM).astype(BF16)
    o_nak[...] = na_k.astype(BF16)
    o_nav[...] = na_v.astype(BF16)
    o_pool[...] = y[:, _C_POOL:_C_POOL + 256]

    cq = _rms_full(y[:, _C_CQ:_C_CQ + MLA_Q_RANK], g_qa[...])
    mq = jnp.dot(cq.astype(BF16), wuq_ref[...], preferred_element_type=F32)
    ckv = _rms_full(y[:, _C_CKV:_C_CKV + MLA_KV_RANK], g_kv[...])
    ckv_b = ckv.astype(BF16)
    mk = jnp.dot(ckv_b, wuk_ref[...], preferred_element_type=F32)
    kpe_block = y[:, _C_KPE_BLOCK:_C_KPE_BLOCK + MLA_PAD]
    o_mv[...] = (jnp.dot(ckv_b, wuv_ref[...], preferred_element_type=F32) + ones_ref[...]).astype(BF16)
    for hd in range(MLA_HEADS):
        cols = slice(hd * MLA_PAD, (hd + 1) * MLA_PAD)
        qh = _rms_lanes(mq[:, cols], g_mq[:, cols], MLA_QK)
        kh = _rms_lanes(mk[:, cols] + kpe_block, g_mk[:, cols], MLA_QK)
        if rope:
            qh = _rope(qh, cm, sam, sbm, MLA_ROPE // 4)
            kh = _rope(kh, cm, sam, sbm, MLA_ROPE // 4)
        o_mq[:, cols] = qh.astype(BF16)
        o_mk[:, cols] = kh.astype(BF16)

    sq = _rms_heads(y[:, _C_SWA_Q:_C_SWA_Q + 256], g_sq[...], m256, HEAD_DIM)
    sk = _rms_heads(y[:, _C_SWA_K:_C_SWA_K + 128], g_sk[...], m128_ref[...], HEAD_DIM)
    sv = y[:, _C_SWA_V:_C_SWA_V + 128]
    if states:
        s_nak[...] = na_k
        s_nav[...] = na_v
        s_ckv[...] = ckv
        s_kpe[...] = y[:, _C_KPE:_C_KPE + MLA_ROPE]
        s_sk[...] = sk
        s_sv[...] = sv
    if rope:
        for blk in range(2):
            cols = slice(blk * 128, (blk + 1) * 128)
            o_sq[:, cols] = _rope(sq[:, cols], cs, sas, sbs, HEAD_DIM // 4).astype(BF16)
        sk = _rope(sk, cs, sas, sbs, HEAD_DIM // 4)
    else:
        o_sq[...] = sq.astype(BF16)
    o_sk[...] = sk.astype(BF16)
    o_sv[...] = sv.astype(BF16)


def _pre(x, mod, l, wts, consts, rope_tabs, *, tokens_per_mod, first_mod_row, states, name):
    t = x.shape[0]
    tm = TOKEN_TILE
    rope = rope_tabs is not None
    tiles_per_mod = tokens_per_mod // tm
    n_pos = rope_tabs[0].shape[0] // tm if rope else 1

    def row(width, dtype):
        return jax.ShapeDtypeStruct((t, width), dtype)

    def tok(width):
        return pl.BlockSpec((tm, width), lambda i: (i, 0))

    in_specs = [tok(D_MODEL),
                pl.BlockSpec((None, None, 1, 6 * D_MODEL), lambda i: (l, first_mod_row + i // tiles_per_mod, 0, 0)),
                _layer_spec(wts["norm1"], l, 1), _layer_spec(wts["w_in"], l, 1), _layer_spec(wts["w_uq"], l, 1),
                _layer_spec(wts["w_uk"], l, 1), _layer_spec(wts["w_uv"], l, 1),
                _const_spec(consts["v_ones"], 1), _const_spec(consts["m256"], 1), _const_spec(consts["m128"], 1)]
    args = [x, mod, wts["norm1"], wts["w_in"], wts["w_uq"], wts["w_uk"], wts["w_uv"],
            consts["v_ones"], consts["m256"], consts["m128"]]
    for nm in ("g_naq", "g_nak", "g_qa", "g_kv", "g_mq", "g_mk", "g_sq", "g_sk"):
        in_specs.append(_layer_spec(wts[nm], l, 1))
        args.append(wts[nm])
    if rope:
        for tab in rope_tabs:
            in_specs.append(pl.BlockSpec((tm, tab.shape[1]), lambda i: (i % n_pos, 0)))
            args.append(tab)
    widths = (256, 256, 256, 256, 512, 512, 512, 256, 128, 128)
    dtypes = (BF16, BF16, BF16, F32, BF16, BF16, BF16, BF16, BF16, BF16)
    out_shape = [row(w, d) for w, d in zip(widths, dtypes)]
    out_specs = [tok(w) for w in widths]
    if states:
        for w in (256, 256, MLA_KV_RANK, MLA_ROPE, 128, 128):
            out_shape.append(row(w, F32))
            out_specs.append(tok(w))
    return pl.pallas_call(
        functools.partial(_pre_kernel, rope=rope, states=states),
        grid=(t // tm,), in_specs=in_specs, out_specs=out_specs, out_shape=out_shape,
        compiler_params=_params(1), name=name,
    )(*args)


def _mla_cache_kernel(ckv_ref, kpe_ref, wuk_ref, wuv_ref, e_ref, ones_ref, m512_ref, g_mk, ok_ref, ov_ref):
    ckv_b = ckv_ref[...].astype(BF16)
    mk = jnp.dot(ckv_b, wuk_ref[...], preferred_element_type=F32) + _split_dot(kpe_ref[...], e_ref[...])
    ok_ref[...] = _rms_heads(mk, g_mk[...], m512_ref[...], MLA_QK).astype(BF16)
    ov_ref[...] = (jnp.dot(ckv_b, wuv_ref[...], preferred_element_type=F32) + ones_ref[...]).astype(BF16)


def _mla_cache(c_ckv, c_kpe, wts, consts):
    b, _, n, _ = c_ckv.shape
    width = MLA_HEADS * MLA_PAD

    def lay(arr):
        nd = arr.ndim
        return pl.BlockSpec((None,) + tuple(arr.shape[1:]), lambda l, i: (l,) + (0,) * (nd - 1))

    return pl.pallas_call(
        _mla_cache_kernel,
        grid=(DEPTH, b),
        in_specs=[pl.BlockSpec((None, None, n, MLA_KV_RANK), lambda l, i: (i, l, 0, 0)),
                  pl.BlockSpec((None, None, n, MLA_ROPE), lambda l, i: (i, l, 0, 0)),
                  lay(wts["w_uk"]), lay(wts["w_uv"]), _const_spec(consts["e_kpe"], 2),
                  _const_spec(consts["v_ones"], 2), _const_spec(consts["m512"], 2), lay(wts["g_mk"])],
        out_specs=[pl.BlockSpec((None, None, n, width), lambda l, i: (l, i, 0, 0)),
                   pl.BlockSpec((None, None, n, width), lambda l, i: (l, i, 0, 0))],
        out_shape=[jax.ShapeDtypeStruct((DEPTH, b, n, width), BF16),
                   jax.ShapeDtypeStruct((DEPTH, b, n, width), BF16)],
        compiler_params=_params(2), name="mla_cache",
    )(c_ckv, c_kpe, wts["w_uk"], wts["w_uv"], consts["e_kpe"], consts["v_ones"], consts["m512"], wts["g_mk"])


_NA_VARIANTS = ((0, 0), (4, 0), (24, 20), (28, 20))


def _na_bias_kernel(rpb_ref, o_ref, *, grid_rows):
    l, hd = pl.program_id(0), pl.program_id(1)
    n_dr, n_dc = 2 * NA_WIN_ROWS - 1, 2 * NA_WIN_COLS - 1
    base = (l * NA_HEADS + hd) * (n_dr * n_dc)
    c = lax.broadcasted_iota(jnp.int32, (GRID_W, GRID_W), 0)
    kc = lax.broadcasted_iota(jnp.int32, (GRID_W, GRID_W), 1)
    dcv = kc - c + (NA_WIN_COLS - 1)
    col_start = jnp.clip(c - NA_WIN_COLS // 2, 0, GRID_W - NA_WIN_COLS)
    col_ok = (kc >= col_start) & (kc < col_start + NA_WIN_COLS)
    neg = jnp.full((GRID_W, GRID_W), NEG, F32)
    blocks = {}

    def block(dr):
        if dr not in blocks:
            t = jnp.zeros((GRID_W, GRID_W), F32)
            for dc in range(n_dc):
                t = jnp.where(dcv == dc, rpb_ref[base + dr * n_dc + dc] * LOG2E, t)
            blocks[dr] = jnp.where(col_ok, t, NEG)
        return blocks[dr]

    for v, (r0, w0) in enumerate(_NA_VARIANTS):
        for rl in range(NA_TILE_ROWS):
            r = r0 + rl
            kr0 = min(max(r - NA_WIN_ROWS // 2, 0), grid_rows - NA_WIN_ROWS)
            for j in range(NA_KEY_ROWS):
                kr = w0 + j
                inside = kr0 <= kr < kr0 + NA_WIN_ROWS
                o_ref[v, rl * GRID_W:(rl + 1) * GRID_W, j * GRID_W:(j + 1) * GRID_W] = (
                    block(kr - r + NA_WIN_ROWS - 1) if inside else neg)


def _na_bias(na_rpb, grid_rows):
    n_k = NA_KEY_ROWS * GRID_W
    return pl.pallas_call(
        functools.partial(_na_bias_kernel, grid_rows=grid_rows),
        grid=(DEPTH, NA_HEADS),
        in_specs=[pl.BlockSpec(memory_space=pltpu.SMEM)],
        out_specs=pl.BlockSpec((None, len(_NA_VARIANTS), None, Q_TILE, n_k), lambda l, h: (l, 0, h, 0, 0)),
        out_shape=jax.ShapeDtypeStruct((DEPTH, len(_NA_VARIANTS), NA_HEADS, Q_TILE, n_k), F32),
        compiler_params=_params(2), name="na_bias",
    )(na_rpb.reshape(-1))


def _attn_kernel(*refs, tq, wk, wv, dl, n_slabs, mode, has_ctx, has_sink, n_keys):
    q_ref, k_ref, v_ref = refs[:3]
    pos = 3
    if has_ctx:
        ck_ref, cv_ref = refs[pos:pos + 2]
        pos += 2
    if mode == "na":
        bias_ref = refs[pos]
        pos += 1
    if has_sink:
        sink_ref = refs[pos]
        pos += 1
    o_ref = refs[pos]
    i = pl.program_id(1)
    hs = wk // dl

    valid = None
    if mode == "full":
        kl, vl = k_ref[...], v_ref[...]
    elif mode == "swa":
        w0 = pl.multiple_of(jnp.clip(i * tq - SWA_WINDOW, 0, n_keys - SWA_KEYS), SWA_WINDOW)
        kl, vl = k_ref[pl.ds(w0, SWA_KEYS), :], v_ref[pl.ds(w0, SWA_KEYS), :]
        qpos = i * tq + lax.broadcasted_iota(jnp.int32, (tq, SWA_KEYS), 0)
        kpos = w0 + lax.broadcasted_iota(jnp.int32, (tq, SWA_KEYS), 1)
        valid = jnp.abs(kpos - qpos) <= SWA_WINDOW
    else:
        last = n_keys // GRID_W - NA_KEY_ROWS
        w0 = pl.multiple_of(jnp.clip(i * NA_TILE_ROWS - NA_WIN_ROWS // 2, 0, last) * GRID_W, GRID_W)
        kl, vl = k_ref[pl.ds(w0, NA_KEY_ROWS * GRID_W), :], v_ref[pl.ds(w0, NA_KEY_ROWS * GRID_W), :]
    if has_ctx:
        ck, cv = ck_ref[...], cv_ref[...]

    lane_k = lax.broadcasted_iota(jnp.int32, (1, wk), 1)
    lane_v = lax.broadcasted_iota(jnp.int32, (1, wv), 1)
    nt = (((1,), (1,)), ((), ()))
    for s in range(n_slabs):
        qs = q_ref[:, s * wk:(s + 1) * wk]
        acc = jnp.zeros((tq, wv), F32)
        for j in range(hs):
            head = j * n_slabs + s
            qm = jnp.where((lane_k >= j * dl) & (lane_k < (j + 1) * dl), qs, jnp.zeros_like(qs))
            s_loc = lax.dot_general(qm, kl, nt, preferred_element_type=F32)
            if mode == "na":
                s_loc = s_loc + bias_ref[head]
            if valid is not None:
                s_loc = jnp.where(valid, s_loc, NEG)
            m = jnp.max(s_loc, axis=-1, keepdims=True)
            if has_ctx:
                s_ctx = lax.dot_general(qm, ck, nt, preferred_element_type=F32)
                m = jnp.maximum(m, jnp.max(s_ctx, axis=-1, keepdims=True))
            if has_sink:
                sink = sink_ref[head] * LOG2E
                m = jnp.maximum(m, sink)
            p_loc = jnp.exp2(s_loc - m)
            den = jnp.sum(p_loc, axis=-1, keepdims=True)
            o = jnp.dot(p_loc.astype(BF16), vl, preferred_element_type=F32)
            if has_ctx:
                p_ctx = jnp.exp2(s_ctx - m)
                den = den + jnp.sum(p_ctx, axis=-1, keepdims=True)
                o = o + jnp.dot(p_ctx.astype(BF16), cv, preferred_element_type=F32)
            if has_sink:
                den = den + jnp.exp2(sink - m)
            out_lanes = (lane_v >= j * HEAD_DIM) & (lane_v < (j + 1) * HEAD_DIM)
            acc = acc + jnp.where(out_lanes, o * (1.0 / den), 0.0)
        o_ref[:, s * wv:(s + 1) * wv] = acc.astype(o_ref.dtype)


def _attention(q, k, v, *, dl, mode="full", ck=None, cv=None, ctx_layer=None, bias=None, bias_layer=None,
               sink=None, name):
    b, n, wq = q.shape
    _, nk, wk = k.shape
    wv = v.shape[-1]
    n_slabs = wq // wk
    tq = min(Q_TILE, n)
    in_specs = [pl.BlockSpec((None, tq, wq), lambda bi, i: (bi, i, 0)),
                pl.BlockSpec((None, nk, wk), lambda bi, i: (bi, 0, 0)),
                pl.BlockSpec((None, nk, wv), lambda bi, i: (bi, 0, 0))]
    args = [q, k, v]
    if ck is not None:
        n_ctx = ck.shape[2]
        in_specs += [pl.BlockSpec((None, None, n_ctx, wk), lambda bi, i: (ctx_layer, bi, 0, 0)),
                     pl.BlockSpec((None, None, n_ctx, wv), lambda bi, i: (ctx_layer, bi, 0, 0))]
        args += [ck, cv]
    if mode == "na":
        n_tiles = n // tq

        def variant(bi, i):
            v_idx = jnp.where(i == 0, 0, jnp.where(i <= n_tiles - 3, 1, i - (n_tiles - 4)))
            return (bias_layer, v_idx, 0, 0, 0)

        in_specs.append(pl.BlockSpec((None, None) + tuple(bias.shape[2:]), variant))
        args.append(bias)
    if sink is not None:
        in_specs.append(pl.BlockSpec(memory_space=pltpu.SMEM))
        args.append(sink)
    return pl.pallas_call(
        functools.partial(_attn_kernel, tq=tq, wk=wk, wv=wv, dl=dl, n_slabs=n_slabs, mode=mode,
                          has_ctx=ck is not None, has_sink=sink is not None, n_keys=nk),
        grid=(b, n // tq), in_specs=in_specs,
        out_specs=pl.BlockSpec((None, tq, n_slabs * wv), lambda bi, i: (bi, i, 0)),
        out_shape=jax.ShapeDtypeStruct((b, n, n_slabs * wv), BF16),
        compiler_params=_params(2), name=name,
    )(*args)


def _mla_attn_kernel(*refs, tq, has_ctx):
    q_ref, k_ref, v_ref = refs[:3]
    pos = 3
    if has_ctx:
        ck_ref, cv_ref = refs[pos:pos + 2]
        pos += 2
    o_ref, s_ref = refs[pos], refs[pos + 1]
    blocks = [(k_ref, v_ref, off) for off in range(0, k_ref.shape[0], KEY_BLOCK)]
    if has_ctx:
        blocks += [(ck_ref, cv_ref, off) for off in range(0, ck_ref.shape[0], KEY_BLOCK)]
    nt = (((1,), (1,)), ((), ()))
    lane = lax.broadcasted_iota(jnp.int32, (1, MLA_PAD), 1)
    outs = []
    for hd in range(MLA_HEADS):
        cols = slice(hd * MLA_PAD, (hd + 1) * MLA_PAD)
        qh = q_ref[:, cols]
        m_part = jnp.full((tq, 128), NEG, F32)
        for bi, (kr, _, off) in enumerate(blocks):
            sb = lax.dot_general(qh, kr[off:off + KEY_BLOCK, cols], nt, preferred_element_type=F32)
            s_ref[:, bi * KEY_BLOCK:(bi + 1) * KEY_BLOCK] = sb
            m_part = jnp.maximum(m_part, jnp.maximum(sb[:, :128], sb[:, 128:]))
        m = jnp.max(m_part, axis=-1, keepdims=True)
        o = jnp.zeros((tq, MLA_PAD), F32)
        for bi, (_, vr, off) in enumerate(blocks):
            p = jnp.exp2(s_ref[:, bi * KEY_BLOCK:(bi + 1) * KEY_BLOCK] - m)
            o = o + jnp.dot(p.astype(BF16), vr[off:off + KEY_BLOCK, cols], preferred_element_type=F32)
        outs.append(o * pltpu.roll(1.0 / o, MLA_V, 1))
    for hd in range(0, MLA_HEADS, 2):
        pair = jnp.where(lane < MLA_V, outs[hd], pltpu.roll(outs[hd + 1], MLA_V, 1))
        o_ref[:, (hd // 2) * 128:(hd // 2 + 1) * 128] = pair.astype(o_ref.dtype)


def _mla_attention(q, k, v, *, ck=None, cv=None, ctx_layer=None, name):
    b, n, w = q.shape
    nk = k.shape[1]
    tq = min(Q_TILE, n)
    in_specs = [pl.BlockSpec((None, tq, w), lambda bi, i: (bi, i, 0)),
                pl.BlockSpec((None, nk, w), lambda bi, i: (bi, 0, 0)),
                pl.BlockSpec((None, nk, w), lambda bi, i: (bi, 0, 0))]
    args = [q, k, v]
    n_ctx = 0
    if ck is not None:
        n_ctx = ck.shape[2]
        in_specs += [pl.BlockSpec((None, None, n_ctx, w), lambda bi, i: (ctx_layer, bi, 0, 0)),
                     pl.BlockSpec((None, None, n_ctx, w), lambda bi, i: (ctx_layer, bi, 0, 0))]
        args += [ck, cv]
    return pl.pallas_call(
        functools.partial(_mla_attn_kernel, tq=tq, has_ctx=ck is not None),
        grid=(b, n // tq), in_specs=in_specs,
        out_specs=pl.BlockSpec((None, tq, MLA_HEADS * MLA_V), lambda bi, i: (bi, i, 0)),
        out_shape=jax.ShapeDtypeStruct((b, n, MLA_HEADS * MLA_V), BF16),
        scratch_shapes=[pltpu.VMEM((tq, nk + n_ctx), F32)],
        compiler_params=_params(2), name=name,
    )(*args)


def _pool_kernel(x_ref, w_ref, s_ref, o_ref, *, n):
    x = x_ref[...]
    t = lax.broadcasted_iota(jnp.int32, x.shape, 0)
    lane = lax.broadcasted_iota(jnp.int32, x.shape, 1)

    def later(a, d):
        return jnp.where(t < n - d, pltpu.roll(a, n - d, 0), 0.0)

    def earlier(a, d):
        return jnp.where(t >= d, pltpu.roll(a, d, 0), 0.0)

    fwd, bwd = {1: x}, {1: x}
    for k in (1, 2, 4):
        fwd[2 * k] = fwd[k] + later(fwd[k], k)
        bwd[2 * k] = bwd[k] + earlier(bwd[k], k)
    pooled = jnp.zeros_like(x)
    for gi, w in enumerate(POOL_WINDOWS):
        total = fwd[w // 2] + earlier(bwd[w // 2], 1)
        count = (jnp.clip(t + w // 2, 0, n) - jnp.clip(t - w // 2, 0, n)).astype(F32)
        in_group = (lane >= gi * POOL_GW) & (lane < (gi + 1) * POOL_GW)
        pooled = jnp.where(in_group, total / count - x, pooled)
    y = jnp.dot(pooled.astype(BF16), w_ref[...], preferred_element_type=F32)
    o_ref[...] = (y * s_ref[...]).astype(o_ref.dtype)


def _pool(x, wts, l, name):
    b, n, w = x.shape
    return pl.pallas_call(
        functools.partial(_pool_kernel, n=n),
        grid=(b,),
        in_specs=[pl.BlockSpec((None, n, w), lambda i: (i, 0, 0)),
                  _layer_spec(wts["pool_w"], l, 1), _layer_spec(wts["pool_scale"], l, 1)],
        out_specs=pl.BlockSpec((None, n, w), lambda i: (i, 0, 0)),
        out_shape=jax.ShapeDtypeStruct((b, n, w), BF16),
        compiler_params=_params(1), name=name,
    )(x, wts["pool_w"], wts["pool_scale"])


def _post_kernel(x_ref, p0, p1, p2, p3, mod_ref, n2_ref, wo_ref, w1_ref, w2_ref, o_ref):
    x = x_ref[...]
    mod = mod_ref[...]
    g1 = mod[:, 2 * D_MODEL:3 * D_MODEL]
    sh2, sc2, g2 = (mod[:, 3 * D_MODEL:4 * D_MODEL], mod[:, 4 * D_MODEL:5 * D_MODEL],
                    mod[:, 5 * D_MODEL:6 * D_MODEL])
    a = jnp.zeros(x.shape, F32)
    for i, p in enumerate((p0, p1, p2, p3)):
        a = a + jnp.dot(p[...], wo_ref[i * 256:(i + 1) * 256, :], preferred_element_type=F32)
    x = x + g1 * a
    h = (_rms_full(x, n2_ref[...]) * (1.0 + sc2) + sh2).astype(BF16)
    acc = jnp.zeros(x.shape, F32)
    for c in range(D_FF // FF_CHUNK):
        z = jnp.maximum(jnp.dot(h, w1_ref[:, c * FF_CHUNK:(c + 1) * FF_CHUNK], preferred_element_type=F32), 0.0)
        acc = acc + jnp.dot((z * z).astype(BF16), w2_ref[c * FF_CHUNK:(c + 1) * FF_CHUNK, :],
                            preferred_element_type=F32)
    o_ref[...] = x + g2 * acc


def _post(x, parts, mod, l, wts, *, tokens_per_mod, first_mod_row, name):
    t = x.shape[0]
    tm = TOKEN_TILE
    tiles_per_mod = tokens_per_mod // tm

    def tok(width):
        return pl.BlockSpec((tm, width), lambda i: (i, 0))

    def resident(arr):
        nd = arr.ndim
        return pl.BlockSpec((None,) + tuple(arr.shape[1:]), lambda i: (l,) + (0,) * (nd - 1),
                            pipeline_mode=pl.Buffered(1))

    return pl.pallas_call(
        _post_kernel,
        grid=(t // tm,),
        in_specs=[tok(D_MODEL), tok(256), tok(256), tok(256), tok(256),
                  pl.BlockSpec((None, None, 1, 6 * D_MODEL), lambda i: (l, first_mod_row + i // tiles_per_mod, 0, 0)),
                  _layer_spec(wts["norm2"], l, 1), resident(wts["w_out"]), resident(wts["w1"]), resident(wts["w2"])],
        out_specs=tok(D_MODEL),
        out_shape=jax.ShapeDtypeStruct((t, D_MODEL), F32),
        compiler_params=_params(1), name=name,
    )(x, *parts, mod, wts["norm2"], wts["w_out"], wts["w1"], wts["w2"])


def _block_ones(width, group):
    idx = np.arange(width) // group
    return jnp.asarray(idx[:, None] == idx[None, :], BF16)


def _rope_tables(n_tokens, head_lanes, first, rope_dims):
    half, quarter = rope_dims // 2, rope_dims // 4
    t = jnp.arange(n_tokens, dtype=jnp.int32)
    rows, cols = t // GRID_W, t % GRID_W
    freqs = ROPE_THETA ** (-jnp.arange(quarter, dtype=F32) / quarter)
    lane = np.arange(128) % head_lanes - first
    active = (lane >= 0) & (lane < rope_dims)
    lane = np.where(active, lane, 0)
    use_cols = lane >= half
    within = lane % half
    upper = within >= quarter
    f_idx = within % quarter
    pos = jnp.where(jnp.asarray(use_cols)[None, :], cols[:, None], rows[:, None]).astype(F32)
    ang = pos * freqs[jnp.asarray(f_idx)][None, :]
    act = jnp.asarray(active)[None, :]
    up = jnp.asarray(upper)[None, :]
    cos = jnp.where(act, jnp.cos(ang), 1.0)
    sin = jnp.where(act, jnp.sin(ang), 0.0)
    return cos, jnp.where(up, 0.0, -sin), jnp.where(up, sin, 0.0)


def _prepare(norm1, norm2, w_in, w_out, na_q_norm, na_k_norm, pool_w, pool_scale, mla_q_a_norm, mla_w_uq,
             mla_kv_norm, mla_w_ukv, mla_q_norm, mla_k_norm, swa_q_norm, swa_k_norm, mlp_w1, mlp_w2):
    d = DEPTH
    offs = np.cumsum((256, 256, 256, 256, MLA_Q_RANK, MLA_KV_RANK, MLA_ROPE, 256, 128, 128))[:-1]
    (w_naq, w_nak, w_nav, w_pool, w_cq, w_ckv, w_kpe, w_sq, w_sk, w_sv) = jnp.split(w_in, offs, axis=-1)
    w_sq = w_sq.reshape(d, D_MODEL, 2, 2, HEAD_DIM).swapaxes(2, 3).reshape(d, D_MODEL, 256)
    w_kpe_block = jnp.pad(w_kpe, ((0, 0), (0, 0), (MLA_NOPE, MLA_PAD - MLA_QK)))
    w_in_r = jnp.concatenate([w_naq, w_nak, w_nav, w_pool, w_cq, w_ckv, w_sq, w_sk, w_sv, w_kpe_block], axis=-1)

    def pad_heads(a, width):
        a = a.reshape(a.shape[:-1] + (MLA_HEADS, width))
        a = jnp.pad(a, [(0, 0)] * (a.ndim - 1) + [(0, MLA_PAD - width)])
        return a.reshape(a.shape[:-2] + (MLA_HEADS * MLA_PAD,))

    ukv = mla_w_ukv.reshape(d, MLA_KV_RANK, MLA_HEADS, MLA_NOPE + MLA_V)
    w_out_s = w_out[:, 768:1024].reshape(d, 2, 2, HEAD_DIM, D_MODEL).swapaxes(1, 2).reshape(d, 256, D_MODEL)
    eye = jnp.eye(POOL_GROUPS, dtype=pool_w.dtype)
    pool_bd = jnp.einsum("lgce,gh->lgche", pool_w, eye).reshape(d, 256, 256)

    def gain(g, reps, scale=1.0):
        return (jnp.tile(g, (1, reps)) * scale).reshape(d, 1, -1)

    return {
        "norm1": norm1.reshape(d, 1, D_MODEL), "norm2": norm2.reshape(d, 1, D_MODEL),
        "w_in": w_in_r.astype(BF16),
        "w_uq": pad_heads(mla_w_uq, MLA_QK).astype(BF16),
        "w_uk": pad_heads(ukv[..., :MLA_NOPE].reshape(d, MLA_KV_RANK, -1), MLA_NOPE).astype(BF16),
        "w_uv": pad_heads(ukv[..., MLA_NOPE:].reshape(d, MLA_KV_RANK, -1), MLA_V).astype(BF16),
        "w_out": jnp.concatenate([w_out[:, :768], w_out_s], axis=1).astype(BF16),
        "w1": mlp_w1.astype(BF16), "w2": mlp_w2.astype(BF16),
        "pool_w": pool_bd.astype(BF16), "pool_scale": pool_scale.reshape(d, 1, 256),
        "g_naq": gain(na_q_norm, NA_HEADS, HEAD_DIM ** -0.5 * LOG2E), "g_nak": gain(na_k_norm, NA_HEADS),
        "g_qa": mla_q_a_norm.reshape(d, 1, MLA_Q_RANK), "g_kv": mla_kv_norm.reshape(d, 1, MLA_KV_RANK),
        "g_mq": pad_heads(jnp.tile(mla_q_norm, (1, MLA_HEADS)) * (MLA_QK ** -0.5 * LOG2E), MLA_QK).reshape(d, 1, -1),
        "g_mk": pad_heads(jnp.tile(mla_k_norm, (1, MLA_HEADS)), MLA_QK).reshape(d, 1, -1),
        "g_sq": gain(swa_q_norm, SWA_HEADS, HEAD_DIM ** -0.5 * LOG2E), "g_sk": gain(swa_k_norm, SWA_KV_HEADS),
    }


def _constants():
    width = MLA_HEADS * MLA_PAD
    e = np.zeros((MLA_ROPE, width), np.float32)
    for h in range(MLA_HEADS):
        e[np.arange(MLA_ROPE), h * MLA_PAD + MLA_NOPE + np.arange(MLA_ROPE)] = 1.0
    v_ones = (np.arange(width) % MLA_PAD >= MLA_V).astype(np.float32)[None, :]
    return {"e_kpe": jnp.asarray(e, BF16), "v_ones": jnp.asarray(v_ones), "m256": _block_ones(256, HEAD_DIM),
            "m128": _block_ones(128, HEAD_DIM), "m512": _block_ones(width, MLA_PAD)}


def kernel(x_prompt, x_sample, cache_na_k, cache_na_v, cache_mla_ckv, cache_mla_kpe, cache_swa_k, cache_swa_v, c, c_ctx, norm1, norm2, w_ada, b_ada, w_in, w_out, na_q_norm, na_k_norm, na_rpb, pool_w, pool_scale, mla_q_a_norm, mla_w_uq, mla_kv_norm, mla_w_ukv, mla_q_norm, mla_k_norm, swa_q_norm, swa_k_norm, swa_sink, mlp_w1, mlp_w2):
    bc, nc, _ = x_prompt.shape
    bl, nl, _ = x_sample.shape
    n_past = cache_na_k.shape[2]
    wts = _prepare(norm1, norm2, w_in, w_out, na_q_norm, na_k_norm, pool_w, pool_scale, mla_q_a_norm, mla_w_uq,
                   mla_kv_norm, mla_w_ukv, mla_q_norm, mla_k_norm, swa_q_norm, swa_k_norm, mlp_w1, mlp_w2)
    consts = _constants()
    rope_tabs = _rope_tables(nl, MLA_PAD, MLA_NOPE, MLA_ROPE) + _rope_tables(nl, HEAD_DIM, 0, HEAD_DIM)

    cond = jnp.concatenate([c_ctx[None, :], c, jnp.zeros((8 - 1 - bl, D_MODEL), c.dtype)], axis=0)
    mod = _adaln(cond, w_ada, b_ada).reshape(DEPTH, 8, 1, 6 * D_MODEL)

    def cache(a):
        return a.reshape(bl, DEPTH, n_past, -1).swapaxes(0, 1).astype(BF16)

    c_nak, c_nav, c_sk, c_sv = cache(cache_na_k), cache(cache_na_v), cache(cache_swa_k), cache(cache_swa_v)
    c_mk, c_mv = _mla_cache(cache_mla_ckv, cache_mla_kpe, wts, consts)
    na_bias = _na_bias(na_rpb, nl // GRID_W)

    xp = x_prompt.reshape(bc * nc, D_MODEL)
    xs = x_sample.reshape(bl * nl, D_MODEL)
    states = [[] for _ in range(6)]
    for l in range(DEPTH):
        sink = swa_sink[l]
        outs = _pre(xp, mod, l, wts, consts, None, tokens_per_mod=bc * nc, first_mod_row=0, states=True,
                    name="pre_ctx")
        naq, nak, nav, pool_in, mq, mk, mv, sq, sk, sv = [o.reshape(bc, nc, -1) for o in outs[:10]]
        for acc, st in zip(states, outs[10:]):
            acc.append(st)
        parts = [
            _attention(naq, nak, nav, dl=HEAD_DIM, name="ctx_na"),
            _pool(pool_in, wts, l, "ctx_pool"),
            _mla_attention(mq, mk, mv, name="ctx_mla"),
            _attention(sq, sk, sv, dl=HEAD_DIM, sink=sink, name="ctx_swa"),
        ]
        xp = _post(xp, [p.reshape(bc * nc, 256) for p in parts], mod, l, wts, tokens_per_mod=bc * nc,
                   first_mod_row=0, name="post_ctx")
        outs = _pre(xs, mod, l, wts, consts, rope_tabs, tokens_per_mod=nl, first_mod_row=1, states=False,
                    name="pre_lat")
        naq, nak, nav, pool_in, mq, mk, mv, sq, sk, sv = [o.reshape(bl, nl, -1) for o in outs]
        parts = [
            _attention(naq, nak, nav, dl=HEAD_DIM, mode="na", ck=c_nak, cv=c_nav, ctx_layer=l,
                       bias=na_bias, bias_layer=l, name="lat_na"),
            _pool(pool_in, wts, l, "lat_pool"),
            _mla_attention(mq, mk, mv, ck=c_mk, cv=c_mv, ctx_layer=l, name="lat_mla"),
            _attention(sq, sk, sv, dl=HEAD_DIM, mode="swa", ck=c_sk, cv=c_sv, ctx_layer=l,
                       sink=sink, name="lat_swa"),
        ]
        xs = _post(xs, [p.reshape(bl * nl, 256) for p in parts], mod, l, wts, tokens_per_mod=nl,
                   first_mod_row=1, name="post_lat")

    def stack(acc, *tail):
        return jnp.stack([a.reshape(bc, nc, *tail) for a in acc], axis=1)

    return (xp.reshape(bc, nc, D_MODEL), xs.reshape(bl, nl, D_MODEL),
            stack(states[0], NA_HEADS, HEAD_DIM), stack(states[1], NA_HEADS, HEAD_DIM),
            stack(states[2], MLA_KV_RANK), stack(states[3], MLA_ROPE),
            stack(states[4], SWA_KV_HEADS, HEAD_DIM), stack(states[5], SWA_KV_HEADS, HEAD_DIM))
```

```python
import functools

import numpy as np
import jax
import jax.numpy as jnp
from jax import lax
from jax.experimental import pallas as pl
from jax.experimental.pallas import tpu as pltpu

F32, BF16 = jnp.float32, jnp.bfloat16

D_MODEL = 1024
DEPTH = 4
GRID_W = 64
HEAD_DIM = 64
NA_HEADS = 4
NA_WIN_ROWS = 8
NA_WIN_COLS = 16
POOL_GROUPS = 4
POOL_GW = 64
POOL_WINDOWS = (2, 4, 8, 16)
MLA_HEADS = 4
MLA_Q_RANK = 256
MLA_KV_RANK = 128
MLA_NOPE = 64
MLA_ROPE = 32
MLA_V = 64
MLA_QK = MLA_NOPE + MLA_ROPE
MLA_PAD = 128
SWA_HEADS = 4
SWA_KV_HEADS = 2
SWA_WINDOW = 128
D_FF = 4 * D_MODEL
ROPE_THETA = 10000.0
EPS = 1e-6
NEG = -1e30
LOG2E = 1.4426950408889634

TOKEN_TILE = 256
PRE_TILE = 512
PRE_CHAIN = 128
Q_TILE = 256
KEY_BLOCK = 256
NA_TILE_ROWS = Q_TILE // GRID_W
NA_KEY_ROWS = NA_TILE_ROWS + NA_WIN_ROWS
SWA_KEYS = Q_TILE + 2 * SWA_WINDOW
ADA_COLS = 2048
FF_CHUNK = 1024
W_IN_COLS = 2048
VMEM_LIMIT = 48 * 1024 * 1024

_C_NA_Q, _C_NA_K, _C_NA_V, _C_POOL, _C_CQ, _C_CKV, _C_SWA_Q, _C_SWA_K, _C_SWA_V, _C_KPE_BLOCK = (
    0, 256, 512, 768, 1024, 1280, 1408, 1664, 1792, 1920)
_C_KPE = _C_KPE_BLOCK + MLA_NOPE


def _params(n_grid):
    return pltpu.CompilerParams(dimension_semantics=("arbitrary",) * n_grid, vmem_limit_bytes=VMEM_LIMIT)


def _layer_spec(arr, l, n_grid):
    nd = arr.ndim
    shape = (None,) + tuple(arr.shape[1:])
    if n_grid == 1:
        return pl.BlockSpec(shape, lambda i: (l,) + (0,) * (nd - 1))
    return pl.BlockSpec(shape, lambda b, i: (l,) + (0,) * (nd - 1))


def _const_spec(arr, n_grid):
    nd = arr.ndim
    if n_grid == 1:
        return pl.BlockSpec(arr.shape, lambda i: (0,) * nd)
    return pl.BlockSpec(arr.shape, lambda b, i: (0,) * nd)


def _rms_full(x, g):
    return x * lax.rsqrt(jnp.mean(x * x, axis=-1, keepdims=True) + EPS) * g


def _split_dot(s, m):
    hi = s.astype(BF16)
    lo = (s - hi.astype(F32)).astype(BF16)
    return jnp.dot(hi, m, preferred_element_type=F32) + jnp.dot(lo, m, preferred_element_type=F32)


def _rms_heads(x, g, m, dim):
    ss = jnp.dot((x * x).astype(BF16), m, preferred_element_type=F32)
    return x * lax.rsqrt(ss * (1.0 / dim) + EPS) * g


def _rms_lanes(x, g, dim):
    ss = jnp.sum(x * x, axis=-1, keepdims=True)
    return x * lax.rsqrt(ss * (1.0 / dim) + EPS) * g


def _rope(x, c_ref, sa_ref, sb_ref, quarter, rows):
    return (x * c_ref[rows, :] + pltpu.roll(x, 128 - quarter, 1) * sa_ref[rows, :]
            + pltpu.roll(x, quarter, 1) * sb_ref[rows, :])


def _adaln_kernel(cond_ref, w_ref, b_ref, o_ref):
    s = jax.nn.silu(cond_ref[...]).astype(BF16)
    o_ref[...] = jnp.dot(s, w_ref[...].astype(BF16), preferred_element_type=F32) + b_ref[...]


def _adaln(cond, w_ada, b_ada):
    n_out = w_ada.shape[-1]
    return pl.pallas_call(
        _adaln_kernel,
        grid=(DEPTH, n_out // ADA_COLS),
        in_specs=[pl.BlockSpec((8, D_MODEL), lambda l, j: (0, 0)),
                  pl.BlockSpec((None, D_MODEL, ADA_COLS), lambda l, j: (l, 0, j)),
                  pl.BlockSpec((None, 1, ADA_COLS), lambda l, j: (l, 0, j))],
        out_specs=pl.BlockSpec((None, 8, ADA_COLS), lambda l, j: (l, 0, j)),
        out_shape=jax.ShapeDtypeStruct((DEPTH, 8, n_out), F32),
        compiler_params=_params(2),
        name="adaln",
    )(cond, w_ada, b_ada.reshape(DEPTH, 1, n_out))


def _pre_kernel(*refs, rope, state_seq, n_carried):
    (x_ref, mod_ref, n1_ref, win_ref, wuq_ref, wuk_ref, wuv_ref, ones_ref, m256_ref, m128_ref,
     g_naq, g_nak, g_qa, g_kv, g_mq, g_mk, g_sq, g_sk) = refs[:18]
    pos = 18
    if rope:
        cm, sam, sbm, cs, sas, sbs = refs[pos:pos + 6]
        pos += 6
    pos += n_carried
    (o_naq, o_nak, o_nav, o_pool, o_mq, o_mk, o_mv, o_sq, o_sk, o_sv) = refs[pos:pos + 10]
    pos += 10
    if state_seq:
        state_refs = refs[pos:pos + 6]

    mod = mod_ref[...]
    sh1 = mod[:, 0:D_MODEL]
    gain1 = n1_ref[...] * (1.0 + mod[:, D_MODEL:2 * D_MODEL])
    m256 = m256_ref[...]
    for r0 in range(0, x_ref.shape[0], PRE_CHAIN):
        rs = slice(r0, r0 + PRE_CHAIN)
        x = x_ref[rs, :]
        h = x * lax.rsqrt(jnp.mean(x * x, axis=-1, keepdims=True) + EPS) * gain1 + sh1
        y = jnp.dot(h.astype(BF16), win_ref[...], preferred_element_type=F32)

        na_k = _rms_heads(y[:, _C_NA_K:_C_NA_K + 256], g_nak[...], m256, HEAD_DIM)
        na_v = y[:, _C_NA_V:_C_NA_V + 256]
        o_naq[rs, :] = _rms_heads(y[:, _C_NA_Q:_C_NA_Q + 256], g_naq[...], m256, HEAD_DIM).astype(BF16)
        o_nak[rs, :] = na_k.astype(BF16)
        o_nav[rs, :] = na_v.astype(BF16)
        o_pool[rs, :] = y[:, _C_POOL:_C_POOL + 256]

        cq = _rms_full(y[:, _C_CQ:_C_CQ + MLA_Q_RANK], g_qa[...])
        mq = jnp.dot(cq.astype(BF16), wuq_ref[...], preferred_element_type=F32)
        ckv = _rms_full(y[:, _C_CKV:_C_CKV + MLA_KV_RANK], g_kv[...])
        ckv_b = ckv.astype(BF16)
        mk = jnp.dot(ckv_b, wuk_ref[...], preferred_element_type=F32)
        kpe_block = y[:, _C_KPE_BLOCK:_C_KPE_BLOCK + MLA_PAD]
        o_mv[rs, :] = (jnp.dot(ckv_b, wuv_ref[...], preferred_element_type=F32) + ones_ref[...]).astype(BF16)
        for hd in range(MLA_HEADS):
            cols = slice(hd * MLA_PAD, (hd + 1) * MLA_PAD)
            qh = _rms_lanes(mq[:, cols], g_mq[:, cols], MLA_QK)
            kh = _rms_lanes(mk[:, cols] + kpe_block, g_mk[:, cols], MLA_QK)
            if rope:
                qh = _rope(qh, cm, sam, sbm, MLA_ROPE // 4, rs)
                kh = _rope(kh, cm, sam, sbm, MLA_ROPE // 4, rs)
            o_mq[rs, cols] = qh.astype(BF16)
            o_mk[rs, cols] = kh.astype(BF16)

        sq = _rms_heads(y[:, _C_SWA_Q:_C_SWA_Q + 256], g_sq[...], m256, HEAD_DIM)
        sk = _rms_heads(y[:, _C_SWA_K:_C_SWA_K + 128], g_sk[...], m128_ref[...], HEAD_DIM)
        sv = y[:, _C_SWA_V:_C_SWA_V + 128]
        if state_seq:
            seq, s0 = r0 // state_seq, r0 % state_seq
            for ref, val in zip(state_refs, (na_k, na_v, ckv, y[:, _C_KPE:_C_KPE + MLA_ROPE], sk, sv)):
                ref[seq, s0:s0 + PRE_CHAIN, :] = val
        if rope:
            for blk in range(2):
                cols = slice(blk * 128, (blk + 1) * 128)
                o_sq[rs, cols] = _rope(sq[:, cols], cs, sas, sbs, HEAD_DIM // 4, rs).astype(BF16)
            sk = _rope(sk, cs, sas, sbs, HEAD_DIM // 4, rs)
        else:
            o_sq[rs, :] = sq.astype(BF16)
        o_sk[rs, :] = sk.astype(BF16)
        o_sv[rs, :] = sv.astype(BF16)


_STATE_WIDTHS = (NA_HEADS * HEAD_DIM, NA_HEADS * HEAD_DIM, MLA_KV_RANK, MLA_ROPE,
                 SWA_KV_HEADS * HEAD_DIM, SWA_KV_HEADS * HEAD_DIM)


def _pre(x, mod, l, wts, consts, rope_tabs, *, tokens_per_mod, first_mod_row, name, state_seq=0, carried=None):
    t = x.shape[0]
    tm = PRE_TILE
    rope = rope_tabs is not None
    tiles_per_mod = tokens_per_mod // tm
    n_pos = rope_tabs[0].shape[0] // tm if rope else 1

    def row(width, dtype):
        return jax.ShapeDtypeStruct((t, width), dtype)

    def tok(width):
        return pl.BlockSpec((tm, width), lambda i: (i, 0))

    in_specs = [tok(D_MODEL),
                pl.BlockSpec((None, None, 1, 6 * D_MODEL), lambda i: (l, first_mod_row + i // tiles_per_mod, 0, 0)),
                _layer_spec(wts["norm1"], l, 1), _layer_spec(wts["w_in"], l, 1), _layer_spec(wts["w_uq"], l, 1),
                _layer_spec(wts["w_uk"], l, 1), _layer_spec(wts["w_uv"], l, 1),
                _const_spec(consts["v_ones"], 1), _const_spec(consts["m256"], 1), _const_spec(consts["m128"], 1)]
    args = [x, mod, wts["norm1"], wts["w_in"], wts["w_uq"], wts["w_uk"], wts["w_uv"],
            consts["v_ones"], consts["m256"], consts["m128"]]
    for nm in ("g_naq", "g_nak", "g_qa", "g_kv", "g_mq", "g_mk", "g_sq", "g_sk"):
        in_specs.append(_layer_spec(wts[nm], l, 1))
        args.append(wts[nm])
    if rope:
        for tab in rope_tabs:
            in_specs.append(pl.BlockSpec((tm, tab.shape[1]), lambda i: (i % n_pos, 0)))
            args.append(tab)
    widths = (256, 256, 256, 256, 512, 512, 512, 256, 128, 128)
    dtypes = (BF16, BF16, BF16, F32, BF16, BF16, BF16, BF16, BF16, BF16)
    out_shape = [row(w, d) for w, d in zip(widths, dtypes)]
    out_specs = [tok(w) for w in widths]
    aliases = {}
    n_carried = 0
    if state_seq:
        seqs_per_tile = tm // state_seq
        if carried is not None:
            n_carried = len(carried)
            for k, buf in enumerate(carried):
                aliases[len(args)] = len(out_shape) + k
                in_specs.append(pl.BlockSpec(memory_space=pl.ANY))
                args.append(buf)
        for w in _STATE_WIDTHS:
            out_shape.append(jax.ShapeDtypeStruct((t // state_seq, DEPTH, state_seq, w), F32))
            out_specs.append(pl.BlockSpec((seqs_per_tile, None, state_seq, w), lambda i: (i, l, 0, 0)))
    return pl.pallas_call(
        functools.partial(_pre_kernel, rope=rope, state_seq=state_seq, n_carried=n_carried),
        grid=(t // tm,), in_specs=in_specs, out_specs=out_specs, out_shape=out_shape,
        input_output_aliases=aliases, compiler_params=_params(1), name=name,
    )(*args)


def _mla_cache_kernel(ckv_ref, kpe_ref, wuk_ref, wuv_ref, e_ref, ones_ref, m512_ref, g_mk, ok_ref, ov_ref):
    ckv_b = ckv_ref[...].astype(BF16)
    mk = jnp.dot(ckv_b, wuk_ref[...], preferred_element_type=F32) + _split_dot(kpe_ref[...], e_ref[...])
    ok_ref[...] = _rms_heads(mk, g_mk[...], m512_ref[...], MLA_QK).astype(BF16)
    ov_ref[...] = (jnp.dot(ckv_b, wuv_ref[...], preferred_element_type=F32) + ones_ref[...]).astype(BF16)


def _mla_cache(c_ckv, c_kpe, wts, consts):
    b, _, n, _ = c_ckv.shape
    width = MLA_HEADS * MLA_PAD

    def lay(arr):
        nd = arr.ndim
        return pl.BlockSpec((None,) + tuple(arr.shape[1:]), lambda l, i: (l,) + (0,) * (nd - 1))

    return pl.pallas_call(
        _mla_cache_kernel,
        grid=(DEPTH, b),
        in_specs=[pl.BlockSpec((None, None, n, MLA_KV_RANK), lambda l, i: (i, l, 0, 0)),
                  pl.BlockSpec((None, None, n, MLA_ROPE), lambda l, i: (i, l, 0, 0)),
                  lay(wts["w_uk"]), lay(wts["w_uv"]), _const_spec(consts["e_kpe"], 2),
                  _const_spec(consts["v_ones"], 2), _const_spec(consts["m512"], 2), lay(wts["g_mk"])],
        out_specs=[pl.BlockSpec((None, None, n, width), lambda l, i: (l, i, 0, 0)),
                   pl.BlockSpec((None, None, n, width), lambda l, i: (l, i, 0, 0))],
        out_shape=[jax.ShapeDtypeStruct((DEPTH, b, n, width), BF16),
                   jax.ShapeDtypeStruct((DEPTH, b, n, width), BF16)],
        compiler_params=_params(2), name="mla_cache",
    )(c_ckv, c_kpe, wts["w_uk"], wts["w_uv"], consts["e_kpe"], consts["v_ones"], consts["m512"], wts["g_mk"])


_NA_VARIANTS = ((0, 0), (4, 0), (24, 20), (28, 20))


def _na_bias_kernel(rpb_ref, o_ref, *, grid_rows):
    l, hd = pl.program_id(0), pl.program_id(1)
    n_dr, n_dc = 2 * NA_WIN_ROWS - 1, 2 * NA_WIN_COLS - 1
    base = (l * NA_HEADS + hd) * (n_dr * n_dc)
    c = lax.broadcasted_iota(jnp.int32, (GRID_W, GRID_W), 0)
    kc = lax.broadcasted_iota(jnp.int32, (GRID_W, GRID_W), 1)
    dcv = kc - c + (NA_WIN_COLS - 1)
    col_start = jnp.clip(c - NA_WIN_COLS // 2, 0, GRID_W - NA_WIN_COLS)
    col_ok = (kc >= col_start) & (kc < col_start + NA_WIN_COLS)
    neg = jnp.full((GRID_W, GRID_W), NEG, F32)
    blocks = {}

    def block(dr):
        if dr not in blocks:
            t = jnp.zeros((GRID_W, GRID_W), F32)
            for dc in range(n_dc):
                t = jnp.where(dcv == dc, rpb_ref[base + dr * n_dc + dc] * LOG2E, t)
            blocks[dr] = jnp.where(col_ok, t, NEG)
        return blocks[dr]

    for v, (r0, w0) in enumerate(_NA_VARIANTS):
        for rl in range(NA_TILE_ROWS):
            r = r0 + rl
            kr0 = min(max(r - NA_WIN_ROWS // 2, 0), grid_rows - NA_WIN_ROWS)
            for j in range(NA_KEY_ROWS):
                kr = w0 + j
                inside = kr0 <= kr < kr0 + NA_WIN_ROWS
                o_ref[v, rl * GRID_W:(rl + 1) * GRID_W, j * GRID_W:(j + 1) * GRID_W] = (
                    block(kr - r + NA_WIN_ROWS - 1) if inside else neg)


def _na_bias(na_rpb, grid_rows):
    n_k = NA_KEY_ROWS * GRID_W
    return pl.pallas_call(
        functools.partial(_na_bias_kernel, grid_rows=grid_rows),
        grid=(DEPTH, NA_HEADS),
        in_specs=[pl.BlockSpec(memory_space=pltpu.SMEM)],
        out_specs=pl.BlockSpec((None, len(_NA_VARIANTS), None, Q_TILE, n_k), lambda l, h: (l, 0, h, 0, 0)),
        out_shape=jax.ShapeDtypeStruct((DEPTH, len(_NA_VARIANTS), NA_HEADS, Q_TILE, n_k), F32),
        compiler_params=_params(2), name="na_bias",
    )(na_rpb.reshape(-1))


def _attn_kernel(*refs, tq, wk, wv, dl, n_slabs, mode, has_ctx, has_sink, n_keys):
    q_ref, k_ref, v_ref = refs[:3]
    pos = 3
    if has_ctx:
        ck_ref, cv_ref = refs[pos:pos + 2]
        pos += 2
    if mode == "na":
        bias_ref = refs[pos]
        pos += 1
    if has_sink:
        sink_ref = refs[pos]
        pos += 1
    o_ref = refs[pos]
    i = pl.program_id(1)
    hs = wk // dl

    valid = None
    if mode == "full":
        kl, vl = k_ref[...], v_ref[...]
    elif mode == "swa":
        w0 = pl.multiple_of(jnp.clip(i * tq - SWA_WINDOW, 0, n_keys - SWA_KEYS), SWA_WINDOW)
        kl, vl = k_ref[pl.ds(w0, SWA_KEYS), :], v_ref[pl.ds(w0, SWA_KEYS), :]
        qpos = i * tq + lax.broadcasted_iota(jnp.int32, (tq, SWA_KEYS), 0)
        kpos = w0 + lax.broadcasted_iota(jnp.int32, (tq, SWA_KEYS), 1)
        valid = jnp.abs(kpos - qpos) <= SWA_WINDOW
    else:
        last = n_keys // GRID_W - NA_KEY_ROWS
        w0 = pl.multiple_of(jnp.clip(i * NA_TILE_ROWS - NA_WIN_ROWS // 2, 0, last) * GRID_W, GRID_W)
        kl, vl = k_ref[pl.ds(w0, NA_KEY_ROWS * GRID_W), :], v_ref[pl.ds(w0, NA_KEY_ROWS * GRID_W), :]
    if has_ctx:
        ck, cv = ck_ref[...], cv_ref[...]

    lane_k = lax.broadcasted_iota(jnp.int32, (1, wk), 1)
    lane_v = lax.broadcasted_iota(jnp.int32, (1, wv), 1)
    nt = (((1,), (1,)), ((), ()))
    for s in range(n_slabs):
        qs = q_ref[:, s * wk:(s + 1) * wk]
        acc = jnp.zeros((tq, wv), F32)
        for j in range(hs):
            head = j * n_slabs + s
            qm = jnp.where((lane_k >= j * dl) & (lane_k < (j + 1) * dl), qs, jnp.zeros_like(qs))
            s_loc = lax.dot_general(qm, kl, nt, preferred_element_type=F32)
            if mode == "na":
                s_loc = s_loc + bias_ref[head]
            if valid is not None:
                s_loc = jnp.where(valid, s_loc, NEG)
            m = jnp.max(s_loc, axis=-1, keepdims=True)
            if has_ctx:
                s_ctx = lax.dot_general(qm, ck, nt, preferred_element_type=F32)
                m = jnp.maximum(m, jnp.max(s_ctx, axis=-1, keepdims=True))
            if has_sink:
                sink = sink_ref[head] * LOG2E
                m = jnp.maximum(m, sink)
            p_loc = jnp.exp2(s_loc - m)
            den = jnp.sum(p_loc, axis=-1, keepdims=True)
            o = jnp.dot(p_loc.astype(BF16), vl, preferred_element_type=F32)
            if has_ctx:
                p_ctx = jnp.exp2(s_ctx - m)
                den = den + jnp.sum(p_ctx, axis=-1, keepdims=True)
                o = o + jnp.dot(p_ctx.astype(BF16), cv, preferred_element_type=F32)
            if has_sink:
                den = den + jnp.exp2(sink - m)
            out_lanes = (lane_v >= j * HEAD_DIM) & (lane_v < (j + 1) * HEAD_DIM)
            acc = acc + jnp.where(out_lanes, o * (1.0 / den), 0.0)
        o_ref[:, s * wv:(s + 1) * wv] = acc.astype(o_ref.dtype)


def _attention(q, k, v, *, dl, mode="full", ck=None, cv=None, ctx_layer=None, bias=None, bias_layer=None,
               sink=None, name):
    b, n, wq = q.shape
    _, nk, wk = k.shape
    wv = v.shape[-1]
    n_slabs = wq // wk
    tq = min(Q_TILE, n)
    in_specs = [pl.BlockSpec((None, tq, wq), lambda bi, i: (bi, i, 0)),
                pl.BlockSpec((None, nk, wk), lambda bi, i: (bi, 0, 0)),
                pl.BlockSpec((None, nk, wv), lambda bi, i: (bi, 0, 0))]
    args = [q, k, v]
    if ck is not None:
        n_ctx = ck.shape[2]
        in_specs += [pl.BlockSpec((None, None, n_ctx, wk), lambda bi, i: (ctx_layer, bi, 0, 0)),
                     pl.BlockSpec((None, None, n_ctx, wv), lambda bi, i: (ctx_layer, bi, 0, 0))]
        args += [ck, cv]
    if mode == "na":
        n_tiles = n // tq

        def variant(bi, i):
            v_idx = jnp.where(i == 0, 0, jnp.where(i <= n_tiles - 3, 1, i - (n_tiles - 4)))
            return (bias_layer, v_idx, 0, 0, 0)

        in_specs.append(pl.BlockSpec((None, None) + tuple(bias.shape[2:]), variant))
        args.append(bias)
    if sink is not None:
        in_specs.append(pl.BlockSpec(memory_space=pltpu.SMEM))
        args.append(sink)
    return pl.pallas_call(
        functools.partial(_attn_kernel, tq=tq, wk=wk, wv=wv, dl=dl, n_slabs=n_slabs, mode=mode,
                          has_ctx=ck is not None, has_sink=sink is not None, n_keys=nk),
        grid=(b, n // tq), in_specs=in_specs,
        out_specs=pl.BlockSpec((None, tq, n_slabs * wv), lambda bi, i: (bi, i, 0)),
        out_shape=jax.ShapeDtypeStruct((b, n, n_slabs * wv), BF16),
        compiler_params=_params(2), name=name,
    )(*args)


def _mla_attn_kernel(*refs, tq, has_ctx):
    q_ref, k_ref, v_ref = refs[:3]
    pos = 3
    if has_ctx:
        ck_ref, cv_ref = refs[pos:pos + 2]
        pos += 2
    o_ref, s_ref = refs[pos], refs[pos + 1]
    blocks = [(k_ref, v_ref, off) for off in range(0, k_ref.shape[0], KEY_BLOCK)]
    if has_ctx:
        blocks += [(ck_ref, cv_ref, off) for off in range(0, ck_ref.shape[0], KEY_BLOCK)]
    nt = (((1,), (1,)), ((), ()))
    lane = lax.broadcasted_iota(jnp.int32, (1, MLA_PAD), 1)
    outs = []
    for hd in range(MLA_HEADS):
        cols = slice(hd * MLA_PAD, (hd + 1) * MLA_PAD)
        qh = q_ref[:, cols]
        m_part = jnp.full((tq, 128), NEG, F32)
        for bi, (kr, _, off) in enumerate(blocks):
            sb = lax.dot_general(qh, kr[off:off + KEY_BLOCK, cols], nt, preferred_element_type=F32)
            s_ref[:, bi * KEY_BLOCK:(bi + 1) * KEY_BLOCK] = sb
            m_part = jnp.maximum(m_part, jnp.maximum(sb[:, :128], sb[:, 128:]))
        m = jnp.max(m_part, axis=-1, keepdims=True)
        o = jnp.zeros((tq, MLA_PAD), F32)
        for bi, (_, vr, off) in enumerate(blocks):
            p = jnp.exp2(s_ref[:, bi * KEY_BLOCK:(bi + 1) * KEY_BLOCK] - m)
            o = o + jnp.dot(p.astype(BF16), vr[off:off + KEY_BLOCK, cols], preferred_element_type=F32)
        outs.append(o * pltpu.roll(1.0 / o, MLA_V, 1))
    for hd in range(0, MLA_HEADS, 2):
        pair = jnp.where(lane < MLA_V, outs[hd], pltpu.roll(outs[hd + 1], MLA_V, 1))
        o_ref[:, (hd // 2) * 128:(hd // 2 + 1) * 128] = pair.astype(o_ref.dtype)


def _mla_attention(q, k, v, *, ck=None, cv=None, ctx_layer=None, name):
    b, n, w = q.shape
    nk = k.shape[1]
    tq = min(Q_TILE, n)
    in_specs = [pl.BlockSpec((None, tq, w), lambda bi, i: (bi, i, 0)),
                pl.BlockSpec((None, nk, w), lambda bi, i: (bi, 0, 0)),
                pl.BlockSpec((None, nk, w), lambda bi, i: (bi, 0, 0))]
    args = [q, k, v]
    n_ctx = 0
    if ck is not None:
        n_ctx = ck.shape[2]
        in_specs += [pl.BlockSpec((None, None, n_ctx, w), lambda bi, i: (ctx_layer, bi, 0, 0)),
                     pl.BlockSpec((None, None, n_ctx, w), lambda bi, i: (ctx_layer, bi, 0, 0))]
        args += [ck, cv]
    return pl.pallas_call(
        functools.partial(_mla_attn_kernel, tq=tq, has_ctx=ck is not None),
        grid=(b, n // tq), in_specs=in_specs,
        out_specs=pl.BlockSpec((None, tq, MLA_HEADS * MLA_V), lambda bi, i: (bi, i, 0)),
        out_shape=jax.ShapeDtypeStruct((b, n, MLA_HEADS * MLA_V), BF16),
        scratch_shapes=[pltpu.VMEM((tq, nk + n_ctx), F32)],
        compiler_params=_params(2), name=name,
    )(*args)


def _pool_kernel(x_ref, w_ref, s_ref, o_ref, *, n):
    x = x_ref[...]
    t = lax.broadcasted_iota(jnp.int32, x.shape, 0)
    lane = lax.broadcasted_iota(jnp.int32, x.shape, 1)

    def later(a, d):
        return jnp.where(t < n - d, pltpu.roll(a, n - d, 0), 0.0)

    def earlier(a, d):
        return jnp.where(t >= d, pltpu.roll(a, d, 0), 0.0)

    fwd, bwd = {1: x}, {1: x}
    for k in (1, 2, 4):
        fwd[2 * k] = fwd[k] + later(fwd[k], k)
        bwd[2 * k] = bwd[k] + earlier(bwd[k], k)
    pooled = jnp.zeros_like(x)
    for gi, w in enumerate(POOL_WINDOWS):
        total = fwd[w // 2] + earlier(bwd[w // 2], 1)
        count = (jnp.clip(t + w // 2, 0, n) - jnp.clip(t - w // 2, 0, n)).astype(F32)
        in_group = (lane >= gi * POOL_GW) & (lane < (gi + 1) * POOL_GW)
        pooled = jnp.where(in_group, total / count - x, pooled)
    y = jnp.dot(pooled.astype(BF16), w_ref[...], preferred_element_type=F32)
    o_ref[...] = (y * s_ref[...]).astype(o_ref.dtype)


def _pool(x, wts, l, name):
    b, n, w = x.shape
    return pl.pallas_call(
        functools.partial(_pool_kernel, n=n),
        grid=(b,),
        in_specs=[pl.BlockSpec((None, n, w), lambda i: (i, 0, 0)),
                  _layer_spec(wts["pool_w"], l, 1), _layer_spec(wts["pool_scale"], l, 1)],
        out_specs=pl.BlockSpec((None, n, w), lambda i: (i, 0, 0)),
        out_shape=jax.ShapeDtypeStruct((b, n, w), BF16),
        compiler_params=_params(1), name=name,
    )(x, wts["pool_w"], wts["pool_scale"])


def _post_kernel(x_ref, p0, p1, p2, p3, mod_ref, n2_ref, wo_ref, w1_ref, w2_ref, o_ref):
    x = x_ref[...]
    mod = mod_ref[...]
    g1 = mod[:, 2 * D_MODEL:3 * D_MODEL]
    sh2, sc2, g2 = (mod[:, 3 * D_MODEL:4 * D_MODEL], mod[:, 4 * D_MODEL:5 * D_MODEL],
                    mod[:, 5 * D_MODEL:6 * D_MODEL])
    a = jnp.zeros(x.shape, F32)
    for i, p in enumerate((p0, p1, p2, p3)):
        a = a + jnp.dot(p[...], wo_ref[i * 256:(i + 1) * 256, :], preferred_element_type=F32)
    x = x + g1 * a
    h = (_rms_full(x, n2_ref[...]) * (1.0 + sc2) + sh2).astype(BF16)
    acc = jnp.zeros(x.shape, F32)
    for c in range(D_FF // FF_CHUNK):
        z = jnp.maximum(jnp.dot(h, w1_ref[:, c * FF_CHUNK:(c + 1) * FF_CHUNK], preferred_element_type=F32), 0.0)
        acc = acc + jnp.dot((z * z).astype(BF16), w2_ref[c * FF_CHUNK:(c + 1) * FF_CHUNK, :],
                            preferred_element_type=F32)
    o_ref[...] = x + g2 * acc


def _post(x, parts, mod, l, wts, *, tokens_per_mod, first_mod_row, name):
    t = x.shape[0]
    tm = TOKEN_TILE
    tiles_per_mod = tokens_per_mod // tm

    def tok(width):
        return pl.BlockSpec((tm, width), lambda i: (i, 0))

    def resident(arr):
        nd = arr.ndim
        return pl.BlockSpec((None,) + tuple(arr.shape[1:]), lambda i: (l,) + (0,) * (nd - 1),
                            pipeline_mode=pl.Buffered(1))

    return pl.pallas_call(
        _post_kernel,
        grid=(t // tm,),
        in_specs=[tok(D_MODEL), tok(256), tok(256), tok(256), tok(256),
                  pl.BlockSpec((None, None, 1, 6 * D_MODEL), lambda i: (l, first_mod_row + i // tiles_per_mod, 0, 0)),
                  _layer_spec(wts["norm2"], l, 1), resident(wts["w_out"]), resident(wts["w1"]), resident(wts["w2"])],
        out_specs=tok(D_MODEL),
        out_shape=jax.ShapeDtypeStruct((t, D_MODEL), F32),
        compiler_params=_params(1), name=name,
    )(x, *parts, mod, wts["norm2"], wts["w_out"], wts["w1"], wts["w2"])


def _block_ones(width, group):
    idx = np.arange(width) // group
    return jnp.asarray(idx[:, None] == idx[None, :], BF16)


def _rope_tables(n_tokens, head_lanes, first, rope_dims):
    half, quarter = rope_dims // 2, rope_dims // 4
    t = jnp.arange(n_tokens, dtype=jnp.int32)
    rows, cols = t // GRID_W, t % GRID_W
    freqs = ROPE_THETA ** (-jnp.arange(quarter, dtype=F32) / quarter)
    lane = np.arange(128) % head_lanes - first
    active = (lane >= 0) & (lane < rope_dims)
    lane = np.where(active, lane, 0)
    use_cols = lane >= half
    within = lane % half
    upper = within >= quarter
    f_idx = within % quarter
    pos = jnp.where(jnp.asarray(use_cols)[None, :], cols[:, None], rows[:, None]).astype(F32)
    ang = pos * freqs[jnp.asarray(f_idx)][None, :]
    act = jnp.asarray(active)[None, :]
    up = jnp.asarray(upper)[None, :]
    cos = jnp.where(act, jnp.cos(ang), 1.0)
    sin = jnp.where(act, jnp.sin(ang), 0.0)
    return cos, jnp.where(up, 0.0, -sin), jnp.where(up, sin, 0.0)


def _prepare(norm1, norm2, w_in, w_out, na_q_norm, na_k_norm, pool_w, pool_scale, mla_q_a_norm, mla_w_uq,
             mla_kv_norm, mla_w_ukv, mla_q_norm, mla_k_norm, swa_q_norm, swa_k_norm, mlp_w1, mlp_w2):
    d = DEPTH
    offs = np.cumsum((256, 256, 256, 256, MLA_Q_RANK, MLA_KV_RANK, MLA_ROPE, 256, 128, 128))[:-1]
    (w_naq, w_nak, w_nav, w_pool, w_cq, w_ckv, w_kpe, w_sq, w_sk, w_sv) = jnp.split(w_in, offs, axis=-1)
    w_sq = w_sq.reshape(d, D_MODEL, 2, 2, HEAD_DIM).swapaxes(2, 3).reshape(d, D_MODEL, 256)
    w_kpe_block = jnp.pad(w_kpe, ((0, 0), (0, 0), (MLA_NOPE, MLA_PAD - MLA_QK)))
    w_in_r = jnp.concatenate([w_naq, w_nak, w_nav, w_pool, w_cq, w_ckv, w_sq, w_sk, w_sv, w_kpe_block], axis=-1)

    def pad_heads(a, width):
        a = a.reshape(a.shape[:-1] + (MLA_HEADS, width))
        a = jnp.pad(a, [(0, 0)] * (a.ndim - 1) + [(0, MLA_PAD - width)])
        return a.reshape(a.shape[:-2] + (MLA_HEADS * MLA_PAD,))

    ukv = mla_w_ukv.reshape(d, MLA_KV_RANK, MLA_HEADS, MLA_NOPE + MLA_V)
    w_out_s = w_out[:, 768:1024].reshape(d, 2, 2, HEAD_DIM, D_MODEL).swapaxes(1, 2).reshape(d, 256, D_MODEL)
    eye = jnp.eye(POOL_GROUPS, dtype=pool_w.dtype)
    pool_bd = jnp.einsum("lgce,gh->lgche", pool_w, eye).reshape(d, 256, 256)

    def gain(g, reps, scale=1.0):
        return (jnp.tile(g, (1, reps)) * scale).reshape(d, 1, -1)

    return {
        "norm1": norm1.reshape(d, 1, D_MODEL), "norm2": norm2.reshape(d, 1, D_MODEL),
        "w_in": w_in_r.astype(BF16),
        "w_uq": pad_heads(mla_w_uq, MLA_QK).astype(BF16),
        "w_uk": pad_heads(ukv[..., :MLA_NOPE].reshape(d, MLA_KV_RANK, -1), MLA_NOPE).astype(BF16),
        "w_uv": pad_heads(ukv[..., MLA_NOPE:].reshape(d, MLA_KV_RANK, -1), MLA_V).astype(BF16),
        "w_out": jnp.concatenate([w_out[:, :768], w_out_s], axis=1).astype(BF16),
        "w1": mlp_w1.astype(BF16), "w2": mlp_w2.astype(BF16),
        "pool_w": pool_bd.astype(BF16), "pool_scale": pool_scale.reshape(d, 1, 256),
        "g_naq": gain(na_q_norm, NA_HEADS, HEAD_DIM ** -0.5 * LOG2E), "g_nak": gain(na_k_norm, NA_HEADS),
        "g_qa": mla_q_a_norm.reshape(d, 1, MLA_Q_RANK), "g_kv": mla_kv_norm.reshape(d, 1, MLA_KV_RANK),
        "g_mq": pad_heads(jnp.tile(mla_q_norm, (1, MLA_HEADS)) * (MLA_QK ** -0.5 * LOG2E), MLA_QK).reshape(d, 1, -1),
        "g_mk": pad_heads(jnp.tile(mla_k_norm, (1, MLA_HEADS)), MLA_QK).reshape(d, 1, -1),
        "g_sq": gain(swa_q_norm, SWA_HEADS, HEAD_DIM ** -0.5 * LOG2E), "g_sk": gain(swa_k_norm, SWA_KV_HEADS),
    }


def _constants():
    width = MLA_HEADS * MLA_PAD
    e = np.zeros((MLA_ROPE, width), np.float32)
    for h in range(MLA_HEADS):
        e[np.arange(MLA_ROPE), h * MLA_PAD + MLA_NOPE + np.arange(MLA_ROPE)] = 1.0
    v_ones = (np.arange(width) % MLA_PAD >= MLA_V).astype(np.float32)[None, :]
    return {"e_kpe": jnp.asarray(e, BF16), "v_ones": jnp.asarray(v_ones), "m256": _block_ones(256, HEAD_DIM),
            "m128": _block_ones(128, HEAD_DIM), "m512": _block_ones(width, MLA_PAD)}


def kernel(x_prompt, x_sample, cache_na_k, cache_na_v, cache_mla_ckv, cache_mla_kpe, cache_swa_k, cache_swa_v, c, c_ctx, norm1, norm2, w_ada, b_ada, w_in, w_out, na_q_norm, na_k_norm, na_rpb, pool_w, pool_scale, mla_q_a_norm, mla_w_uq, mla_kv_norm, mla_w_ukv, mla_q_norm, mla_k_norm, swa_q_norm, swa_k_norm, swa_sink, mlp_w1, mlp_w2):
    bc, nc, _ = x_prompt.shape
    bl, nl, _ = x_sample.shape
    n_past = cache_na_k.shape[2]
    wts = _prepare(norm1, norm2, w_in, w_out, na_q_norm, na_k_norm, pool_w, pool_scale, mla_q_a_norm, mla_w_uq,
                   mla_kv_norm, mla_w_ukv, mla_q_norm, mla_k_norm, swa_q_norm, swa_k_norm, mlp_w1, mlp_w2)
    consts = _constants()
    rope_tabs = _rope_tables(nl, MLA_PAD, MLA_NOPE, MLA_ROPE) + _rope_tables(nl, HEAD_DIM, 0, HEAD_DIM)

    cond = jnp.concatenate([c_ctx[None, :], c, jnp.zeros((8 - 1 - bl, D_MODEL), c.dtype)], axis=0)
    mod = _adaln(cond, w_ada, b_ada).reshape(DEPTH, 8, 1, 6 * D_MODEL)

    def cache(a):
        return a.reshape(bl, DEPTH, n_past, -1).swapaxes(0, 1).astype(BF16)

    c_nak, c_nav, c_sk, c_sv = cache(cache_na_k), cache(cache_na_v), cache(cache_swa_k), cache(cache_swa_v)
    c_mk, c_mv = _mla_cache(cache_mla_ckv, cache_mla_kpe, wts, consts)
    na_bias = _na_bias(na_rpb, nl // GRID_W)

    xp = x_prompt.reshape(bc * nc, D_MODEL)
    xs = x_sample.reshape(bl * nl, D_MODEL)
    states = [jnp.zeros((bc, DEPTH, nc, w), F32) for w in _STATE_WIDTHS]
    for l in range(DEPTH):
        sink = swa_sink[l]
        outs = _pre(xp, mod, l, wts, consts, None, tokens_per_mod=bc * nc, first_mod_row=0, state_seq=nc,
                    carried=states, name="pre_ctx")
        naq, nak, nav, pool_in, mq, mk, mv, sq, sk, sv = [o.reshape(bc, nc, -1) for o in outs[:10]]
        states = outs[10:]
        parts = [
            _attention(naq, nak, nav, dl=HEAD_DIM, name="ctx_na"),
            _pool(pool_in, wts, l, "ctx_pool"),
            _mla_attention(mq, mk, mv, name="ctx_mla"),
            _attention(sq, sk, sv, dl=HEAD_DIM, sink=sink, name="ctx_swa"),
        ]
        xp = _post(xp, [p.reshape(bc * nc, 256) for p in parts], mod, l, wts, tokens_per_mod=bc * nc,
                   first_mod_row=0, name="post_ctx")
        outs = _pre(xs, mod, l, wts, consts, rope_tabs, tokens_per_mod=nl, first_mod_row=1, name="pre_lat")
        naq, nak, nav, pool_in, mq, mk, mv, sq, sk, sv = [o.reshape(bl, nl, -1) for o in outs]
        parts = [
            _attention(naq, nak, nav, dl=HEAD_DIM, mode="na", ck=c_nak, cv=c_nav, ctx_layer=l,
                       bias=na_bias, bias_layer=l, name="lat_na"),
            _pool(pool_in, wts, l, "lat_pool"),
            _mla_attention(mq, mk, mv, ck=c_mk, cv=c_mv, ctx_layer=l, name="lat_mla"),
            _attention(sq, sk, sv, dl=HEAD_DIM, mode="swa", ck=c_sk, cv=c_sv, ctx_layer=l,
                       sink=sink, name="lat_swa"),
        ]
        xs = _post(xs, [p.reshape(bl * nl, 256) for p in parts], mod, l, wts, tokens_per_mod=nl,
                   first_mod_row=1, name="post_lat")

    st_nak, st_nav, st_ckv, st_kpe, st_sk, st_sv = states
    return (xp.reshape(bc, nc, D_MODEL), xs.reshape(bl, nl, D_MODEL),
            st_nak.reshape(bc, DEPTH, nc, NA_HEADS, HEAD_DIM), st_nav.reshape(bc, DEPTH, nc, NA_HEADS, HEAD_DIM),
            st_ckv, st_kpe,
            st_sk.reshape(bc, DEPTH, nc, SWA_KV_HEADS, HEAD_DIM), st_sv.reshape(bc, DEPTH, nc, SWA_KV_HEADS, HEAD_DIM))
```

```python
import functools

import numpy as np
import jax
import jax.numpy as jnp
from jax import lax
from jax.experimental import pallas as pl
from jax.experimental.pallas import tpu as pltpu

F32, BF16 = jnp.float32, jnp.bfloat16

D_MODEL = 1024
DEPTH = 4
GRID_W = 64
HEAD_DIM = 64
NA_HEADS = 4
NA_WIN_ROWS = 8
NA_WIN_COLS = 16
POOL_GROUPS = 4
POOL_GW = 64
POOL_WINDOWS = (2, 4, 8, 16)
MLA_HEADS = 4
MLA_Q_RANK = 256
MLA_KV_RANK = 128
MLA_NOPE = 64
MLA_ROPE = 32
MLA_V = 64
MLA_QK = MLA_NOPE + MLA_ROPE
MLA_PAD = 128
SWA_HEADS = 4
SWA_KV_HEADS = 2
SWA_WINDOW = 128
D_FF = 4 * D_MODEL
ROPE_THETA = 10000.0
EPS = 1e-6
NEG = -1e30
LOG2E = 1.4426950408889634

TOKEN_TILE = 256
PRE_TILE = 512
PRE_CHAIN = 128
Q_TILE = 256
KEY_BLOCK = 256
NA_TILE_ROWS = Q_TILE // GRID_W
NA_KEY_ROWS = NA_TILE_ROWS + NA_WIN_ROWS
SWA_KEYS = Q_TILE + 2 * SWA_WINDOW
ADA_COLS = 2048
FF_CHUNK = 1024
W_IN_COLS = 2048
VMEM_LIMIT = 48 * 1024 * 1024

_C_NA_Q, _C_NA_K, _C_NA_V, _C_POOL, _C_CQ, _C_CKV, _C_SWA_Q, _C_SWA_K, _C_SWA_V, _C_KPE_BLOCK = (
    0, 256, 512, 768, 1024, 1280, 1408, 1664, 1792, 1920)
_C_KPE = _C_KPE_BLOCK + MLA_NOPE


def _params(n_grid):
    return pltpu.CompilerParams(dimension_semantics=("arbitrary",) * n_grid, vmem_limit_bytes=VMEM_LIMIT)


def _layer_spec(arr, l, n_grid):
    nd = arr.ndim
    shape = (None,) + tuple(arr.shape[1:])
    if n_grid == 1:
        return pl.BlockSpec(shape, lambda i: (l,) + (0,) * (nd - 1))
    return pl.BlockSpec(shape, lambda b, i: (l,) + (0,) * (nd - 1))


def _const_spec(arr, n_grid):
    nd = arr.ndim
    if n_grid == 1:
        return pl.BlockSpec(arr.shape, lambda i: (0,) * nd)
    return pl.BlockSpec(arr.shape, lambda b, i: (0,) * nd)


def _rms_full(x, g):
    return x * lax.rsqrt(jnp.mean(x * x, axis=-1, keepdims=True) + EPS) * g


def _split_dot(s, m):
    hi = s.astype(BF16)
    lo = (s - hi.astype(F32)).astype(BF16)
    return jnp.dot(hi, m, preferred_element_type=F32) + jnp.dot(lo, m, preferred_element_type=F32)


def _rms_heads(x, g, m, dim):
    ss = jnp.dot((x * x).astype(BF16), m, preferred_element_type=F32)
    return x * lax.rsqrt(ss * (1.0 / dim) + EPS) * g


def _rms_lanes(x, g, dim):
    ss = jnp.sum(x * x, axis=-1, keepdims=True)
    return x * lax.rsqrt(ss * (1.0 / dim) + EPS) * g


def _rope(x, c_ref, sa_ref, sb_ref, quarter, rows):
    return (x * c_ref[rows, :] + pltpu.roll(x, 128 - quarter, 1) * sa_ref[rows, :]
            + pltpu.roll(x, quarter, 1) * sb_ref[rows, :])


def _adaln_kernel(cond_ref, w_ref, b_ref, o_ref):
    s = jax.nn.silu(cond_ref[...]).astype(BF16)
    o_ref[...] = jnp.dot(s, w_ref[...].astype(BF16), preferred_element_type=F32) + b_ref[...]


def _adaln(cond, w_ada, b_ada):
    n_out = w_ada.shape[-1]
    return pl.pallas_call(
        _adaln_kernel,
        grid=(DEPTH, n_out // ADA_COLS),
        in_specs=[pl.BlockSpec((8, D_MODEL), lambda l, j: (0, 0)),
                  pl.BlockSpec((None, D_MODEL, ADA_COLS), lambda l, j: (l, 0, j)),
                  pl.BlockSpec((None, 1, ADA_COLS), lambda l, j: (l, 0, j))],
        out_specs=pl.BlockSpec((None, 8, ADA_COLS), lambda l, j: (l, 0, j)),
        out_shape=jax.ShapeDtypeStruct((DEPTH, 8, n_out), F32),
        compiler_params=_params(2),
        name="adaln",
    )(cond, w_ada, b_ada.reshape(DEPTH, 1, n_out))


def _pre_kernel(*refs, rope, state_seq, n_carried):
    (x_ref, mod_ref, n1_ref, win_ref, wuq_ref, wuk_ref, wuv_ref, ones_ref, m256_ref, m128_ref,
     g_naq, g_nak, g_qa, g_kv, g_mq, g_mk, g_sq, g_sk) = refs[:18]
    pos = 18
    if rope:
        cm, sam, sbm, cs, sas, sbs = refs[pos:pos + 6]
        pos += 6
    pos += n_carried
    (o_naq, o_nak, o_nav, o_pool, o_mq, o_mk, o_mv, o_sq, o_sk, o_sv) = refs[pos:pos + 10]
    pos += 10
    if state_seq:
        state_refs = refs[pos:pos + 6]

    mod = mod_ref[...]
    sh1 = mod[:, 0:D_MODEL]
    gain1 = n1_ref[...] * (1.0 + mod[:, D_MODEL:2 * D_MODEL])
    m256 = m256_ref[...]
    for r0 in range(0, x_ref.shape[0], PRE_CHAIN):
        rs = slice(r0, r0 + PRE_CHAIN)
        x = x_ref[rs, :]
        h = x * lax.rsqrt(jnp.mean(x * x, axis=-1, keepdims=True) + EPS) * gain1 + sh1
        y = jnp.dot(h.astype(BF16), win_ref[...], preferred_element_type=F32)

        na_k = _rms_heads(y[:, _C_NA_K:_C_NA_K + 256], g_nak[...], m256, HEAD_DIM)
        na_v = y[:, _C_NA_V:_C_NA_V + 256]
        o_naq[rs, :] = _rms_heads(y[:, _C_NA_Q:_C_NA_Q + 256], g_naq[...], m256, HEAD_DIM).astype(BF16)
        o_nak[rs, :] = na_k.astype(BF16)
        o_nav[rs, :] = na_v.astype(BF16)
        o_pool[rs, :] = y[:, _C_POOL:_C_POOL + 256]

        cq = _rms_full(y[:, _C_CQ:_C_CQ + MLA_Q_RANK], g_qa[...])
        mq = jnp.dot(cq.astype(BF16), wuq_ref[...], preferred_element_type=F32)
        ckv = _rms_full(y[:, _C_CKV:_C_CKV + MLA_KV_RANK], g_kv[...])
        ckv_b = ckv.astype(BF16)
        mk = jnp.dot(ckv_b, wuk_ref[...], preferred_element_type=F32)
        kpe_block = y[:, _C_KPE_BLOCK:_C_KPE_BLOCK + MLA_PAD]
        o_mv[rs, :] = (jnp.dot(ckv_b, wuv_ref[...], preferred_element_type=F32) + ones_ref[...]).astype(BF16)
        for hd in range(MLA_HEADS):
            cols = slice(hd * MLA_PAD, (hd + 1) * MLA_PAD)
            qh = _rms_lanes(mq[:, cols], g_mq[:, cols], MLA_QK)
            kh = _rms_lanes(mk[:, cols] + kpe_block, g_mk[:, cols], MLA_QK)
            if rope:
                qh = _rope(qh, cm, sam, sbm, MLA_ROPE // 4, rs)
                kh = _rope(kh, cm, sam, sbm, MLA_ROPE // 4, rs)
            o_mq[rs, cols] = qh.astype(BF16)
            o_mk[rs, cols] = kh.astype(BF16)

        sq = _rms_heads(y[:, _C_SWA_Q:_C_SWA_Q + 256], g_sq[...], m256, HEAD_DIM)
        sk = _rms_heads(y[:, _C_SWA_K:_C_SWA_K + 128], g_sk[...], m128_ref[...], HEAD_DIM)
        sv = y[:, _C_SWA_V:_C_SWA_V + 128]
        if state_seq:
            seq, s0 = r0 // state_seq, r0 % state_seq
            for ref, val in zip(state_refs, (na_k, na_v, ckv, y[:, _C_KPE:_C_KPE + MLA_ROPE], sk, sv)):
                ref[seq, s0:s0 + PRE_CHAIN, :] = val
        if rope:
            for blk in range(2):
                cols = slice(blk * 128, (blk + 1) * 128)
                o_sq[rs, cols] = _rope(sq[:, cols], cs, sas, sbs, HEAD_DIM // 4, rs).astype(BF16)
            sk = _rope(sk, cs, sas, sbs, HEAD_DIM // 4, rs)
        else:
            o_sq[rs, :] = sq.astype(BF16)
        o_sk[rs, :] = sk.astype(BF16)
        o_sv[rs, :] = sv.astype(BF16)


_STATE_WIDTHS = (NA_HEADS * HEAD_DIM, NA_HEADS * HEAD_DIM, MLA_KV_RANK, MLA_ROPE,
                 SWA_KV_HEADS * HEAD_DIM, SWA_KV_HEADS * HEAD_DIM)


def _pre(x, mod, l, wts, consts, rope_tabs, *, tokens_per_mod, first_mod_row, name, state_seq=0, carried=None):
    t = x.shape[0]
    tm = PRE_TILE
    rope = rope_tabs is not None
    tiles_per_mod = tokens_per_mod // tm
    n_pos = rope_tabs[0].shape[0] // tm if rope else 1

    def row(width, dtype):
        return jax.ShapeDtypeStruct((t, width), dtype)

    def tok(width):
        return pl.BlockSpec((tm, width), lambda i: (i, 0))

    in_specs = [tok(D_MODEL),
                pl.BlockSpec((None, None, 1, 6 * D_MODEL), lambda i: (l, first_mod_row + i // tiles_per_mod, 0, 0)),
                _layer_spec(wts["norm1"], l, 1), _layer_spec(wts["w_in"], l, 1), _layer_spec(wts["w_uq"], l, 1),
                _layer_spec(wts["w_uk"], l, 1), _layer_spec(wts["w_uv"], l, 1),
                _const_spec(consts["v_ones"], 1), _const_spec(consts["m256"], 1), _const_spec(consts["m128"], 1)]
    args = [x, mod, wts["norm1"], wts["w_in"], wts["w_uq"], wts["w_uk"], wts["w_uv"],
            consts["v_ones"], consts["m256"], consts["m128"]]
    for nm in ("g_naq", "g_nak", "g_qa", "g_kv", "g_mq", "g_mk", "g_sq", "g_sk"):
        in_specs.append(_layer_spec(wts[nm], l, 1))
        args.append(wts[nm])
    if rope:
        for tab in rope_tabs:
            in_specs.append(pl.BlockSpec((tm, tab.shape[1]), lambda i: (i % n_pos, 0)))
            args.append(tab)
    widths = (256, 256, 256, 256, 512, 512, 512, 256, 128, 128)
    dtypes = (BF16, BF16, BF16, F32, BF16, BF16, BF16, BF16, BF16, BF16)
    out_shape = [row(w, d) for w, d in zip(widths, dtypes)]
    out_specs = [tok(w) for w in widths]
    aliases = {}
    n_carried = 0
    if state_seq:
        seqs_per_tile = tm // state_seq
        if carried is not None:
            n_carried = len(carried)
            for k, buf in enumerate(carried):
                aliases[len(args)] = len(out_shape) + k
                in_specs.append(pl.BlockSpec(memory_space=pl.ANY))
                args.append(buf)
        for w in _STATE_WIDTHS:
            out_shape.append(jax.ShapeDtypeStruct((t // state_seq, DEPTH, state_seq, w), F32))
            out_specs.append(pl.BlockSpec((seqs_per_tile, None, state_seq, w), lambda i: (i, l, 0, 0)))
    return pl.pallas_call(
        functools.partial(_pre_kernel, rope=rope, state_seq=state_seq, n_carried=n_carried),
        grid=(t // tm,), in_specs=in_specs, out_specs=out_specs, out_shape=out_shape,
        input_output_aliases=aliases, compiler_params=_params(1), name=name,
    )(*args)


def _mla_cache_kernel(ckv_ref, kpe_ref, wuk_ref, wuv_ref, e_ref, ones_ref, m512_ref, g_mk, ok_ref, ov_ref):
    ckv_b = ckv_ref[...].astype(BF16)
    mk = jnp.dot(ckv_b, wuk_ref[...], preferred_element_type=F32) + _split_dot(kpe_ref[...], e_ref[...])
    ok_ref[...] = _rms_heads(mk, g_mk[...], m512_ref[...], MLA_QK).astype(BF16)
    ov_ref[...] = (jnp.dot(ckv_b, wuv_ref[...], preferred_element_type=F32) + ones_ref[...]).astype(BF16)


def _mla_cache(c_ckv, c_kpe, wts, consts):
    b, _, n, _ = c_ckv.shape
    width = MLA_HEADS * MLA_PAD

    def lay(arr):
        nd = arr.ndim
        return pl.BlockSpec((None,) + tuple(arr.shape[1:]), lambda l, i: (l,) + (0,) * (nd - 1))

    return pl.pallas_call(
        _mla_cache_kernel,
        grid=(DEPTH, b),
        in_specs=[pl.BlockSpec((None, None, n, MLA_KV_RANK), lambda l, i: (i, l, 0, 0)),
                  pl.BlockSpec((None, None, n, MLA_ROPE), lambda l, i: (i, l, 0, 0)),
                  lay(wts["w_uk"]), lay(wts["w_uv"]), _const_spec(consts["e_kpe"], 2),
                  _const_spec(consts["v_ones"], 2), _const_spec(consts["m512"], 2), lay(wts["g_mk"])],
        out_specs=[pl.BlockSpec((None, None, n, width), lambda l, i: (l, i, 0, 0)),
                   pl.BlockSpec((None, None, n, width), lambda l, i: (l, i, 0, 0))],
        out_shape=[jax.ShapeDtypeStruct((DEPTH, b, n, width), BF16),
                   jax.ShapeDtypeStruct((DEPTH, b, n, width), BF16)],
        compiler_params=_params(2), name="mla_cache",
    )(c_ckv, c_kpe, wts["w_uk"], wts["w_uv"], consts["e_kpe"], consts["v_ones"], consts["m512"], wts["g_mk"])


_NA_VARIANTS = ((0, 0), (4, 0), (24, 20), (28, 20))


def _na_bias_kernel(rpb_ref, o_ref, *, grid_rows):
    l, hd = pl.program_id(0), pl.program_id(1)
    n_dr, n_dc = 2 * NA_WIN_ROWS - 1, 2 * NA_WIN_COLS - 1
    base = (l * NA_HEADS + hd) * (n_dr * n_dc)
    c = lax.broadcasted_iota(jnp.int32, (GRID_W, GRID_W), 0)
    kc = lax.broadcasted_iota(jnp.int32, (GRID_W, GRID_W), 1)
    dcv = kc - c + (NA_WIN_COLS - 1)
    col_start = jnp.clip(c - NA_WIN_COLS // 2, 0, GRID_W - NA_WIN_COLS)
    col_ok = (kc >= col_start) & (kc < col_start + NA_WIN_COLS)
    neg = jnp.full((GRID_W, GRID_W), NEG, F32)
    blocks = {}

    def block(dr):
        if dr not in blocks:
            t = jnp.zeros((GRID_W, GRID_W), F32)
            for dc in range(n_dc):
                t = jnp.where(dcv == dc, rpb_ref[base + dr * n_dc + dc] * LOG2E, t)
            blocks[dr] = jnp.where(col_ok, t, NEG)
        return blocks[dr]

    for v, (r0, w0) in enumerate(_NA_VARIANTS):
        for rl in range(NA_TILE_ROWS):
            r = r0 + rl
            kr0 = min(max(r - NA_WIN_ROWS // 2, 0), grid_rows - NA_WIN_ROWS)
            for j in range(NA_KEY_ROWS):
                kr = w0 + j
                inside = kr0 <= kr < kr0 + NA_WIN_ROWS
                o_ref[v, rl * GRID_W:(rl + 1) * GRID_W, j * GRID_W:(j + 1) * GRID_W] = (
                    block(kr - r + NA_WIN_ROWS - 1) if inside else neg)


def _na_bias(na_rpb, grid_rows):
    n_k = NA_KEY_ROWS * GRID_W
    return pl.pallas_call(
        functools.partial(_na_bias_kernel, grid_rows=grid_rows),
        grid=(DEPTH, NA_HEADS),
        in_specs=[pl.BlockSpec(memory_space=pltpu.SMEM)],
        out_specs=pl.BlockSpec((None, len(_NA_VARIANTS), None, Q_TILE, n_k), lambda l, h: (l, 0, h, 0, 0)),
        out_shape=jax.ShapeDtypeStruct((DEPTH, len(_NA_VARIANTS), NA_HEADS, Q_TILE, n_k), F32),
        compiler_params=_params(2), name="na_bias",
    )(na_rpb.reshape(-1))


def _attn_kernel(*refs, tq, wk, wv, dl, n_slabs, mode, has_ctx, has_sink, n_keys):
    q_ref, k_ref, v_ref = refs[:3]
    pos = 3
    if has_ctx:
        ck_ref, cv_ref = refs[pos:pos + 2]
        pos += 2
    if mode == "na":
        bias_ref = refs[pos]
        pos += 1
    if has_sink:
        sink_ref = refs[pos]
        pos += 1
    o_ref = refs[pos]
    i = pl.program_id(1)
    hs = wk // dl

    valid = None
    if mode == "full":
        kl, vl = k_ref[...], v_ref[...]
    elif mode == "swa":
        w0 = pl.multiple_of(jnp.clip(i * tq - SWA_WINDOW, 0, n_keys - SWA_KEYS), SWA_WINDOW)
        kl, vl = k_ref[pl.ds(w0, SWA_KEYS), :], v_ref[pl.ds(w0, SWA_KEYS), :]
        qpos = i * tq + lax.broadcasted_iota(jnp.int32, (tq, SWA_KEYS), 0)
        kpos = w0 + lax.broadcasted_iota(jnp.int32, (tq, SWA_KEYS), 1)
        valid = jnp.abs(kpos - qpos) <= SWA_WINDOW
    else:
        last = n_keys // GRID_W - NA_KEY_ROWS
        w0 = pl.multiple_of(jnp.clip(i * NA_TILE_ROWS - NA_WIN_ROWS // 2, 0, last) * GRID_W, GRID_W)
        kl, vl = k_ref[pl.ds(w0, NA_KEY_ROWS * GRID_W), :], v_ref[pl.ds(w0, NA_KEY_ROWS * GRID_W), :]
    if has_ctx:
        ck, cv = ck_ref[...], cv_ref[...]

    lane_k = lax.broadcasted_iota(jnp.int32, (1, wk), 1)
    lane_v = lax.broadcasted_iota(jnp.int32, (1, wv), 1)
    nt = (((1,), (1,)), ((), ()))
    for s in range(n_slabs):
        qs = q_ref[:, s * wk:(s + 1) * wk]
        acc = jnp.zeros((tq, wv), F32)
        for j in range(hs):
            head = j * n_slabs + s
            qm = jnp.where((lane_k >= j * dl) & (lane_k < (j + 1) * dl), qs, jnp.zeros_like(qs))
            s_loc = lax.dot_general(qm, kl, nt, preferred_element_type=F32)
            if mode == "na":
                s_loc = s_loc + bias_ref[head]
            if valid is not None:
                s_loc = jnp.where(valid, s_loc, NEG)
            m = jnp.max(s_loc, axis=-1, keepdims=True)
            if has_ctx:
                s_ctx = lax.dot_general(qm, ck, nt, preferred_element_type=F32)
                m = jnp.maximum(m, jnp.max(s_ctx, axis=-1, keepdims=True))
            if has_sink:
                sink = sink_ref[head] * LOG2E
                m = jnp.maximum(m, sink)
            p_loc = jnp.exp2(s_loc - m)
            den = jnp.sum(p_loc, axis=-1, keepdims=True)
            o = jnp.dot(p_loc.astype(BF16), vl, preferred_element_type=F32)
            if has_ctx:
                p_ctx = jnp.exp2(s_ctx - m)
                den = den + jnp.sum(p_ctx, axis=-1, keepdims=True)
                o = o + jnp.dot(p_ctx.astype(BF16), cv, preferred_element_type=F32)
            if has_sink:
                den = den + jnp.exp2(sink - m)
            out_lanes = (lane_v >= j * HEAD_DIM) & (lane_v < (j + 1) * HEAD_DIM)
            acc = acc + jnp.where(out_lanes, o * (1.0 / den), 0.0)
        o_ref[:, s * wv:(s + 1) * wv] = acc.astype(o_ref.dtype)


class _Mixer:
    def __init__(self, body, grid, in_specs, args, out_specs, out_shape, scratch):
        self.body, self.grid = body, grid
        self.in_specs, self.args = in_specs, args
        self.out_specs, self.out_shape, self.scratch = out_specs, out_shape, scratch


def _mix_kernel(*refs, bodies, n_in, n_out, n_scratch):
    ins, outs, scr = refs[:sum(n_in)], refs[sum(n_in):sum(n_in) + sum(n_out)], refs[sum(n_in) + sum(n_out):]
    i0 = o0 = s0 = 0
    for body, ni, no, ns in zip(bodies, n_in, n_out, n_scratch):
        body(*ins[i0:i0 + ni], *outs[o0:o0 + no], *scr[s0:s0 + ns])
        i0, o0, s0 = i0 + ni, o0 + no, s0 + ns


def _run_mixers(mixers, name):
    grid = mixers[0].grid
    assert all(m.grid == grid for m in mixers)
    return pl.pallas_call(
        functools.partial(_mix_kernel, bodies=[m.body for m in mixers], n_in=[len(m.args) for m in mixers],
                          n_out=[len(m.out_shape) for m in mixers], n_scratch=[len(m.scratch) for m in mixers]),
        grid=grid,
        in_specs=[s for m in mixers for s in m.in_specs],
        out_specs=[s for m in mixers for s in m.out_specs],
        out_shape=[s for m in mixers for s in m.out_shape],
        scratch_shapes=[s for m in mixers for s in m.scratch],
        compiler_params=_params(2), name=name,
    )(*[a for m in mixers for a in m.args])


def _attention(q, k, v, *, dl, mode="full", ck=None, cv=None, ctx_layer=None, bias=None, bias_layer=None,
               sink=None):
    b, n, wq = q.shape
    _, nk, wk = k.shape
    wv = v.shape[-1]
    n_slabs = wq // wk
    tq = min(Q_TILE, n)
    in_specs = [pl.BlockSpec((None, tq, wq), lambda bi, i: (bi, i, 0)),
                pl.BlockSpec((None, nk, wk), lambda bi, i: (bi, 0, 0)),
                pl.BlockSpec((None, nk, wv), lambda bi, i: (bi, 0, 0))]
    args = [q, k, v]
    if ck is not None:
        n_ctx = ck.shape[2]
        in_specs += [pl.BlockSpec((None, None, n_ctx, wk), lambda bi, i: (ctx_layer, bi, 0, 0)),
                     pl.BlockSpec((None, None, n_ctx, wv), lambda bi, i: (ctx_layer, bi, 0, 0))]
        args += [ck, cv]
    if mode == "na":
        n_tiles = n // tq

        def variant(bi, i):
            v_idx = jnp.where(i == 0, 0, jnp.where(i <= n_tiles - 3, 1, i - (n_tiles - 4)))
            return (bias_layer, v_idx, 0, 0, 0)

        in_specs.append(pl.BlockSpec((None, None) + tuple(bias.shape[2:]), variant))
        args.append(bias)
    if sink is not None:
        in_specs.append(pl.BlockSpec(memory_space=pltpu.SMEM))
        args.append(sink)
    body = functools.partial(_attn_kernel, tq=tq, wk=wk, wv=wv, dl=dl, n_slabs=n_slabs, mode=mode,
                             has_ctx=ck is not None, has_sink=sink is not None, n_keys=nk)
    return _Mixer(body, (b, n // tq), in_specs, args,
                  [pl.BlockSpec((None, tq, n_slabs * wv), lambda bi, i: (bi, i, 0))],
                  [jax.ShapeDtypeStruct((b, n, n_slabs * wv), BF16)], [])


def _mla_attn_kernel(*refs, tq, has_ctx):
    q_ref, k_ref, v_ref = refs[:3]
    pos = 3
    if has_ctx:
        ck_ref, cv_ref = refs[pos:pos + 2]
        pos += 2
    o_ref, s_ref = refs[pos], refs[pos + 1]
    blocks = [(k_ref, v_ref, off) for off in range(0, k_ref.shape[0], KEY_BLOCK)]
    if has_ctx:
        blocks += [(ck_ref, cv_ref, off) for off in range(0, ck_ref.shape[0], KEY_BLOCK)]
    nt = (((1,), (1,)), ((), ()))
    lane = lax.broadcasted_iota(jnp.int32, (1, MLA_PAD), 1)
    outs = []
    for hd in range(MLA_HEADS):
        cols = slice(hd * MLA_PAD, (hd + 1) * MLA_PAD)
        qh = q_ref[:, cols]
        m_part = jnp.full((tq, 128), NEG, F32)
        for bi, (kr, _, off) in enumerate(blocks):
            sb = lax.dot_general(qh, kr[off:off + KEY_BLOCK, cols], nt, preferred_element_type=F32)
            s_ref[:, bi * KEY_BLOCK:(bi + 1) * KEY_BLOCK] = sb
            m_part = jnp.maximum(m_part, jnp.maximum(sb[:, :128], sb[:, 128:]))
        m = jnp.max(m_part, axis=-1, keepdims=True)
        o = jnp.zeros((tq, MLA_PAD), F32)
        for bi, (_, vr, off) in enumerate(blocks):
            p = jnp.exp2(s_ref[:, bi * KEY_BLOCK:(bi + 1) * KEY_BLOCK] - m)
            o = o + jnp.dot(p.astype(BF16), vr[off:off + KEY_BLOCK, cols], preferred_element_type=F32)
        outs.append(o * pltpu.roll(1.0 / o, MLA_V, 1))
    for hd in range(0, MLA_HEADS, 2):
        pair = jnp.where(lane < MLA_V, outs[hd], pltpu.roll(outs[hd + 1], MLA_V, 1))
        o_ref[:, (hd // 2) * 128:(hd // 2 + 1) * 128] = pair.astype(o_ref.dtype)


def _mla_attention(q, k, v, *, ck=None, cv=None, ctx_layer=None):
    b, n, w = q.shape
    nk = k.shape[1]
    tq = min(Q_TILE, n)
    in_specs = [pl.BlockSpec((None, tq, w), lambda bi, i: (bi, i, 0)),
                pl.BlockSpec((None, nk, w), lambda bi, i: (bi, 0, 0)),
                pl.BlockSpec((None, nk, w), lambda bi, i: (bi, 0, 0))]
    args = [q, k, v]
    n_ctx = 0
    if ck is not None:
        n_ctx = ck.shape[2]
        in_specs += [pl.BlockSpec((None, None, n_ctx, w), lambda bi, i: (ctx_layer, bi, 0, 0)),
                     pl.BlockSpec((None, None, n_ctx, w), lambda bi, i: (ctx_layer, bi, 0, 0))]
        args += [ck, cv]
    return _Mixer(functools.partial(_mla_attn_kernel, tq=tq, has_ctx=ck is not None), (b, n // tq), in_specs, args,
                  [pl.BlockSpec((None, tq, MLA_HEADS * MLA_V), lambda bi, i: (bi, i, 0))],
                  [jax.ShapeDtypeStruct((b, n, MLA_HEADS * MLA_V), BF16)],
                  [pltpu.VMEM((tq, nk + n_ctx), F32)])


def _pool_kernel(x_ref, w_ref, s_ref, o_ref, *, n):
    x = x_ref[...]
    t = lax.broadcasted_iota(jnp.int32, x.shape, 0)
    lane = lax.broadcasted_iota(jnp.int32, x.shape, 1)

    def later(a, d):
        return jnp.where(t < n - d, pltpu.roll(a, n - d, 0), 0.0)

    def earlier(a, d):
        return jnp.where(t >= d, pltpu.roll(a, d, 0), 0.0)

    fwd, bwd = {1: x}, {1: x}
    for k in (1, 2, 4):
        fwd[2 * k] = fwd[k] + later(fwd[k], k)
        bwd[2 * k] = bwd[k] + earlier(bwd[k], k)
    pooled = jnp.zeros_like(x)
    for gi, w in enumerate(POOL_WINDOWS):
        total = fwd[w // 2] + earlier(bwd[w // 2], 1)
        count = (jnp.clip(t + w // 2, 0, n) - jnp.clip(t - w // 2, 0, n)).astype(F32)
        in_group = (lane >= gi * POOL_GW) & (lane < (gi + 1) * POOL_GW)
        pooled = jnp.where(in_group, total / count - x, pooled)
    y = jnp.dot(pooled.astype(BF16), w_ref[...], preferred_element_type=F32)
    o_ref[...] = (y * s_ref[...]).astype(o_ref.dtype)


def _pool(x, wts, l):
    b, n, w = x.shape
    return _Mixer(functools.partial(_pool_kernel, n=n), (b, 1),
                  [pl.BlockSpec((None, n, w), lambda bi, i: (bi, 0, 0)),
                   _layer_spec(wts["pool_w"], l, 2), _layer_spec(wts["pool_scale"], l, 2)],
                  [x, wts["pool_w"], wts["pool_scale"]],
                  [pl.BlockSpec((None, n, w), lambda bi, i: (bi, 0, 0))],
                  [jax.ShapeDtypeStruct((b, n, w), BF16)], [])


def _post_kernel(x_ref, p0, p1, p2, p3, mod_ref, n2_ref, wo_ref, w1_ref, w2_ref, o_ref):
    x = x_ref[...]
    mod = mod_ref[...]
    g1 = mod[:, 2 * D_MODEL:3 * D_MODEL]
    sh2, sc2, g2 = (mod[:, 3 * D_MODEL:4 * D_MODEL], mod[:, 4 * D_MODEL:5 * D_MODEL],
                    mod[:, 5 * D_MODEL:6 * D_MODEL])
    a = jnp.zeros(x.shape, F32)
    for i, p in enumerate((p0, p1, p2, p3)):
        a = a + jnp.dot(p[...], wo_ref[i * 256:(i + 1) * 256, :], preferred_element_type=F32)
    x = x + g1 * a
    h = (_rms_full(x, n2_ref[...]) * (1.0 + sc2) + sh2).astype(BF16)
    acc = jnp.zeros(x.shape, F32)
    for c in range(D_FF // FF_CHUNK):
        z = jnp.maximum(jnp.dot(h, w1_ref[:, c * FF_CHUNK:(c + 1) * FF_CHUNK], preferred_element_type=F32), 0.0)
        acc = acc + jnp.dot((z * z).astype(BF16), w2_ref[c * FF_CHUNK:(c + 1) * FF_CHUNK, :],
                            preferred_element_type=F32)
    o_ref[...] = x + g2 * acc


def _post(x, parts, mod, l, wts, *, tokens_per_mod, first_mod_row, name):
    t = x.shape[0]
    tm = TOKEN_TILE
    tiles_per_mod = tokens_per_mod // tm

    def tok(width):
        return pl.BlockSpec((tm, width), lambda i: (i, 0))

    def resident(arr):
        nd = arr.ndim
        return pl.BlockSpec((None,) + tuple(arr.shape[1:]), lambda i: (l,) + (0,) * (nd - 1),
                            pipeline_mode=pl.Buffered(1))

    return pl.pallas_call(
        _post_kernel,
        grid=(t // tm,),
        in_specs=[tok(D_MODEL), tok(256), tok(256), tok(256), tok(256),
                  pl.BlockSpec((None, None, 1, 6 * D_MODEL), lambda i: (l, first_mod_row + i // tiles_per_mod, 0, 0)),
                  _layer_spec(wts["norm2"], l, 1), resident(wts["w_out"]), resident(wts["w1"]), resident(wts["w2"])],
        out_specs=tok(D_MODEL),
        out_shape=jax.ShapeDtypeStruct((t, D_MODEL), F32),
        compiler_params=_params(1), name=name,
    )(x, *parts, mod, wts["norm2"], wts["w_out"], wts["w1"], wts["w2"])


def _block_ones(width, group):
    idx = np.arange(width) // group
    return jnp.asarray(idx[:, None] == idx[None, :], BF16)


def _rope_tables(n_tokens, head_lanes, first, rope_dims):
    half, quarter = rope_dims // 2, rope_dims // 4
    t = jnp.arange(n_tokens, dtype=jnp.int32)
    rows, cols = t // GRID_W, t % GRID_W
    freqs = ROPE_THETA ** (-jnp.arange(quarter, dtype=F32) / quarter)
    lane = np.arange(128) % head_lanes - first
    active = (lane >= 0) & (lane < rope_dims)
    lane = np.where(active, lane, 0)
    use_cols = lane >= half
    within = lane % half
    upper = within >= quarter
    f_idx = within % quarter
    pos = jnp.where(jnp.asarray(use_cols)[None, :], cols[:, None], rows[:, None]).astype(F32)
    ang = pos * freqs[jnp.asarray(f_idx)][None, :]
    act = jnp.asarray(active)[None, :]
    up = jnp.asarray(upper)[None, :]
    cos = jnp.where(act, jnp.cos(ang), 1.0)
    sin = jnp.where(act, jnp.sin(ang), 0.0)
    return cos, jnp.where(up, 0.0, -sin), jnp.where(up, sin, 0.0)


def _prepare(norm1, norm2, w_in, w_out, na_q_norm, na_k_norm, pool_w, pool_scale, mla_q_a_norm, mla_w_uq,
             mla_kv_norm, mla_w_ukv, mla_q_norm, mla_k_norm, swa_q_norm, swa_k_norm, mlp_w1, mlp_w2):
    d = DEPTH
    offs = np.cumsum((256, 256, 256, 256, MLA_Q_RANK, MLA_KV_RANK, MLA_ROPE, 256, 128, 128))[:-1]
    (w_naq, w_nak, w_nav, w_pool, w_cq, w_ckv, w_kpe, w_sq, w_sk, w_sv) = jnp.split(w_in, offs, axis=-1)
    w_sq = w_sq.reshape(d, D_MODEL, 2, 2, HEAD_DIM).swapaxes(2, 3).reshape(d, D_MODEL, 256)
    w_kpe_block = jnp.pad(w_kpe, ((0, 0), (0, 0), (MLA_NOPE, MLA_PAD - MLA_QK)))
    w_in_r = jnp.concatenate([w_naq, w_nak, w_nav, w_pool, w_cq, w_ckv, w_sq, w_sk, w_sv, w_kpe_block], axis=-1)

    def pad_heads(a, width):
        a = a.reshape(a.shape[:-1] + (MLA_HEADS, width))
        a = jnp.pad(a, [(0, 0)] * (a.ndim - 1) + [(0, MLA_PAD - width)])
        return a.reshape(a.shape[:-2] + (MLA_HEADS * MLA_PAD,))

    ukv = mla_w_ukv.reshape(d, MLA_KV_RANK, MLA_HEADS, MLA_NOPE + MLA_V)
    w_out_s = w_out[:, 768:1024].reshape(d, 2, 2, HEAD_DIM, D_MODEL).swapaxes(1, 2).reshape(d, 256, D_MODEL)
    eye = jnp.eye(POOL_GROUPS, dtype=pool_w.dtype)
    pool_bd = jnp.einsum("lgce,gh->lgche", pool_w, eye).reshape(d, 256, 256)

    def gain(g, reps, scale=1.0):
        return (jnp.tile(g, (1, reps)) * scale).reshape(d, 1, -1)

    return {
        "norm1": norm1.reshape(d, 1, D_MODEL), "norm2": norm2.reshape(d, 1, D_MODEL),
        "w_in": w_in_r.astype(BF16),
        "w_uq": pad_heads(mla_w_uq, MLA_QK).astype(BF16),
        "w_uk": pad_heads(ukv[..., :MLA_NOPE].reshape(d, MLA_KV_RANK, -1), MLA_NOPE).astype(BF16),
        "w_uv": pad_heads(ukv[..., MLA_NOPE:].reshape(d, MLA_KV_RANK, -1), MLA_V).astype(BF16),
        "w_out": jnp.concatenate([w_out[:, :768], w_out_s], axis=1).astype(BF16),
        "w1": mlp_w1.astype(BF16), "w2": mlp_w2.astype(BF16),
        "pool_w": pool_bd.astype(BF16), "pool_scale": pool_scale.reshape(d, 1, 256),
        "g_naq": gain(na_q_norm, NA_HEADS, HEAD_DIM ** -0.5 * LOG2E), "g_nak": gain(na_k_norm, NA_HEADS),
        "g_qa": mla_q_a_norm.reshape(d, 1, MLA_Q_RANK), "g_kv": mla_kv_norm.reshape(d, 1, MLA_KV_RANK),
        "g_mq": pad_heads(jnp.tile(mla_q_norm, (1, MLA_HEADS)) * (MLA_QK ** -0.5 * LOG2E), MLA_QK).reshape(d, 1, -1),
        "g_mk": pad_heads(jnp.tile(mla_k_norm, (1, MLA_HEADS)), MLA_QK).reshape(d, 1, -1),
        "g_sq": gain(swa_q_norm, SWA_HEADS, HEAD_DIM ** -0.5 * LOG2E), "g_sk": gain(swa_k_norm, SWA_KV_HEADS),
    }


def _constants():
    width = MLA_HEADS * MLA_PAD
    e = np.zeros((MLA_ROPE, width), np.float32)
    for h in range(MLA_HEADS):
        e[np.arange(MLA_ROPE), h * MLA_PAD + MLA_NOPE + np.arange(MLA_ROPE)] = 1.0
    v_ones = (np.arange(width) % MLA_PAD >= MLA_V).astype(np.float32)[None, :]
    return {"e_kpe": jnp.asarray(e, BF16), "v_ones": jnp.asarray(v_ones), "m256": _block_ones(256, HEAD_DIM),
            "m128": _block_ones(128, HEAD_DIM), "m512": _block_ones(width, MLA_PAD)}


def kernel(x_prompt, x_sample, cache_na_k, cache_na_v, cache_mla_ckv, cache_mla_kpe, cache_swa_k, cache_swa_v, c, c_ctx, norm1, norm2, w_ada, b_ada, w_in, w_out, na_q_norm, na_k_norm, na_rpb, pool_w, pool_scale, mla_q_a_norm, mla_w_uq, mla_kv_norm, mla_w_ukv, mla_q_norm, mla_k_norm, swa_q_norm, swa_k_norm, swa_sink, mlp_w1, mlp_w2):
    bc, nc, _ = x_prompt.shape
    bl, nl, _ = x_sample.shape
    n_past = cache_na_k.shape[2]
    wts = _prepare(norm1, norm2, w_in, w_out, na_q_norm, na_k_norm, pool_w, pool_scale, mla_q_a_norm, mla_w_uq,
                   mla_kv_norm, mla_w_ukv, mla_q_norm, mla_k_norm, swa_q_norm, swa_k_norm, mlp_w1, mlp_w2)
    consts = _constants()
    rope_tabs = _rope_tables(nl, MLA_PAD, MLA_NOPE, MLA_ROPE) + _rope_tables(nl, HEAD_DIM, 0, HEAD_DIM)

    cond = jnp.concatenate([c_ctx[None, :], c, jnp.zeros((8 - 1 - bl, D_MODEL), c.dtype)], axis=0)
    mod = _adaln(cond, w_ada, b_ada).reshape(DEPTH, 8, 1, 6 * D_MODEL)

    def cache(a):
        return a.reshape(bl, DEPTH, n_past, -1).swapaxes(0, 1).astype(BF16)

    c_nak, c_nav, c_sk, c_sv = cache(cache_na_k), cache(cache_na_v), cache(cache_swa_k), cache(cache_swa_v)
    c_mk, c_mv = _mla_cache(cache_mla_ckv, cache_mla_kpe, wts, consts)
    na_bias = _na_bias(na_rpb, nl // GRID_W)

    xp = x_prompt.reshape(bc * nc, D_MODEL)
    xs = x_sample.reshape(bl * nl, D_MODEL)
    states = [jnp.zeros((bc, DEPTH, nc, w), F32) for w in _STATE_WIDTHS]
    for l in range(DEPTH):
        sink = swa_sink[l]
        outs = _pre(xp, mod, l, wts, consts, None, tokens_per_mod=bc * nc, first_mod_row=0, state_seq=nc,
                    carried=states, name="pre_ctx")
        naq, nak, nav, pool_in, mq, mk, mv, sq, sk, sv = [o.reshape(bc, nc, -1) for o in outs[:10]]
        states = outs[10:]
        o_pool, o_mla, o_na, o_swa = _run_mixers(
            [_pool(pool_in, wts, l), _mla_attention(mq, mk, mv), _attention(naq, nak, nav, dl=HEAD_DIM),
             _attention(sq, sk, sv, dl=HEAD_DIM, sink=sink)], "ctx_mix")
        xp = _post(xp, [p.reshape(bc * nc, 256) for p in (o_na, o_pool, o_mla, o_swa)], mod, l, wts,
                   tokens_per_mod=bc * nc, first_mod_row=0, name="post_ctx")
        outs = _pre(xs, mod, l, wts, consts, rope_tabs, tokens_per_mod=nl, first_mod_row=1, name="pre_lat")
        naq, nak, nav, pool_in, mq, mk, mv, sq, sk, sv = [o.reshape(bl, nl, -1) for o in outs]
        o_mla, o_na, o_swa = _run_mixers(
            [_mla_attention(mq, mk, mv, ck=c_mk, cv=c_mv, ctx_layer=l),
             _attention(naq, nak, nav, dl=HEAD_DIM, mode="na", ck=c_nak, cv=c_nav, ctx_layer=l,
                        bias=na_bias, bias_layer=l),
             _attention(sq, sk, sv, dl=HEAD_DIM, mode="swa", ck=c_sk, cv=c_sv, ctx_layer=l, sink=sink)],
            "lat_mix")
        (o_pool,) = _run_mixers([_pool(pool_in, wts, l)], "lat_pool")
        xs = _post(xs, [p.reshape(bl * nl, 256) for p in (o_na, o_pool, o_mla, o_swa)], mod, l, wts,
                   tokens_per_mod=nl, first_mod_row=1, name="post_lat")

    st_nak, st_nav, st_ckv, st_kpe, st_sk, st_sv = states
    return (xp.reshape(bc, nc, D_MODEL), xs.reshape(bl, nl, D_MODEL),
            st_nak.reshape(bc, DEPTH, nc, NA_HEADS, HEAD_DIM), st_nav.reshape(bc, DEPTH, nc, NA_HEADS, HEAD_DIM),
            st_ckv, st_kpe,
            st_sk.reshape(bc, DEPTH, nc, SWA_KV_HEADS, HEAD_DIM), st_sv.reshape(bc, DEPTH, nc, SWA_KV_HEADS, HEAD_DIM))
```

```python
import functools

import numpy as np
import jax
import jax.numpy as jnp
from jax import lax
from jax.experimental import pallas as pl
from jax.experimental.pallas import tpu as pltpu

F32, BF16 = jnp.float32, jnp.bfloat16

D_MODEL = 1024
DEPTH = 4
GRID_W = 64
HEAD_DIM = 64
NA_HEADS = 4
NA_WIN_ROWS = 8
NA_WIN_COLS = 16
POOL_GROUPS = 4
POOL_GW = 64
POOL_WINDOWS = (2, 4, 8, 16)
MLA_HEADS = 4
MLA_Q_RANK = 256
MLA_KV_RANK = 128
MLA_NOPE = 64
MLA_ROPE = 32
MLA_V = 64
MLA_QK = MLA_NOPE + MLA_ROPE
MLA_PAD = 128
SWA_HEADS = 4
SWA_KV_HEADS = 2
SWA_WINDOW = 128
D_FF = 4 * D_MODEL
ROPE_THETA = 10000.0
EPS = 1e-6
NEG = -1e30
LOG2E = 1.4426950408889634

POST_TILE = 1024
POST_CHAIN = 256
PRE_TILE = 512
PRE_CHAIN = 128
Q_TILE = 256
KEY_BLOCK = 256
NA_TILE_ROWS = Q_TILE // GRID_W
NA_KEY_ROWS = NA_TILE_ROWS + NA_WIN_ROWS
SWA_KEYS = Q_TILE + 2 * SWA_WINDOW
ADA_COLS = 2048
FF_CHUNK = 1024
W_IN_COLS = 2048
VMEM_LIMIT = 48 * 1024 * 1024

_C_NA_Q, _C_NA_K, _C_NA_V, _C_POOL, _C_CQ, _C_CKV, _C_SWA_Q, _C_SWA_K, _C_SWA_V, _C_KPE_BLOCK = (
    0, 256, 512, 768, 1024, 1280, 1408, 1664, 1792, 1920)
_C_KPE = _C_KPE_BLOCK + MLA_NOPE


def _params(n_grid):
    return pltpu.CompilerParams(dimension_semantics=("arbitrary",) * n_grid, vmem_limit_bytes=VMEM_LIMIT)


def _layer_spec(arr, l, n_grid):
    nd = arr.ndim
    shape = (None,) + tuple(arr.shape[1:])
    if n_grid == 1:
        return pl.BlockSpec(shape, lambda i: (l,) + (0,) * (nd - 1))
    return pl.BlockSpec(shape, lambda b, i: (l,) + (0,) * (nd - 1))


def _const_spec(arr, n_grid):
    nd = arr.ndim
    if n_grid == 1:
        return pl.BlockSpec(arr.shape, lambda i: (0,) * nd)
    return pl.BlockSpec(arr.shape, lambda b, i: (0,) * nd)


def _rms_full(x, g):
    return x * lax.rsqrt(jnp.mean(x * x, axis=-1, keepdims=True) + EPS) * g


def _split_dot(s, m):
    hi = s.astype(BF16)
    lo = (s - hi.astype(F32)).astype(BF16)
    return jnp.dot(hi, m, preferred_element_type=F32) + jnp.dot(lo, m, preferred_element_type=F32)


def _rms_heads(x, g, m, dim):
    ss = jnp.dot((x * x).astype(BF16), m, preferred_element_type=F32)
    return x * lax.rsqrt(ss * (1.0 / dim) + EPS) * g


def _rms_lanes(x, g, dim):
    ss = jnp.sum(x * x, axis=-1, keepdims=True)
    return x * lax.rsqrt(ss * (1.0 / dim) + EPS) * g


def _rope(x, c_ref, sa_ref, sb_ref, quarter, rows):
    return (x * c_ref[rows, :] + pltpu.roll(x, 128 - quarter, 1) * sa_ref[rows, :]
            + pltpu.roll(x, quarter, 1) * sb_ref[rows, :])


def _adaln_kernel(cond_ref, w_ref, b_ref, o_ref):
    s = jax.nn.silu(cond_ref[...]).astype(BF16)
    o_ref[...] = jnp.dot(s, w_ref[...].astype(BF16), preferred_element_type=F32) + b_ref[...]


def _adaln(cond, w_ada, b_ada):
    n_out = w_ada.shape[-1]
    return pl.pallas_call(
        _adaln_kernel,
        grid=(DEPTH, n_out // ADA_COLS),
        in_specs=[pl.BlockSpec((8, D_MODEL), lambda l, j: (0, 0)),
                  pl.BlockSpec((None, D_MODEL, ADA_COLS), lambda l, j: (l, 0, j)),
                  pl.BlockSpec((None, 1, ADA_COLS), lambda l, j: (l, 0, j))],
        out_specs=pl.BlockSpec((None, 8, ADA_COLS), lambda l, j: (l, 0, j)),
        out_shape=jax.ShapeDtypeStruct((DEPTH, 8, n_out), F32),
        compiler_params=_params(2),
        name="adaln",
    )(cond, w_ada, b_ada.reshape(DEPTH, 1, n_out))


def _pre_kernel(*refs, rope, state_seq, n_carried, state_layer):
    (x_ref, mod_ref, n1_ref, win_ref, wuq_ref, wuk_ref, wuv_ref, ones_ref, m256_ref, m128_ref,
     g_naq, g_nak, g_qa, g_kv, g_mq, g_mk, g_sq, g_sk) = refs[:18]
    pos = 18
    if rope:
        cm, sam, sbm, cs, sas, sbs = refs[pos:pos + 6]
        pos += 6
    pos += n_carried
    (o_naq, o_nak, o_nav, o_pool, o_mq, o_mk, o_mv, o_sq, o_sk, o_sv) = refs[pos:pos + 10]
    pos += 10
    if state_seq:
        state_refs = refs[pos:pos + 6]

    mod = mod_ref[...]
    sh1 = mod[:, 0:D_MODEL]
    gain1 = n1_ref[...] * (1.0 + mod[:, D_MODEL:2 * D_MODEL])
    m256 = m256_ref[...]
    for r0 in range(0, x_ref.shape[0], PRE_CHAIN):
        rs = slice(r0, r0 + PRE_CHAIN)
        x = x_ref[rs, :]
        h = x * lax.rsqrt(jnp.mean(x * x, axis=-1, keepdims=True) + EPS) * gain1 + sh1
        y = jnp.dot(h.astype(BF16), win_ref[...], preferred_element_type=F32)

        na_k = _rms_heads(y[:, _C_NA_K:_C_NA_K + 256], g_nak[...], m256, HEAD_DIM)
        na_v = y[:, _C_NA_V:_C_NA_V + 256]
        o_naq[rs, :] = _rms_heads(y[:, _C_NA_Q:_C_NA_Q + 256], g_naq[...], m256, HEAD_DIM).astype(BF16)
        o_nak[rs, :] = na_k.astype(BF16)
        o_nav[rs, :] = na_v.astype(BF16)
        o_pool[rs, :] = y[:, _C_POOL:_C_POOL + 256]

        cq = _rms_full(y[:, _C_CQ:_C_CQ + MLA_Q_RANK], g_qa[...])
        mq = jnp.dot(cq.astype(BF16), wuq_ref[...], preferred_element_type=F32)
        ckv = _rms_full(y[:, _C_CKV:_C_CKV + MLA_KV_RANK], g_kv[...])
        ckv_b = ckv.astype(BF16)
        mk = jnp.dot(ckv_b, wuk_ref[...], preferred_element_type=F32)
        kpe_block = y[:, _C_KPE_BLOCK:_C_KPE_BLOCK + MLA_PAD]
        o_mv[rs, :] = (jnp.dot(ckv_b, wuv_ref[...], preferred_element_type=F32) + ones_ref[...]).astype(BF16)
        for hd in range(MLA_HEADS):
            cols = slice(hd * MLA_PAD, (hd + 1) * MLA_PAD)
            qh = _rms_lanes(mq[:, cols], g_mq[:, cols], MLA_QK)
            kh = _rms_lanes(mk[:, cols] + kpe_block, g_mk[:, cols], MLA_QK)
            if rope:
                qh = _rope(qh, cm, sam, sbm, MLA_ROPE // 4, rs)
                kh = _rope(kh, cm, sam, sbm, MLA_ROPE // 4, rs)
            o_mq[rs, cols] = qh.astype(BF16)
            o_mk[rs, cols] = kh.astype(BF16)

        sq = _rms_heads(y[:, _C_SWA_Q:_C_SWA_Q + 256], g_sq[...], m256, HEAD_DIM)
        sk = _rms_heads(y[:, _C_SWA_K:_C_SWA_K + 128], g_sk[...], m128_ref[...], HEAD_DIM)
        sv = y[:, _C_SWA_V:_C_SWA_V + 128]
        if state_seq:
            seq, s0 = r0 // state_seq, r0 % state_seq
            for ref, val in zip(state_refs, (na_k, na_v, ckv, y[:, _C_KPE:_C_KPE + MLA_ROPE], sk, sv)):
                if state_layer is None:
                    ref[seq, s0:s0 + PRE_CHAIN, :] = val
                else:
                    for k in range(DEPTH):
                        ref[seq, k, s0:s0 + PRE_CHAIN, :] = val if k == state_layer else jnp.zeros_like(val)
        if rope:
            for blk in range(2):
                cols = slice(blk * 128, (blk + 1) * 128)
                o_sq[rs, cols] = _rope(sq[:, cols], cs, sas, sbs, HEAD_DIM // 4, rs).astype(BF16)
            sk = _rope(sk, cs, sas, sbs, HEAD_DIM // 4, rs)
        else:
            o_sq[rs, :] = sq.astype(BF16)
        o_sk[rs, :] = sk.astype(BF16)
        o_sv[rs, :] = sv.astype(BF16)


_STATE_WIDTHS = (NA_HEADS * HEAD_DIM, NA_HEADS * HEAD_DIM, MLA_KV_RANK, MLA_ROPE,
                 SWA_KV_HEADS * HEAD_DIM, SWA_KV_HEADS * HEAD_DIM)


def _pre(x, mod, l, wts, consts, rope_tabs, *, tokens_per_mod, first_mod_row, name, state_seq=0, carried=None):
    t = x.shape[0]
    tm = PRE_TILE
    rope = rope_tabs is not None
    tiles_per_mod = tokens_per_mod // tm
    n_pos = rope_tabs[0].shape[0] // tm if rope else 1

    def row(width, dtype):
        return jax.ShapeDtypeStruct((t, width), dtype)

    def tok(width):
        return pl.BlockSpec((tm, width), lambda i: (i, 0))

    in_specs = [tok(D_MODEL),
                pl.BlockSpec((None, None, 1, 6 * D_MODEL), lambda i: (l, first_mod_row + i // tiles_per_mod, 0, 0)),
                _layer_spec(wts["norm1"], l, 1), _layer_spec(wts["w_in"], l, 1), _layer_spec(wts["w_uq"], l, 1),
                _layer_spec(wts["w_uk"], l, 1), _layer_spec(wts["w_uv"], l, 1),
                _const_spec(consts["v_ones"], 1), _const_spec(consts["m256"], 1), _const_spec(consts["m128"], 1)]
    args = [x, mod, wts["norm1"], wts["w_in"], wts["w_uq"], wts["w_uk"], wts["w_uv"],
            consts["v_ones"], consts["m256"], consts["m128"]]
    for nm in ("g_naq", "g_nak", "g_qa", "g_kv", "g_mq", "g_mk", "g_sq", "g_sk"):
        in_specs.append(_layer_spec(wts[nm], l, 1))
        args.append(wts[nm])
    if rope:
        for tab in rope_tabs:
            in_specs.append(pl.BlockSpec((tm, tab.shape[1]), lambda i: (i % n_pos, 0)))
            args.append(tab)
    widths = (256, 256, 256, 256, 512, 512, 512, 256, 128, 128)
    dtypes = (BF16, BF16, BF16, F32, BF16, BF16, BF16, BF16, BF16, BF16)
    out_shape = [row(w, d) for w, d in zip(widths, dtypes)]
    out_specs = [tok(w) for w in widths]
    aliases = {}
    n_carried = 0
    if state_seq:
        seqs_per_tile = tm // state_seq
        if carried is not None:
            n_carried = len(carried)
            for k, buf in enumerate(carried):
                aliases[len(args)] = len(out_shape) + k
                in_specs.append(pl.BlockSpec(memory_space=pl.ANY))
                args.append(buf)
        for w in _STATE_WIDTHS:
            out_shape.append(jax.ShapeDtypeStruct((t // state_seq, DEPTH, state_seq, w), F32))
            if carried is None:
                out_specs.append(pl.BlockSpec((seqs_per_tile, DEPTH, state_seq, w), lambda i: (i, 0, 0, 0)))
            else:
                out_specs.append(pl.BlockSpec((seqs_per_tile, None, state_seq, w), lambda i: (i, l, 0, 0)))
    return pl.pallas_call(
        functools.partial(_pre_kernel, rope=rope, state_seq=state_seq, n_carried=n_carried,
                          state_layer=l if state_seq and carried is None else None),
        grid=(t // tm,), in_specs=in_specs, out_specs=out_specs, out_shape=out_shape,
        input_output_aliases=aliases, compiler_params=_params(1), name=name,
    )(*args)


def _mla_cache_kernel(ckv_ref, kpe_ref, wuk_ref, wuv_ref, e_ref, ones_ref, m512_ref, g_mk, ok_ref, ov_ref):
    ckv_b = ckv_ref[...].astype(BF16)
    mk = jnp.dot(ckv_b, wuk_ref[...], preferred_element_type=F32) + _split_dot(kpe_ref[...], e_ref[...])
    ok_ref[...] = _rms_heads(mk, g_mk[...], m512_ref[...], MLA_QK).astype(BF16)
    ov_ref[...] = (jnp.dot(ckv_b, wuv_ref[...], preferred_element_type=F32) + ones_ref[...]).astype(BF16)


def _mla_cache(c_ckv, c_kpe, wts, consts):
    b, _, n, _ = c_ckv.shape
    width = MLA_HEADS * MLA_PAD

    def lay(arr):
        nd = arr.ndim
        return pl.BlockSpec((None,) + tuple(arr.shape[1:]), lambda l, i: (l,) + (0,) * (nd - 1))

    return pl.pallas_call(
        _mla_cache_kernel,
        grid=(DEPTH, b),
        in_specs=[pl.BlockSpec((None, None, n, MLA_KV_RANK), lambda l, i: (i, l, 0, 0)),
                  pl.BlockSpec((None, None, n, MLA_ROPE), lambda l, i: (i, l, 0, 0)),
                  lay(wts["w_uk"]), lay(wts["w_uv"]), _const_spec(consts["e_kpe"], 2),
                  _const_spec(consts["v_ones"], 2), _const_spec(consts["m512"], 2), lay(wts["g_mk"])],
        out_specs=[pl.BlockSpec((None, None, n, width), lambda l, i: (i, l, 0, 0)),
                   pl.BlockSpec((None, None, n, width), lambda l, i: (i, l, 0, 0))],
        out_shape=[jax.ShapeDtypeStruct((b, DEPTH, n, width), BF16),
                   jax.ShapeDtypeStruct((b, DEPTH, n, width), BF16)],
        compiler_params=_params(2), name="mla_cache",
    )(c_ckv, c_kpe, wts["w_uk"], wts["w_uv"], consts["e_kpe"], consts["v_ones"], consts["m512"], wts["g_mk"])


_NA_VARIANTS = ((0, 0), (4, 0), (24, 20), (28, 20))


def _na_bias_kernel(rpb_ref, o_ref, *, grid_rows):
    l, hd = pl.program_id(0), pl.program_id(1)
    n_dr, n_dc = 2 * NA_WIN_ROWS - 1, 2 * NA_WIN_COLS - 1
    base = (l * NA_HEADS + hd) * (n_dr * n_dc)
    c = lax.broadcasted_iota(jnp.int32, (GRID_W, GRID_W), 0)
    kc = lax.broadcasted_iota(jnp.int32, (GRID_W, GRID_W), 1)
    dcv = kc - c + (NA_WIN_COLS - 1)
    col_start = jnp.clip(c - NA_WIN_COLS // 2, 0, GRID_W - NA_WIN_COLS)
    col_ok = (kc >= col_start) & (kc < col_start + NA_WIN_COLS)
    neg = jnp.full((GRID_W, GRID_W), NEG, F32)
    blocks = {}

    def block(dr):
        if dr not in blocks:
            t = jnp.zeros((GRID_W, GRID_W), F32)
            for dc in range(n_dc):
                t = jnp.where(dcv == dc, rpb_ref[base + dr * n_dc + dc] * LOG2E, t)
            blocks[dr] = jnp.where(col_ok, t, NEG)
        return blocks[dr]

    for v, (r0, w0) in enumerate(_NA_VARIANTS):
        for rl in range(NA_TILE_ROWS):
            r = r0 + rl
            kr0 = min(max(r - NA_WIN_ROWS // 2, 0), grid_rows - NA_WIN_ROWS)
            for j in range(NA_KEY_ROWS):
                kr = w0 + j
                inside = kr0 <= kr < kr0 + NA_WIN_ROWS
                o_ref[v, rl * GRID_W:(rl + 1) * GRID_W, j * GRID_W:(j + 1) * GRID_W] = (
                    block(kr - r + NA_WIN_ROWS - 1) if inside else neg)


def _na_bias(na_rpb, grid_rows):
    n_k = NA_KEY_ROWS * GRID_W
    return pl.pallas_call(
        functools.partial(_na_bias_kernel, grid_rows=grid_rows),
        grid=(DEPTH, NA_HEADS),
        in_specs=[pl.BlockSpec(memory_space=pltpu.SMEM)],
        out_specs=pl.BlockSpec((None, len(_NA_VARIANTS), None, Q_TILE, n_k), lambda l, h: (l, 0, h, 0, 0)),
        out_shape=jax.ShapeDtypeStruct((DEPTH, len(_NA_VARIANTS), NA_HEADS, Q_TILE, n_k), F32),
        compiler_params=_params(2), name="na_bias",
    )(na_rpb.reshape(-1))


def _attn_kernel(*refs, tq, wk, wv, dl, n_slabs, mode, has_ctx, has_sink, n_keys):
    q_ref, k_ref, v_ref = refs[:3]
    pos = 3
    if has_ctx:
        ck_ref, cv_ref = refs[pos:pos + 2]
        pos += 2
    if mode == "na":
        bias_ref = refs[pos]
        pos += 1
    if has_sink:
        sink_ref = refs[pos]
        pos += 1
    o_ref = refs[pos]
    i = pl.program_id(1)
    hs = wk // dl

    valid = None
    if mode == "full":
        kl, vl = k_ref[...], v_ref[...]
    elif mode == "swa":
        w0 = pl.multiple_of(jnp.clip(i * tq - SWA_WINDOW, 0, n_keys - SWA_KEYS), SWA_WINDOW)
        kl, vl = k_ref[pl.ds(w0, SWA_KEYS), :], v_ref[pl.ds(w0, SWA_KEYS), :]
        qpos = i * tq + lax.broadcasted_iota(jnp.int32, (tq, SWA_KEYS), 0)
        kpos = w0 + lax.broadcasted_iota(jnp.int32, (tq, SWA_KEYS), 1)
        valid = jnp.abs(kpos - qpos) <= SWA_WINDOW
    else:
        last = n_keys // GRID_W - NA_KEY_ROWS
        w0 = pl.multiple_of(jnp.clip(i * NA_TILE_ROWS - NA_WIN_ROWS // 2, 0, last) * GRID_W, GRID_W)
        kl, vl = k_ref[pl.ds(w0, NA_KEY_ROWS * GRID_W), :], v_ref[pl.ds(w0, NA_KEY_ROWS * GRID_W), :]
    if has_ctx:
        ck, cv = ck_ref[...], cv_ref[...]

    lane_k = lax.broadcasted_iota(jnp.int32, (1, wk), 1)
    lane_v = lax.broadcasted_iota(jnp.int32, (1, wv), 1)
    nt = (((1,), (1,)), ((), ()))
    for s in range(n_slabs):
        qs = q_ref[:, s * wk:(s + 1) * wk]
        acc = jnp.zeros((tq, wv), F32)
        for j in range(hs):
            head = j * n_slabs + s
            qm = jnp.where((lane_k >= j * dl) & (lane_k < (j + 1) * dl), qs, jnp.zeros_like(qs))
            s_loc = lax.dot_general(qm, kl, nt, preferred_element_type=F32)
            if mode == "na":
                s_loc = s_loc + bias_ref[head]
            if valid is not None:
                s_loc = jnp.where(valid, s_loc, NEG)
            m = jnp.max(s_loc, axis=-1, keepdims=True)
            if has_ctx:
                s_ctx = lax.dot_general(qm, ck, nt, preferred_element_type=F32)
                m = jnp.maximum(m, jnp.max(s_ctx, axis=-1, keepdims=True))
            if has_sink:
                sink = sink_ref[head] * LOG2E
                m = jnp.maximum(m, sink)
            p_loc = jnp.exp2(s_loc - m)
            den = jnp.sum(p_loc, axis=-1, keepdims=True)
            o = jnp.dot(p_loc.astype(BF16), vl, preferred_element_type=F32)
            if has_ctx:
                p_ctx = jnp.exp2(s_ctx - m)
                den = den + jnp.sum(p_ctx, axis=-1, keepdims=True)
                o = o + jnp.dot(p_ctx.astype(BF16), cv, preferred_element_type=F32)
            if has_sink:
                den = den + jnp.exp2(sink - m)
            out_lanes = (lane_v >= j * HEAD_DIM) & (lane_v < (j + 1) * HEAD_DIM)
            acc = acc + jnp.where(out_lanes, o * (1.0 / den), 0.0)
        o_ref[:, s * wv:(s + 1) * wv] = acc.astype(o_ref.dtype)


class _Mixer:
    def __init__(self, body, grid, in_specs, args, out_specs, out_shape, scratch):
        self.body, self.grid = body, grid
        self.in_specs, self.args = in_specs, args
        self.out_specs, self.out_shape, self.scratch = out_specs, out_shape, scratch


def _mix_kernel(*refs, bodies, n_in, n_out, n_scratch):
    ins, outs, scr = refs[:sum(n_in)], refs[sum(n_in):sum(n_in) + sum(n_out)], refs[sum(n_in) + sum(n_out):]
    i0 = o0 = s0 = 0
    for body, ni, no, ns in zip(bodies, n_in, n_out, n_scratch):
        body(*ins[i0:i0 + ni], *outs[o0:o0 + no], *scr[s0:s0 + ns])
        i0, o0, s0 = i0 + ni, o0 + no, s0 + ns


def _run_mixers(mixers, name):
    grid = mixers[0].grid
    assert all(m.grid == grid for m in mixers)
    return pl.pallas_call(
        functools.partial(_mix_kernel, bodies=[m.body for m in mixers], n_in=[len(m.args) for m in mixers],
                          n_out=[len(m.out_shape) for m in mixers], n_scratch=[len(m.scratch) for m in mixers]),
        grid=grid,
        in_specs=[s for m in mixers for s in m.in_specs],
        out_specs=[s for m in mixers for s in m.out_specs],
        out_shape=[s for m in mixers for s in m.out_shape],
        scratch_shapes=[s for m in mixers for s in m.scratch],
        compiler_params=_params(2), name=name,
    )(*[a for m in mixers for a in m.args])


def _attention(q, k, v, *, dl, mode="full", ck=None, cv=None, ctx_layer=None, bias=None, bias_layer=None,
               sink=None):
    b, n, wq = q.shape
    _, nk, wk = k.shape
    wv = v.shape[-1]
    n_slabs = wq // wk
    tq = min(Q_TILE, n)
    in_specs = [pl.BlockSpec((None, tq, wq), lambda bi, i: (bi, i, 0)),
                pl.BlockSpec((None, nk, wk), lambda bi, i: (bi, 0, 0)),
                pl.BlockSpec((None, nk, wv), lambda bi, i: (bi, 0, 0))]
    args = [q, k, v]
    if ck is not None:
        n_ctx = ck.shape[2]
        in_specs += [pl.BlockSpec((None, None, n_ctx, wk), lambda bi, i: (bi, ctx_layer, 0, 0)),
                     pl.BlockSpec((None, None, n_ctx, wv), lambda bi, i: (bi, ctx_layer, 0, 0))]
        args += [ck, cv]
    if mode == "na":
        n_tiles = n // tq

        def variant(bi, i):
            v_idx = jnp.where(i == 0, 0, jnp.where(i <= n_tiles - 3, 1, i - (n_tiles - 4)))
            return (bias_layer, v_idx, 0, 0, 0)

        in_specs.append(pl.BlockSpec((None, None) + tuple(bias.shape[2:]), variant))
        args.append(bias)
    if sink is not None:
        in_specs.append(pl.BlockSpec(memory_space=pltpu.SMEM))
        args.append(sink)
    body = functools.partial(_attn_kernel, tq=tq, wk=wk, wv=wv, dl=dl, n_slabs=n_slabs, mode=mode,
                             has_ctx=ck is not None, has_sink=sink is not None, n_keys=nk)
    return _Mixer(body, (b, n // tq), in_specs, args,
                  [pl.BlockSpec((None, tq, n_slabs * wv), lambda bi, i: (bi, i, 0))],
                  [jax.ShapeDtypeStruct((b, n, n_slabs * wv), BF16)], [])


def _mla_attn_kernel(*refs, tq, has_ctx):
    q_ref, k_ref, v_ref = refs[:3]
    pos = 3
    if has_ctx:
        ck_ref, cv_ref = refs[pos:pos + 2]
        pos += 2
    o_ref, s_ref = refs[pos], refs[pos + 1]
    blocks = [(k_ref, v_ref, off) for off in range(0, k_ref.shape[0], KEY_BLOCK)]
    if has_ctx:
        blocks += [(ck_ref, cv_ref, off) for off in range(0, ck_ref.shape[0], KEY_BLOCK)]
    nt = (((1,), (1,)), ((), ()))
    lane = lax.broadcasted_iota(jnp.int32, (1, MLA_PAD), 1)
    outs = []
    for hd in range(MLA_HEADS):
        cols = slice(hd * MLA_PAD, (hd + 1) * MLA_PAD)
        qh = q_ref[:, cols]
        m_part = jnp.full((tq, 128), NEG, F32)
        for bi, (kr, _, off) in enumerate(blocks):
            sb = lax.dot_general(qh, kr[off:off + KEY_BLOCK, cols], nt, preferred_element_type=F32)
            s_ref[:, bi * KEY_BLOCK:(bi + 1) * KEY_BLOCK] = sb
            m_part = jnp.maximum(m_part, jnp.maximum(sb[:, :128], sb[:, 128:]))
        m = jnp.max(m_part, axis=-1, keepdims=True)
        o = jnp.zeros((tq, MLA_PAD), F32)
        for bi, (_, vr, off) in enumerate(blocks):
            p = jnp.exp2(s_ref[:, bi * KEY_BLOCK:(bi + 1) * KEY_BLOCK] - m)
            o = o + jnp.dot(p.astype(BF16), vr[off:off + KEY_BLOCK, cols], preferred_element_type=F32)
        outs.append(o * pltpu.roll(1.0 / o, MLA_V, 1))
    for hd in range(0, MLA_HEADS, 2):
        pair = jnp.where(lane < MLA_V, outs[hd], pltpu.roll(outs[hd + 1], MLA_V, 1))
        o_ref[:, (hd // 2) * 128:(hd // 2 + 1) * 128] = pair.astype(o_ref.dtype)


def _mla_attention(q, k, v, *, ck=None, cv=None, ctx_layer=None):
    b, n, w = q.shape
    nk = k.shape[1]
    tq = min(Q_TILE, n)
    in_specs = [pl.BlockSpec((None, tq, w), lambda bi, i: (bi, i, 0)),
                pl.BlockSpec((None, nk, w), lambda bi, i: (bi, 0, 0)),
                pl.BlockSpec((None, nk, w), lambda bi, i: (bi, 0, 0))]
    args = [q, k, v]
    n_ctx = 0
    if ck is not None:
        n_ctx = ck.shape[2]
        in_specs += [pl.BlockSpec((None, None, n_ctx, w), lambda bi, i: (bi, ctx_layer, 0, 0)),
                     pl.BlockSpec((None, None, n_ctx, w), lambda bi, i: (bi, ctx_layer, 0, 0))]
        args += [ck, cv]
    return _Mixer(functools.partial(_mla_attn_kernel, tq=tq, has_ctx=ck is not None), (b, n // tq), in_specs, args,
                  [pl.BlockSpec((None, tq, MLA_HEADS * MLA_V), lambda bi, i: (bi, i, 0))],
                  [jax.ShapeDtypeStruct((b, n, MLA_HEADS * MLA_V), BF16)],
                  [pltpu.VMEM((tq, nk + n_ctx), F32)])


def _pool_kernel(x_ref, w_ref, s_ref, o_ref, *, n):
    x = x_ref[...]
    t = lax.broadcasted_iota(jnp.int32, x.shape, 0)
    lane = lax.broadcasted_iota(jnp.int32, x.shape, 1)

    def later(a, d):
        return jnp.where(t < n - d, pltpu.roll(a, n - d, 0), 0.0)

    def earlier(a, d):
        return jnp.where(t >= d, pltpu.roll(a, d, 0), 0.0)

    fwd, bwd = {1: x}, {1: x}
    for k in (1, 2, 4):
        fwd[2 * k] = fwd[k] + later(fwd[k], k)
        bwd[2 * k] = bwd[k] + earlier(bwd[k], k)
    pooled = jnp.zeros_like(x)
    for gi, w in enumerate(POOL_WINDOWS):
        total = fwd[w // 2] + earlier(bwd[w // 2], 1)
        count = (jnp.clip(t + w // 2, 0, n) - jnp.clip(t - w // 2, 0, n)).astype(F32)
        in_group = (lane >= gi * POOL_GW) & (lane < (gi + 1) * POOL_GW)
        pooled = jnp.where(in_group, total / count - x, pooled)
    y = jnp.dot(pooled.astype(BF16), w_ref[...], preferred_element_type=F32)
    o_ref[...] = (y * s_ref[...]).astype(o_ref.dtype)


def _pool(x, wts, l):
    b, n, w = x.shape
    return _Mixer(functools.partial(_pool_kernel, n=n), (b, 1),
                  [pl.BlockSpec((None, n, w), lambda bi, i: (bi, 0, 0)),
                   _layer_spec(wts["pool_w"], l, 2), _layer_spec(wts["pool_scale"], l, 2)],
                  [x, wts["pool_w"], wts["pool_scale"]],
                  [pl.BlockSpec((None, n, w), lambda bi, i: (bi, 0, 0))],
                  [jax.ShapeDtypeStruct((b, n, w), BF16)], [])


def _post_kernel(x_ref, p0, p1, p2, p3, mod_ref, n2_ref, wo_ref, w1_ref, w2_ref, o_ref):
    mod = mod_ref[...]
    g1 = mod[:, 2 * D_MODEL:3 * D_MODEL]
    sh2, sc2, g2 = (mod[:, 3 * D_MODEL:4 * D_MODEL], mod[:, 4 * D_MODEL:5 * D_MODEL],
                    mod[:, 5 * D_MODEL:6 * D_MODEL])
    for r0 in range(0, x_ref.shape[0], POST_CHAIN):
        rs = slice(r0, r0 + POST_CHAIN)
        x = x_ref[rs, :]
        a = jnp.zeros(x.shape, F32)
        for i, p in enumerate((p0, p1, p2, p3)):
            a = a + jnp.dot(p[rs, :], wo_ref[i * 256:(i + 1) * 256, :], preferred_element_type=F32)
        x = x + g1 * a
        h = (_rms_full(x, n2_ref[...]) * (1.0 + sc2) + sh2).astype(BF16)
        acc = jnp.zeros(x.shape, F32)
        for c in range(D_FF // FF_CHUNK):
            z = jnp.maximum(jnp.dot(h, w1_ref[:, c * FF_CHUNK:(c + 1) * FF_CHUNK], preferred_element_type=F32), 0.0)
            acc = acc + jnp.dot((z * z).astype(BF16), w2_ref[c * FF_CHUNK:(c + 1) * FF_CHUNK, :],
                                preferred_element_type=F32)
        o_ref[rs, :] = x + g2 * acc


def _post(x, parts, mod, l, wts, *, tokens_per_mod, first_mod_row, name):
    t = x.shape[0]
    tm = POST_TILE
    tiles_per_mod = tokens_per_mod // tm

    def tok(width):
        return pl.BlockSpec((tm, width), lambda i: (i, 0))

    def resident(arr):
        nd = arr.ndim
        return pl.BlockSpec((None,) + tuple(arr.shape[1:]), lambda i: (l,) + (0,) * (nd - 1),
                            pipeline_mode=pl.Buffered(1))

    return pl.pallas_call(
        _post_kernel,
        grid=(t // tm,),
        in_specs=[tok(D_MODEL), tok(256), tok(256), tok(256), tok(256),
                  pl.BlockSpec((None, None, 1, 6 * D_MODEL), lambda i: (l, first_mod_row + i // tiles_per_mod, 0, 0)),
                  _layer_spec(wts["norm2"], l, 1), resident(wts["w_out"]), resident(wts["w1"]), resident(wts["w2"])],
        out_specs=tok(D_MODEL),
        out_shape=jax.ShapeDtypeStruct((t, D_MODEL), F32),
        compiler_params=_params(1), name=name,
    )(x, *parts, mod, wts["norm2"], wts["w_out"], wts["w1"], wts["w2"])


def _block_ones(width, group):
    idx = np.arange(width) // group
    return jnp.asarray(idx[:, None] == idx[None, :], BF16)


def _rope_tables(n_tokens, head_lanes, first, rope_dims):
    half, quarter = rope_dims // 2, rope_dims // 4
    t = jnp.arange(n_tokens, dtype=jnp.int32)
    rows, cols = t // GRID_W, t % GRID_W
    freqs = ROPE_THETA ** (-jnp.arange(quarter, dtype=F32) / quarter)
    lane = np.arange(128) % head_lanes - first
    active = (lane >= 0) & (lane < rope_dims)
    lane = np.where(active, lane, 0)
    use_cols = lane >= half
    within = lane % half
    upper = within >= quarter
    f_idx = within % quarter
    pos = jnp.where(jnp.asarray(use_cols)[None, :], cols[:, None], rows[:, None]).astype(F32)
    ang = pos * freqs[jnp.asarray(f_idx)][None, :]
    act = jnp.asarray(active)[None, :]
    up = jnp.asarray(upper)[None, :]
    cos = jnp.where(act, jnp.cos(ang), 1.0)
    sin = jnp.where(act, jnp.sin(ang), 0.0)
    return cos, jnp.where(up, 0.0, -sin), jnp.where(up, sin, 0.0)


_SWA_Q_ORDER = (0, 2, 1, 3)
_W_IN_ROWS = 256


def _regroup_w_in_kernel(w_ref, o_ref):
    src_kpe = _C_CKV + MLA_KV_RANK
    src_sq = src_kpe + MLA_ROPE

    def put(dst, src, width):
        o_ref[:, dst:dst + width] = w_ref[:, src:src + width].astype(BF16)

    put(0, 0, src_kpe)
    for slot, head in enumerate(_SWA_Q_ORDER):
        put(_C_SWA_Q + slot * HEAD_DIM, src_sq + head * HEAD_DIM, HEAD_DIM)
    put(_C_SWA_K, src_sq + 256, 128)
    put(_C_SWA_V, src_sq + 384, 128)
    o_ref[:, _C_KPE_BLOCK:_C_KPE_BLOCK + MLA_PAD] = jnp.zeros((o_ref.shape[0], MLA_PAD), BF16)
    put(_C_KPE, src_kpe, MLA_ROPE)


def _regroup_w_in(w_in):
    d, rows, cols = w_in.shape
    return pl.pallas_call(
        _regroup_w_in_kernel, grid=(d, rows // _W_IN_ROWS),
        in_specs=[pl.BlockSpec((None, _W_IN_ROWS, cols), lambda l, i: (l, i, 0))],
        out_specs=pl.BlockSpec((None, _W_IN_ROWS, W_IN_COLS), lambda l, i: (l, i, 0)),
        out_shape=jax.ShapeDtypeStruct((d, rows, W_IN_COLS), BF16),
        compiler_params=_params(2), name="regroup_w_in",
    )(w_in)


def _regroup_w_out_kernel(w_ref, o_ref):
    first = 3 * 256
    o_ref[0:first, :] = w_ref[0:first, :].astype(BF16)
    for slot, head in enumerate(_SWA_Q_ORDER):
        o_ref[first + slot * HEAD_DIM:first + (slot + 1) * HEAD_DIM, :] = (
            w_ref[first + head * HEAD_DIM:first + (head + 1) * HEAD_DIM, :].astype(BF16))


def _regroup_w_out(w_out):
    d, rows, cols = w_out.shape
    return pl.pallas_call(
        _regroup_w_out_kernel, grid=(d, 1),
        in_specs=[pl.BlockSpec((None, rows, cols), lambda l, i: (l, 0, 0))],
        out_specs=pl.BlockSpec((None, rows, cols), lambda l, i: (l, 0, 0)),
        out_shape=jax.ShapeDtypeStruct((d, rows, cols), BF16),
        compiler_params=_params(2), name="regroup_w_out",
    )(w_out)


def _prepare(norm1, norm2, w_in, w_out, na_q_norm, na_k_norm, pool_w, pool_scale, mla_q_a_norm, mla_w_uq,
             mla_kv_norm, mla_w_ukv, mla_q_norm, mla_k_norm, swa_q_norm, swa_k_norm, mlp_w1, mlp_w2):
    d = DEPTH

    def pad_heads(a, width):
        a = a.reshape(a.shape[:-1] + (MLA_HEADS, width))
        a = jnp.pad(a, [(0, 0)] * (a.ndim - 1) + [(0, MLA_PAD - width)])
        return a.reshape(a.shape[:-2] + (MLA_HEADS * MLA_PAD,))

    ukv = mla_w_ukv.reshape(d, MLA_KV_RANK, MLA_HEADS, MLA_NOPE + MLA_V)
    eye = jnp.eye(POOL_GROUPS, dtype=pool_w.dtype)
    pool_bd = jnp.einsum("lgce,gh->lgche", pool_w, eye).reshape(d, 256, 256)

    def gain(g, reps, scale=1.0):
        return (jnp.tile(g, (1, reps)) * scale).reshape(d, 1, -1)

    return {
        "norm1": norm1.reshape(d, 1, D_MODEL), "norm2": norm2.reshape(d, 1, D_MODEL),
        "w_in": _regroup_w_in(w_in),
        "w_uq": pad_heads(mla_w_uq, MLA_QK).astype(BF16),
        "w_uk": pad_heads(ukv[..., :MLA_NOPE].reshape(d, MLA_KV_RANK, -1), MLA_NOPE).astype(BF16),
        "w_uv": pad_heads(ukv[..., MLA_NOPE:].reshape(d, MLA_KV_RANK, -1), MLA_V).astype(BF16),
        "w_out": _regroup_w_out(w_out),
        "w1": mlp_w1.astype(BF16), "w2": mlp_w2.astype(BF16),
        "pool_w": pool_bd.astype(BF16), "pool_scale": pool_scale.reshape(d, 1, 256),
        "g_naq": gain(na_q_norm, NA_HEADS, HEAD_DIM ** -0.5 * LOG2E), "g_nak": gain(na_k_norm, NA_HEADS),
        "g_qa": mla_q_a_norm.reshape(d, 1, MLA_Q_RANK), "g_kv": mla_kv_norm.reshape(d, 1, MLA_KV_RANK),
        "g_mq": pad_heads(jnp.tile(mla_q_norm, (1, MLA_HEADS)) * (MLA_QK ** -0.5 * LOG2E), MLA_QK).reshape(d, 1, -1),
        "g_mk": pad_heads(jnp.tile(mla_k_norm, (1, MLA_HEADS)), MLA_QK).reshape(d, 1, -1),
        "g_sq": gain(swa_q_norm, SWA_HEADS, HEAD_DIM ** -0.5 * LOG2E), "g_sk": gain(swa_k_norm, SWA_KV_HEADS),
    }


def _constants():
    width = MLA_HEADS * MLA_PAD
    e = np.zeros((MLA_ROPE, width), np.float32)
    for h in range(MLA_HEADS):
        e[np.arange(MLA_ROPE), h * MLA_PAD + MLA_NOPE + np.arange(MLA_ROPE)] = 1.0
    v_ones = (np.arange(width) % MLA_PAD >= MLA_V).astype(np.float32)[None, :]
    return {"e_kpe": jnp.asarray(e, BF16), "v_ones": jnp.asarray(v_ones), "m256": _block_ones(256, HEAD_DIM),
            "m128": _block_ones(128, HEAD_DIM), "m512": _block_ones(width, MLA_PAD)}


def kernel(x_prompt, x_sample, cache_na_k, cache_na_v, cache_mla_ckv, cache_mla_kpe, cache_swa_k, cache_swa_v, c, c_ctx, norm1, norm2, w_ada, b_ada, w_in, w_out, na_q_norm, na_k_norm, na_rpb, pool_w, pool_scale, mla_q_a_norm, mla_w_uq, mla_kv_norm, mla_w_ukv, mla_q_norm, mla_k_norm, swa_q_norm, swa_k_norm, swa_sink, mlp_w1, mlp_w2):
    bc, nc, _ = x_prompt.shape
    bl, nl, _ = x_sample.shape
    n_past = cache_na_k.shape[2]
    wts = _prepare(norm1, norm2, w_in, w_out, na_q_norm, na_k_norm, pool_w, pool_scale, mla_q_a_norm, mla_w_uq,
                   mla_kv_norm, mla_w_ukv, mla_q_norm, mla_k_norm, swa_q_norm, swa_k_norm, mlp_w1, mlp_w2)
    consts = _constants()
    rope_tabs = _rope_tables(nl, MLA_PAD, MLA_NOPE, MLA_ROPE) + _rope_tables(nl, HEAD_DIM, 0, HEAD_DIM)

    cond = jnp.concatenate([c_ctx[None, :], c, jnp.zeros((8 - 1 - bl, D_MODEL), c.dtype)], axis=0)
    mod = _adaln(cond, w_ada, b_ada).reshape(DEPTH, 8, 1, 6 * D_MODEL)

    def cache(a):
        return a.reshape(bl, DEPTH, n_past, -1).astype(BF16)

    c_nak, c_nav, c_sk, c_sv = cache(cache_na_k), cache(cache_na_v), cache(cache_swa_k), cache(cache_swa_v)
    c_mk, c_mv = _mla_cache(cache_mla_ckv, cache_mla_kpe, wts, consts)
    na_bias = _na_bias(na_rpb, nl // GRID_W)

    xp = x_prompt.reshape(bc * nc, D_MODEL)
    xs = x_sample.reshape(bl * nl, D_MODEL)
    states = None
    for l in range(DEPTH):
        sink = swa_sink[l]
        outs = _pre(xp, mod, l, wts, consts, None, tokens_per_mod=bc * nc, first_mod_row=0, state_seq=nc,
                    carried=states, name="pre_ctx")
        naq, nak, nav, pool_in, mq, mk, mv, sq, sk, sv = [o.reshape(bc, nc, -1) for o in outs[:10]]
        states = outs[10:]
        o_pool, o_mla, o_na, o_swa = _run_mixers(
            [_pool(pool_in, wts, l), _mla_attention(mq, mk, mv), _attention(naq, nak, nav, dl=HEAD_DIM),
             _attention(sq, sk, sv, dl=HEAD_DIM, sink=sink)], "ctx_mix")
        xp = _post(xp, [p.reshape(bc * nc, 256) for p in (o_na, o_pool, o_mla, o_swa)], mod, l, wts,
                   tokens_per_mod=bc * nc, first_mod_row=0, name="post_ctx")
        outs = _pre(xs, mod, l, wts, consts, rope_tabs, tokens_per_mod=nl, first_mod_row=1, name="pre_lat")
        naq, nak, nav, pool_in, mq, mk, mv, sq, sk, sv = [o.reshape(bl, nl, -1) for o in outs]
        o_mla, o_na, o_swa = _run_mixers(
            [_mla_attention(mq, mk, mv, ck=c_mk, cv=c_mv, ctx_layer=l),
             _attention(naq, nak, nav, dl=HEAD_DIM, mode="na", ck=c_nak, cv=c_nav, ctx_layer=l,
                        bias=na_bias, bias_layer=l),
             _attention(sq, sk, sv, dl=HEAD_DIM, mode="swa", ck=c_sk, cv=c_sv, ctx_layer=l, sink=sink)],
            "lat_mix")
        (o_pool,) = _run_mixers([_pool(pool_in, wts, l)], "lat_pool")
        xs = _post(xs, [p.reshape(bl * nl, 256) for p in (o_na, o_pool, o_mla, o_swa)], mod, l, wts,
                   tokens_per_mod=nl, first_mod_row=1, name="post_lat")

    st_nak, st_nav, st_ckv, st_kpe, st_sk, st_sv = states
    return (xp.reshape(bc, nc, D_MODEL), xs.reshape(bl, nl, D_MODEL),
            st_nak.reshape(bc, DEPTH, nc, NA_HEADS, HEAD_DIM), st_nav.reshape(bc, DEPTH, nc, NA_HEADS, HEAD_DIM),
            st_ckv, st_kpe,
            st_sk.reshape(bc, DEPTH, nc, SWA_KV_HEADS, HEAD_DIM), st_sv.reshape(bc, DEPTH, nc, SWA_KV_HEADS, HEAD_DIM))
```

```python
import functools

import numpy as np
import jax
import jax.numpy as jnp
from jax import lax
from jax.experimental import pallas as pl
from jax.experimental.pallas import tpu as pltpu

F32, BF16 = jnp.float32, jnp.bfloat16

D_MODEL = 1024
DEPTH = 4
GRID_W = 64
HEAD_DIM = 64
NA_HEADS = 4
NA_WIN_ROWS = 8
NA_WIN_COLS = 16
POOL_GROUPS = 4
POOL_GW = 64
POOL_WINDOWS = (2, 4, 8, 16)
MLA_HEADS = 4
MLA_Q_RANK = 256
MLA_KV_RANK = 128
MLA_NOPE = 64
MLA_ROPE = 32
MLA_V = 64
MLA_QK = MLA_NOPE + MLA_ROPE
MLA_PAD = 128
SWA_HEADS = 4
SWA_KV_HEADS = 2
SWA_WINDOW = 128
D_FF = 4 * D_MODEL
ROPE_THETA = 10000.0
EPS = 1e-6
NEG = -1e30
LOG2E = 1.4426950408889634

POST_TILE = 1024
POST_CHAIN = 256
PRE_TILE = 512
PRE_CHAIN = 128
Q_TILE = 256
KEY_BLOCK = 256
NA_TILE_ROWS = Q_TILE // GRID_W
NA_KEY_ROWS = NA_TILE_ROWS + NA_WIN_ROWS
SWA_KEYS = Q_TILE + 2 * SWA_WINDOW
ADA_COLS = 2048
FF_CHUNK = 1024
W_IN_COLS = 2048
VMEM_LIMIT = 48 * 1024 * 1024

_C_NA_Q, _C_NA_K, _C_NA_V, _C_POOL, _C_CQ, _C_CKV, _C_SWA_Q, _C_SWA_K, _C_SWA_V, _C_KPE_BLOCK = (
    0, 256, 512, 768, 1024, 1280, 1408, 1664, 1792, 1920)
_C_KPE = _C_KPE_BLOCK + MLA_NOPE


def _params(n_grid):
    return pltpu.CompilerParams(dimension_semantics=("arbitrary",) * n_grid, vmem_limit_bytes=VMEM_LIMIT)


def _layer_spec(arr, l, n_grid):
    nd = arr.ndim
    shape = (None,) + tuple(arr.shape[1:])
    if n_grid == 1:
        return pl.BlockSpec(shape, lambda i: (l,) + (0,) * (nd - 1))
    return pl.BlockSpec(shape, lambda b, i: (l,) + (0,) * (nd - 1))


def _const_spec(arr, n_grid):
    nd = arr.ndim
    if n_grid == 1:
        return pl.BlockSpec(arr.shape, lambda i: (0,) * nd)
    return pl.BlockSpec(arr.shape, lambda b, i: (0,) * nd)


def _rms_full(x, g):
    return x * lax.rsqrt(jnp.mean(x * x, axis=-1, keepdims=True) + EPS) * g


def _split_dot(s, m):
    hi = s.astype(BF16)
    lo = (s - hi.astype(F32)).astype(BF16)
    return jnp.dot(hi, m, preferred_element_type=F32) + jnp.dot(lo, m, preferred_element_type=F32)


def _rms_heads(x, g, m, dim):
    ss = jnp.dot((x * x).astype(BF16), m, preferred_element_type=F32)
    return x * lax.rsqrt(ss * (1.0 / dim) + EPS) * g


def _rms_lanes(x, g, dim):
    ss = jnp.sum(x * x, axis=-1, keepdims=True)
    return x * lax.rsqrt(ss * (1.0 / dim) + EPS) * g


def _rope(x, c_ref, sa_ref, sb_ref, quarter, rows):
    return (x * c_ref[rows, :] + pltpu.roll(x, 128 - quarter, 1) * sa_ref[rows, :]
            + pltpu.roll(x, quarter, 1) * sb_ref[rows, :])


def _adaln_kernel(cond_ref, w_ref, b_ref, o_ref):
    s = jax.nn.silu(cond_ref[...]).astype(BF16)
    o_ref[...] = jnp.dot(s, w_ref[...].astype(BF16), preferred_element_type=F32) + b_ref[...]


def _adaln(cond, w_ada, b_ada):
    n_out = w_ada.shape[-1]
    return pl.pallas_call(
        _adaln_kernel,
        grid=(DEPTH, n_out // ADA_COLS),
        in_specs=[pl.BlockSpec((8, D_MODEL), lambda l, j: (0, 0)),
                  pl.BlockSpec((None, D_MODEL, ADA_COLS), lambda l, j: (l, 0, j)),
                  pl.BlockSpec((None, 1, ADA_COLS), lambda l, j: (l, 0, j))],
        out_specs=pl.BlockSpec((None, 8, ADA_COLS), lambda l, j: (l, 0, j)),
        out_shape=jax.ShapeDtypeStruct((DEPTH, 8, n_out), F32),
        compiler_params=_params(2),
        name="adaln",
    )(cond, w_ada, b_ada.reshape(DEPTH, 1, n_out))


def _pre_kernel(*refs, rope, state_seq, n_carried, state_layer):
    (x_ref, mod_ref, n1_ref, win_ref, wuq_ref, wuk_ref, wuv_ref, ones_ref, m256_ref, m128_ref,
     g_naq, g_nak, g_qa, g_kv, g_mq, g_mk, g_sq, g_sk) = refs[:18]
    pos = 18
    if rope:
        cm, sam, sbm, cs, sas, sbs = refs[pos:pos + 6]
        pos += 6
    pos += n_carried
    (o_naq, o_nak, o_nav, o_pool, o_mq, o_mk, o_mv, o_sq, o_sk, o_sv) = refs[pos:pos + 10]
    pos += 10
    if state_seq:
        state_refs = refs[pos:pos + 6]

    mod = mod_ref[...]
    sh1 = mod[:, 0:D_MODEL]
    gain1 = n1_ref[...] * (1.0 + mod[:, D_MODEL:2 * D_MODEL])
    m256 = m256_ref[...]
    for r0 in range(0, x_ref.shape[0], PRE_CHAIN):
        rs = slice(r0, r0 + PRE_CHAIN)
        x = x_ref[rs, :]
        h = x * lax.rsqrt(jnp.mean(x * x, axis=-1, keepdims=True) + EPS) * gain1 + sh1
        y = jnp.dot(h.astype(BF16), win_ref[...], preferred_element_type=F32)

        na_k = _rms_heads(y[:, _C_NA_K:_C_NA_K + 256], g_nak[...], m256, HEAD_DIM)
        na_v = y[:, _C_NA_V:_C_NA_V + 256]
        o_naq[rs, :] = _rms_heads(y[:, _C_NA_Q:_C_NA_Q + 256], g_naq[...], m256, HEAD_DIM).astype(BF16)
        o_nak[rs, :] = na_k.astype(BF16)
        o_nav[rs, :] = na_v.astype(BF16)
        o_pool[rs, :] = y[:, _C_POOL:_C_POOL + 256]

        cq = _rms_full(y[:, _C_CQ:_C_CQ + MLA_Q_RANK], g_qa[...])
        mq = jnp.dot(cq.astype(BF16), wuq_ref[...], preferred_element_type=F32)
        ckv = _rms_full(y[:, _C_CKV:_C_CKV + MLA_KV_RANK], g_kv[...])
        ckv_b = ckv.astype(BF16)
        mk = jnp.dot(ckv_b, wuk_ref[...], preferred_element_type=F32)
        kpe_block = y[:, _C_KPE_BLOCK:_C_KPE_BLOCK + MLA_PAD]
        o_mv[rs, :] = (jnp.dot(ckv_b, wuv_ref[...], preferred_element_type=F32) + ones_ref[...]).astype(BF16)
        for hd in range(MLA_HEADS):
            cols = slice(hd * MLA_PAD, (hd + 1) * MLA_PAD)
            qh = _rms_lanes(mq[:, cols], g_mq[:, cols], MLA_QK)
            kh = _rms_lanes(mk[:, cols] + kpe_block, g_mk[:, cols], MLA_QK)
            if rope:
                qh = _rope(qh, cm, sam, sbm, MLA_ROPE // 4, rs)
                kh = _rope(kh, cm, sam, sbm, MLA_ROPE // 4, rs)
            o_mq[rs, cols] = qh.astype(BF16)
            o_mk[rs, cols] = kh.astype(BF16)

        sq = _rms_heads(y[:, _C_SWA_Q:_C_SWA_Q + 256], g_sq[...], m256, HEAD_DIM)
        sk = _rms_heads(y[:, _C_SWA_K:_C_SWA_K + 128], g_sk[...], m128_ref[...], HEAD_DIM)
        sv = y[:, _C_SWA_V:_C_SWA_V + 128]
        if state_seq:
            seq, s0 = r0 // state_seq, r0 % state_seq
            for ref, val in zip(state_refs, (na_k, na_v, ckv, y[:, _C_KPE:_C_KPE + MLA_ROPE], sk, sv)):
                if state_layer is None:
                    ref[seq, s0:s0 + PRE_CHAIN, :] = val
                else:
                    for k in range(DEPTH):
                        ref[seq, k, s0:s0 + PRE_CHAIN, :] = val if k == state_layer else jnp.zeros_like(val)
        if rope:
            for blk in range(2):
                cols = slice(blk * 128, (blk + 1) * 128)
                o_sq[rs, cols] = _rope(sq[:, cols], cs, sas, sbs, HEAD_DIM // 4, rs).astype(BF16)
            sk = _rope(sk, cs, sas, sbs, HEAD_DIM // 4, rs)
        else:
            o_sq[rs, :] = sq.astype(BF16)
        o_sk[rs, :] = sk.astype(BF16)
        o_sv[rs, :] = sv.astype(BF16)


_STATE_WIDTHS = (NA_HEADS * HEAD_DIM, NA_HEADS * HEAD_DIM, MLA_KV_RANK, MLA_ROPE,
                 SWA_KV_HEADS * HEAD_DIM, SWA_KV_HEADS * HEAD_DIM)


def _pre(x, mod, l, wts, consts, rope_tabs, *, tokens_per_mod, first_mod_row, name, state_seq=0, carried=None):
    t = x.shape[0]
    tm = PRE_TILE
    rope = rope_tabs is not None
    tiles_per_mod = tokens_per_mod // tm
    n_pos = rope_tabs[0].shape[0] // tm if rope else 1

    def row(width, dtype):
        return jax.ShapeDtypeStruct((t, width), dtype)

    def tok(width):
        return pl.BlockSpec((tm, width), lambda i: (i, 0))

    in_specs = [tok(D_MODEL),
                pl.BlockSpec((None, None, 1, 6 * D_MODEL), lambda i: (l, first_mod_row + i // tiles_per_mod, 0, 0)),
                _layer_spec(wts["norm1"], l, 1), _layer_spec(wts["w_in"], l, 1), _layer_spec(wts["w_uq"], l, 1),
                _layer_spec(wts["w_uk"], l, 1), _layer_spec(wts["w_uv"], l, 1),
                _const_spec(consts["v_ones"], 1), _const_spec(consts["m256"], 1), _const_spec(consts["m128"], 1)]
    args = [x, mod, wts["norm1"], wts["w_in"], wts["w_uq"], wts["w_uk"], wts["w_uv"],
            consts["v_ones"], consts["m256"], consts["m128"]]
    for nm in ("g_naq", "g_nak", "g_qa", "g_kv", "g_mq", "g_mk", "g_sq", "g_sk"):
        in_specs.append(_layer_spec(wts[nm], l, 1))
        args.append(wts[nm])
    if rope:
        for tab in rope_tabs:
            in_specs.append(pl.BlockSpec((tm, tab.shape[1]), lambda i: (i % n_pos, 0)))
            args.append(tab)
    widths = (256, 256, 256, 256, 512, 512, 512, 256, 128, 128)
    dtypes = (BF16, BF16, BF16, F32, BF16, BF16, BF16, BF16, BF16, BF16)
    out_shape = [row(w, d) for w, d in zip(widths, dtypes)]
    out_specs = [tok(w) for w in widths]
    aliases = {}
    n_carried = 0
    if state_seq:
        seqs_per_tile = tm // state_seq
        if carried is not None:
            n_carried = len(carried)
            for k, buf in enumerate(carried):
                aliases[len(args)] = len(out_shape) + k
                in_specs.append(pl.BlockSpec(memory_space=pl.ANY))
                args.append(buf)
        for w in _STATE_WIDTHS:
            out_shape.append(jax.ShapeDtypeStruct((t // state_seq, DEPTH, state_seq, w), F32))
            if carried is None:
                out_specs.append(pl.BlockSpec((seqs_per_tile, DEPTH, state_seq, w), lambda i: (i, 0, 0, 0)))
            else:
                out_specs.append(pl.BlockSpec((seqs_per_tile, None, state_seq, w), lambda i: (i, l, 0, 0)))
    return pl.pallas_call(
        functools.partial(_pre_kernel, rope=rope, state_seq=state_seq, n_carried=n_carried,
                          state_layer=l if state_seq and carried is None else None),
        grid=(t // tm,), in_specs=in_specs, out_specs=out_specs, out_shape=out_shape,
        input_output_aliases=aliases, compiler_params=_params(1), name=name,
    )(*args)


def _mla_cache_kernel(ckv_ref, kpe_ref, wuk_ref, wuv_ref, e_ref, ones_ref, m512_ref, g_mk, ok_ref, ov_ref):
    ckv_b = ckv_ref[...].astype(BF16)
    mk = jnp.dot(ckv_b, wuk_ref[...], preferred_element_type=F32) + _split_dot(kpe_ref[...], e_ref[...])
    ok_ref[...] = _rms_heads(mk, g_mk[...], m512_ref[...], MLA_QK).astype(BF16)
    ov_ref[...] = (jnp.dot(ckv_b, wuv_ref[...], preferred_element_type=F32) + ones_ref[...]).astype(BF16)


def _mla_cache(c_ckv, c_kpe, wts, consts):
    b, _, n, _ = c_ckv.shape
    width = MLA_HEADS * MLA_PAD

    def lay(arr):
        nd = arr.ndim
        return pl.BlockSpec((None,) + tuple(arr.shape[1:]), lambda l, i: (l,) + (0,) * (nd - 1))

    return pl.pallas_call(
        _mla_cache_kernel,
        grid=(DEPTH, b),
        in_specs=[pl.BlockSpec((None, None, n, MLA_KV_RANK), lambda l, i: (i, l, 0, 0)),
                  pl.BlockSpec((None, None, n, MLA_ROPE), lambda l, i: (i, l, 0, 0)),
                  lay(wts["w_uk"]), lay(wts["w_uv"]), _const_spec(consts["e_kpe"], 2),
                  _const_spec(consts["v_ones"], 2), _const_spec(consts["m512"], 2), lay(wts["g_mk"])],
        out_specs=[pl.BlockSpec((None, None, n, width), lambda l, i: (i, l, 0, 0)),
                   pl.BlockSpec((None, None, n, width), lambda l, i: (i, l, 0, 0))],
        out_shape=[jax.ShapeDtypeStruct((b, DEPTH, n, width), BF16),
                   jax.ShapeDtypeStruct((b, DEPTH, n, width), BF16)],
        compiler_params=_params(2), name="mla_cache",
    )(c_ckv, c_kpe, wts["w_uk"], wts["w_uv"], consts["e_kpe"], consts["v_ones"], consts["m512"], wts["g_mk"])


_NA_VARIANTS = ((0, 0), (4, 0), (24, 20), (28, 20))


def _na_bias_kernel(rpb_ref, o_ref, *, grid_rows):
    l, hd = pl.program_id(0), pl.program_id(1)
    n_dr, n_dc = 2 * NA_WIN_ROWS - 1, 2 * NA_WIN_COLS - 1
    base = (l * NA_HEADS + hd) * (n_dr * n_dc)
    c = lax.broadcasted_iota(jnp.int32, (GRID_W, GRID_W), 0)
    kc = lax.broadcasted_iota(jnp.int32, (GRID_W, GRID_W), 1)
    dcv = kc - c + (NA_WIN_COLS - 1)
    col_start = jnp.clip(c - NA_WIN_COLS // 2, 0, GRID_W - NA_WIN_COLS)
    col_ok = (kc >= col_start) & (kc < col_start + NA_WIN_COLS)
    neg = jnp.full((GRID_W, GRID_W), NEG, F32)
    blocks = {}

    def block(dr):
        if dr not in blocks:
            t = jnp.zeros((GRID_W, GRID_W), F32)
            for dc in range(n_dc):
                t = jnp.where(dcv == dc, rpb_ref[base + dr * n_dc + dc] * LOG2E, t)
            blocks[dr] = jnp.where(col_ok, t, NEG)
        return blocks[dr]

    for v, (r0, w0) in enumerate(_NA_VARIANTS):
        for rl in range(NA_TILE_ROWS):
            r = r0 + rl
            kr0 = min(max(r - NA_WIN_ROWS // 2, 0), grid_rows - NA_WIN_ROWS)
            for j in range(NA_KEY_ROWS):
                kr = w0 + j
                inside = kr0 <= kr < kr0 + NA_WIN_ROWS
                o_ref[v, rl * GRID_W:(rl + 1) * GRID_W, j * GRID_W:(j + 1) * GRID_W] = (
                    block(kr - r + NA_WIN_ROWS - 1) if inside else neg)


def _na_bias(na_rpb, grid_rows):
    n_k = NA_KEY_ROWS * GRID_W
    return pl.pallas_call(
        functools.partial(_na_bias_kernel, grid_rows=grid_rows),
        grid=(DEPTH, NA_HEADS),
        in_specs=[pl.BlockSpec(memory_space=pltpu.SMEM)],
        out_specs=pl.BlockSpec((None, len(_NA_VARIANTS), None, Q_TILE, n_k), lambda l, h: (l, 0, h, 0, 0)),
        out_shape=jax.ShapeDtypeStruct((DEPTH, len(_NA_VARIANTS), NA_HEADS, Q_TILE, n_k), F32),
        compiler_params=_params(2), name="na_bias",
    )(na_rpb.reshape(-1))


def _attn_kernel(*refs, tq, wk, wv, dl, n_slabs, mode, has_ctx, has_sink, n_keys):
    q_ref, k_ref, v_ref = refs[:3]
    pos = 3
    if has_ctx:
        ck_ref, cv_ref = refs[pos:pos + 2]
        pos += 2
    if mode == "na":
        bias_ref = refs[pos]
        pos += 1
    if has_sink:
        sink_ref = refs[pos]
        pos += 1
    o_ref = refs[pos]
    i = pl.program_id(1)
    hs = wk // dl

    valid = None
    if mode == "full":
        kl, vl = k_ref[...], v_ref[...]
    elif mode == "swa":
        w0 = pl.multiple_of(jnp.clip(i * tq - SWA_WINDOW, 0, n_keys - SWA_KEYS), SWA_WINDOW)
        kl, vl = k_ref[pl.ds(w0, SWA_KEYS), :], v_ref[pl.ds(w0, SWA_KEYS), :]
        qpos = i * tq + lax.broadcasted_iota(jnp.int32, (tq, SWA_KEYS), 0)
        kpos = w0 + lax.broadcasted_iota(jnp.int32, (tq, SWA_KEYS), 1)
        valid = jnp.abs(kpos - qpos) <= SWA_WINDOW
    else:
        last = n_keys // GRID_W - NA_KEY_ROWS
        w0 = pl.multiple_of(jnp.clip(i * NA_TILE_ROWS - NA_WIN_ROWS // 2, 0, last) * GRID_W, GRID_W)
        kl, vl = k_ref[pl.ds(w0, NA_KEY_ROWS * GRID_W), :], v_ref[pl.ds(w0, NA_KEY_ROWS * GRID_W), :]
    if has_ctx:
        ck, cv = ck_ref[...], cv_ref[...]

    lane_k = lax.broadcasted_iota(jnp.int32, (1, wk), 1)
    lane_v = lax.broadcasted_iota(jnp.int32, (1, wv), 1)
    nt = (((1,), (1,)), ((), ()))
    for s in range(n_slabs):
        qs = q_ref[:, s * wk:(s + 1) * wk]
        acc = jnp.zeros((tq, wv), F32)
        for j in range(hs):
            head = j * n_slabs + s
            qm = jnp.where((lane_k >= j * dl) & (lane_k < (j + 1) * dl), qs, jnp.zeros_like(qs))
            s_loc = lax.dot_general(qm, kl, nt, preferred_element_type=F32)
            if mode == "na":
                s_loc = s_loc + bias_ref[head]
            if valid is not None:
                s_loc = jnp.where(valid, s_loc, NEG)
            m = jnp.max(s_loc, axis=-1, keepdims=True)
            if has_ctx:
                s_ctx = lax.dot_general(qm, ck, nt, preferred_element_type=F32)
                m = jnp.maximum(m, jnp.max(s_ctx, axis=-1, keepdims=True))
            if has_sink:
                sink = sink_ref[head] * LOG2E
                m = jnp.maximum(m, sink)
            p_loc = jnp.exp2(s_loc - m)
            den = jnp.sum(p_loc, axis=-1, keepdims=True)
            o = jnp.dot(p_loc.astype(BF16), vl, preferred_element_type=F32)
            if has_ctx:
                p_ctx = jnp.exp2(s_ctx - m)
                den = den + jnp.sum(p_ctx, axis=-1, keepdims=True)
                o = o + jnp.dot(p_ctx.astype(BF16), cv, preferred_element_type=F32)
            if has_sink:
                den = den + jnp.exp2(sink - m)
            out_lanes = (lane_v >= j * HEAD_DIM) & (lane_v < (j + 1) * HEAD_DIM)
            acc = acc + jnp.where(out_lanes, o * (1.0 / den), 0.0)
        o_ref[:, s * wv:(s + 1) * wv] = acc.astype(o_ref.dtype)


class _Mixer:
    def __init__(self, body, grid, in_specs, args, out_specs, out_shape, scratch):
        self.body, self.grid = body, grid
        self.in_specs, self.args = in_specs, args
        self.out_specs, self.out_shape, self.scratch = out_specs, out_shape, scratch


def _mix_kernel(*refs, bodies, n_in, n_out, n_scratch):
    ins, outs, scr = refs[:sum(n_in)], refs[sum(n_in):sum(n_in) + sum(n_out)], refs[sum(n_in) + sum(n_out):]
    i0 = o0 = s0 = 0
    for body, ni, no, ns in zip(bodies, n_in, n_out, n_scratch):
        body(*ins[i0:i0 + ni], *outs[o0:o0 + no], *scr[s0:s0 + ns])
        i0, o0, s0 = i0 + ni, o0 + no, s0 + ns


def _run_mixers(mixers, name):
    grid = mixers[0].grid
    assert all(m.grid == grid for m in mixers)
    return pl.pallas_call(
        functools.partial(_mix_kernel, bodies=[m.body for m in mixers], n_in=[len(m.args) for m in mixers],
                          n_out=[len(m.out_shape) for m in mixers], n_scratch=[len(m.scratch) for m in mixers]),
        grid=grid,
        in_specs=[s for m in mixers for s in m.in_specs],
        out_specs=[s for m in mixers for s in m.out_specs],
        out_shape=[s for m in mixers for s in m.out_shape],
        scratch_shapes=[s for m in mixers for s in m.scratch],
        compiler_params=_params(2), name=name,
    )(*[a for m in mixers for a in m.args])


def _attention(q, k, v, *, dl, mode="full", ck=None, cv=None, ctx_layer=None, bias=None, bias_layer=None,
               sink=None):
    b, n, wq = q.shape
    _, nk, wk = k.shape
    wv = v.shape[-1]
    n_slabs = wq // wk
    tq = min(Q_TILE, n)
    in_specs = [pl.BlockSpec((None, tq, wq), lambda bi, i: (bi, i, 0)),
                pl.BlockSpec((None, nk, wk), lambda bi, i: (bi, 0, 0)),
                pl.BlockSpec((None, nk, wv), lambda bi, i: (bi, 0, 0))]
    args = [q, k, v]
    if ck is not None:
        n_ctx = ck.shape[2]
        in_specs += [pl.BlockSpec((None, None, n_ctx, wk), lambda bi, i: (bi, ctx_layer, 0, 0)),
                     pl.BlockSpec((None, None, n_ctx, wv), lambda bi, i: (bi, ctx_layer, 0, 0))]
        args += [ck, cv]
    if mode == "na":
        n_tiles = n // tq

        def variant(bi, i):
            v_idx = jnp.where(i == 0, 0, jnp.where(i <= n_tiles - 3, 1, i - (n_tiles - 4)))
            return (bias_layer, v_idx, 0, 0, 0)

        in_specs.append(pl.BlockSpec((None, None) + tuple(bias.shape[2:]), variant))
        args.append(bias)
    if sink is not None:
        in_specs.append(pl.BlockSpec(memory_space=pltpu.SMEM))
        args.append(sink)
    body = functools.partial(_attn_kernel, tq=tq, wk=wk, wv=wv, dl=dl, n_slabs=n_slabs, mode=mode,
                             has_ctx=ck is not None, has_sink=sink is not None, n_keys=nk)
    return _Mixer(body, (b, n // tq), in_specs, args,
                  [pl.BlockSpec((None, tq, n_slabs * wv), lambda bi, i: (bi, i, 0))],
                  [jax.ShapeDtypeStruct((b, n, n_slabs * wv), BF16)], [])


def _mla_attn_kernel(*refs, tq, has_ctx):
    q_ref, k_ref, v_ref = refs[:3]
    pos = 3
    if has_ctx:
        ck_ref, cv_ref = refs[pos:pos + 2]
        pos += 2
    o_ref, s_ref = refs[pos], refs[pos + 1]
    blocks = [(k_ref, v_ref, off) for off in range(0, k_ref.shape[0], KEY_BLOCK)]
    if has_ctx:
        blocks += [(ck_ref, cv_ref, off) for off in range(0, ck_ref.shape[0], KEY_BLOCK)]
    nt = (((1,), (1,)), ((), ()))
    lane = lax.broadcasted_iota(jnp.int32, (1, MLA_PAD), 1)
    outs = []
    for hd in range(MLA_HEADS):
        cols = slice(hd * MLA_PAD, (hd + 1) * MLA_PAD)
        qh = q_ref[:, cols]
        m_part = jnp.full((tq, 128), NEG, F32)
        for bi, (kr, _, off) in enumerate(blocks):
            sb = lax.dot_general(qh, kr[off:off + KEY_BLOCK, cols], nt, preferred_element_type=F32)
            s_ref[:, bi * KEY_BLOCK:(bi + 1) * KEY_BLOCK] = sb
            m_part = jnp.maximum(m_part, jnp.maximum(sb[:, :128], sb[:, 128:]))
        m = jnp.max(m_part, axis=-1, keepdims=True)
        o = jnp.zeros((tq, MLA_PAD), F32)
        for bi, (_, vr, off) in enumerate(blocks):
            p = jnp.exp2(s_ref[:, bi * KEY_BLOCK:(bi + 1) * KEY_BLOCK] - m)
            o = o + jnp.dot(p.astype(BF16), vr[off:off + KEY_BLOCK, cols], preferred_element_type=F32)
        outs.append(o * pltpu.roll(1.0 / o, MLA_V, 1))
    for hd in range(0, MLA_HEADS, 2):
        pair = jnp.where(lane < MLA_V, outs[hd], pltpu.roll(outs[hd + 1], MLA_V, 1))
        o_ref[:, (hd // 2) * 128:(hd // 2 + 1) * 128] = pair.astype(o_ref.dtype)


def _mla_attention(q, k, v, *, ck=None, cv=None, ctx_layer=None):
    b, n, w = q.shape
    nk = k.shape[1]
    tq = min(Q_TILE, n)
    in_specs = [pl.BlockSpec((None, tq, w), lambda bi, i: (bi, i, 0)),
                pl.BlockSpec((None, nk, w), lambda bi, i: (bi, 0, 0)),
                pl.BlockSpec((None, nk, w), lambda bi, i: (bi, 0, 0))]
    args = [q, k, v]
    n_ctx = 0
    if ck is not None:
        n_ctx = ck.shape[2]
        in_specs += [pl.BlockSpec((None, None, n_ctx, w), lambda bi, i: (bi, ctx_layer, 0, 0)),
                     pl.BlockSpec((None, None, n_ctx, w), lambda bi, i: (bi, ctx_layer, 0, 0))]
        args += [ck, cv]
    return _Mixer(functools.partial(_mla_attn_kernel, tq=tq, has_ctx=ck is not None), (b, n // tq), in_specs, args,
                  [pl.BlockSpec((None, tq, MLA_HEADS * MLA_V), lambda bi, i: (bi, i, 0))],
                  [jax.ShapeDtypeStruct((b, n, MLA_HEADS * MLA_V), BF16)],
                  [pltpu.VMEM((tq, nk + n_ctx), F32)])


def _pool_kernel(x_ref, w_ref, s_ref, o_ref, *, n):
    x = x_ref[...]
    t = lax.broadcasted_iota(jnp.int32, x.shape, 0)
    lane = lax.broadcasted_iota(jnp.int32, x.shape, 1)

    def later(a, d):
        return jnp.where(t < n - d, pltpu.roll(a, n - d, 0), 0.0)

    def earlier(a, d):
        return jnp.where(t >= d, pltpu.roll(a, d, 0), 0.0)

    fwd, bwd = {1: x}, {1: x}
    for k in (1, 2, 4):
        fwd[2 * k] = fwd[k] + later(fwd[k], k)
        bwd[2 * k] = bwd[k] + earlier(bwd[k], k)
    pooled = jnp.zeros_like(x)
    for gi, w in enumerate(POOL_WINDOWS):
        total = fwd[w // 2] + earlier(bwd[w // 2], 1)
        count = (jnp.clip(t + w // 2, 0, n) - jnp.clip(t - w // 2, 0, n)).astype(F32)
        in_group = (lane >= gi * POOL_GW) & (lane < (gi + 1) * POOL_GW)
        pooled = jnp.where(in_group, total / count - x, pooled)
    y = jnp.dot(pooled.astype(BF16), w_ref[...], preferred_element_type=F32)
    o_ref[...] = (y * s_ref[...]).astype(o_ref.dtype)


def _pool_once_kernel(*refs, n):
    @pl.when(pl.program_id(1) == 0)
    def _():
        _pool_kernel(*refs, n=n)


def _pool(x, wts, l, tiles=1):
    b, n, w = x.shape
    return _Mixer(functools.partial(_pool_kernel if tiles == 1 else _pool_once_kernel, n=n), (b, tiles),
                  [pl.BlockSpec((None, n, w), lambda bi, i: (bi, 0, 0)),
                   _layer_spec(wts["pool_w"], l, 2), _layer_spec(wts["pool_scale"], l, 2)],
                  [x, wts["pool_w"], wts["pool_scale"]],
                  [pl.BlockSpec((None, n, w), lambda bi, i: (bi, 0, 0))],
                  [jax.ShapeDtypeStruct((b, n, w), BF16)], [])


def _post_kernel(x_ref, p0, p1, p2, p3, mod_ref, n2_ref, wo_ref, w1_ref, w2_ref, o_ref):
    mod = mod_ref[...]
    g1 = mod[:, 2 * D_MODEL:3 * D_MODEL]
    sh2, sc2, g2 = (mod[:, 3 * D_MODEL:4 * D_MODEL], mod[:, 4 * D_MODEL:5 * D_MODEL],
                    mod[:, 5 * D_MODEL:6 * D_MODEL])
    for r0 in range(0, x_ref.shape[0], POST_CHAIN):
        rs = slice(r0, r0 + POST_CHAIN)
        x = x_ref[rs, :]
        a = jnp.zeros(x.shape, F32)
        for i, p in enumerate((p0, p1, p2, p3)):
            a = a + jnp.dot(p[rs, :], wo_ref[i * 256:(i + 1) * 256, :], preferred_element_type=F32)
        x = x + g1 * a
        h = (_rms_full(x, n2_ref[...]) * (1.0 + sc2) + sh2).astype(BF16)
        acc = jnp.zeros(x.shape, F32)
        for c in range(D_FF // FF_CHUNK):
            z = jnp.maximum(jnp.dot(h, w1_ref[:, c * FF_CHUNK:(c + 1) * FF_CHUNK], preferred_element_type=F32), 0.0)
            acc = acc + jnp.dot((z * z).astype(BF16), w2_ref[c * FF_CHUNK:(c + 1) * FF_CHUNK, :],
                                preferred_element_type=F32)
        o_ref[rs, :] = x + g2 * acc


def _post(x, parts, mod, l, wts, *, tokens_per_mod, first_mod_row, tile):
    t = x.shape[0]
    tiles_per_mod = tokens_per_mod // tile

    def tok(width):
        return pl.BlockSpec((tile, width), lambda bi, i: (bi, 0))

    def resident(arr):
        nd = arr.ndim
        return pl.BlockSpec((None,) + tuple(arr.shape[1:]), lambda bi, i: (l,) + (0,) * (nd - 1),
                            pipeline_mode=pl.Buffered(1))

    in_specs = [tok(D_MODEL), tok(256), tok(256), tok(256), tok(256),
                pl.BlockSpec((None, None, 1, 6 * D_MODEL),
                             lambda bi, i: (l, first_mod_row + bi // tiles_per_mod, 0, 0)),
                _layer_spec(wts["norm2"], l, 2), resident(wts["w_out"]), resident(wts["w1"]), resident(wts["w2"])]
    return _Mixer(_post_kernel, (t // tile, 1), in_specs,
                  [x, *parts, mod, wts["norm2"], wts["w_out"], wts["w1"], wts["w2"]],
                  [tok(D_MODEL)], [jax.ShapeDtypeStruct((t, D_MODEL), F32)], [])


def _block_ones(width, group):
    idx = np.arange(width) // group
    return jnp.asarray(idx[:, None] == idx[None, :], BF16)


def _rope_tables(n_tokens, head_lanes, first, rope_dims):
    half, quarter = rope_dims // 2, rope_dims // 4
    t = jnp.arange(n_tokens, dtype=jnp.int32)
    rows, cols = t // GRID_W, t % GRID_W
    freqs = ROPE_THETA ** (-jnp.arange(quarter, dtype=F32) / quarter)
    lane = np.arange(128) % head_lanes - first
    active = (lane >= 0) & (lane < rope_dims)
    lane = np.where(active, lane, 0)
    use_cols = lane >= half
    within = lane % half
    upper = within >= quarter
    f_idx = within % quarter
    pos = jnp.where(jnp.asarray(use_cols)[None, :], cols[:, None], rows[:, None]).astype(F32)
    ang = pos * freqs[jnp.asarray(f_idx)][None, :]
    act = jnp.asarray(active)[None, :]
    up = jnp.asarray(upper)[None, :]
    cos = jnp.where(act, jnp.cos(ang), 1.0)
    sin = jnp.where(act, jnp.sin(ang), 0.0)
    return cos, jnp.where(up, 0.0, -sin), jnp.where(up, sin, 0.0)


_SWA_Q_ORDER = (0, 2, 1, 3)
_W_IN_ROWS = 256


def _regroup_w_in_kernel(w_ref, o_ref):
    src_kpe = _C_CKV + MLA_KV_RANK
    src_sq = src_kpe + MLA_ROPE

    def put(dst, src, width):
        o_ref[:, dst:dst + width] = w_ref[:, src:src + width].astype(BF16)

    put(0, 0, src_kpe)
    for slot, head in enumerate(_SWA_Q_ORDER):
        put(_C_SWA_Q + slot * HEAD_DIM, src_sq + head * HEAD_DIM, HEAD_DIM)
    put(_C_SWA_K, src_sq + 256, 128)
    put(_C_SWA_V, src_sq + 384, 128)
    o_ref[:, _C_KPE_BLOCK:_C_KPE_BLOCK + MLA_PAD] = jnp.zeros((o_ref.shape[0], MLA_PAD), BF16)
    put(_C_KPE, src_kpe, MLA_ROPE)


def _regroup_w_in(w_in):
    d, rows, cols = w_in.shape
    return pl.pallas_call(
        _regroup_w_in_kernel, grid=(d, rows // _W_IN_ROWS),
        in_specs=[pl.BlockSpec((None, _W_IN_ROWS, cols), lambda l, i: (l, i, 0))],
        out_specs=pl.BlockSpec((None, _W_IN_ROWS, W_IN_COLS), lambda l, i: (l, i, 0)),
        out_shape=jax.ShapeDtypeStruct((d, rows, W_IN_COLS), BF16),
        compiler_params=_params(2), name="regroup_w_in",
    )(w_in)


def _regroup_w_out_kernel(w_ref, o_ref):
    first = 3 * 256
    o_ref[0:first, :] = w_ref[0:first, :].astype(BF16)
    for slot, head in enumerate(_SWA_Q_ORDER):
        o_ref[first + slot * HEAD_DIM:first + (slot + 1) * HEAD_DIM, :] = (
            w_ref[first + head * HEAD_DIM:first + (head + 1) * HEAD_DIM, :].astype(BF16))


def _regroup_w_out(w_out):
    d, rows, cols = w_out.shape
    return pl.pallas_call(
        _regroup_w_out_kernel, grid=(d, 1),
        in_specs=[pl.BlockSpec((None, rows, cols), lambda l, i: (l, 0, 0))],
        out_specs=pl.BlockSpec((None, rows, cols), lambda l, i: (l, 0, 0)),
        out_shape=jax.ShapeDtypeStruct((d, rows, cols), BF16),
        compiler_params=_params(2), name="regroup_w_out",
    )(w_out)


def _prepare(norm1, norm2, w_in, w_out, na_q_norm, na_k_norm, pool_w, pool_scale, mla_q_a_norm, mla_w_uq,
             mla_kv_norm, mla_w_ukv, mla_q_norm, mla_k_norm, swa_q_norm, swa_k_norm, mlp_w1, mlp_w2):
    d = DEPTH

    def pad_heads(a, width):
        a = a.reshape(a.shape[:-1] + (MLA_HEADS, width))
        a = jnp.pad(a, [(0, 0)] * (a.ndim - 1) + [(0, MLA_PAD - width)])
        return a.reshape(a.shape[:-2] + (MLA_HEADS * MLA_PAD,))

    ukv = mla_w_ukv.reshape(d, MLA_KV_RANK, MLA_HEADS, MLA_NOPE + MLA_V)
    eye = jnp.eye(POOL_GROUPS, dtype=pool_w.dtype)
    pool_bd = jnp.einsum("lgce,gh->lgche", pool_w, eye).reshape(d, 256, 256)

    def gain(g, reps, scale=1.0):
        return (jnp.tile(g, (1, reps)) * scale).reshape(d, 1, -1)

    return {
        "norm1": norm1.reshape(d, 1, D_MODEL), "norm2": norm2.reshape(d, 1, D_MODEL),
        "w_in": _regroup_w_in(w_in),
        "w_uq": pad_heads(mla_w_uq, MLA_QK).astype(BF16),
        "w_uk": pad_heads(ukv[..., :MLA_NOPE].reshape(d, MLA_KV_RANK, -1), MLA_NOPE).astype(BF16),
        "w_uv": pad_heads(ukv[..., MLA_NOPE:].reshape(d, MLA_KV_RANK, -1), MLA_V).astype(BF16),
        "w_out": _regroup_w_out(w_out),
        "w1": mlp_w1.astype(BF16), "w2": mlp_w2.astype(BF16),
        "pool_w": pool_bd.astype(BF16), "pool_scale": pool_scale.reshape(d, 1, 256),
        "g_naq": gain(na_q_norm, NA_HEADS, HEAD_DIM ** -0.5 * LOG2E), "g_nak": gain(na_k_norm, NA_HEADS),
        "g_qa": mla_q_a_norm.reshape(d, 1, MLA_Q_RANK), "g_kv": mla_kv_norm.reshape(d, 1, MLA_KV_RANK),
        "g_mq": pad_heads(jnp.tile(mla_q_norm, (1, MLA_HEADS)) * (MLA_QK ** -0.5 * LOG2E), MLA_QK).reshape(d, 1, -1),
        "g_mk": pad_heads(jnp.tile(mla_k_norm, (1, MLA_HEADS)), MLA_QK).reshape(d, 1, -1),
        "g_sq": gain(swa_q_norm, SWA_HEADS, HEAD_DIM ** -0.5 * LOG2E), "g_sk": gain(swa_k_norm, SWA_KV_HEADS),
    }


def _constants():
    width = MLA_HEADS * MLA_PAD
    e = np.zeros((MLA_ROPE, width), np.float32)
    for h in range(MLA_HEADS):
        e[np.arange(MLA_ROPE), h * MLA_PAD + MLA_NOPE + np.arange(MLA_ROPE)] = 1.0
    v_ones = (np.arange(width) % MLA_PAD >= MLA_V).astype(np.float32)[None, :]
    return {"e_kpe": jnp.asarray(e, BF16), "v_ones": jnp.asarray(v_ones), "m256": _block_ones(256, HEAD_DIM),
            "m128": _block_ones(128, HEAD_DIM), "m512": _block_ones(width, MLA_PAD)}


def kernel(x_prompt, x_sample, cache_na_k, cache_na_v, cache_mla_ckv, cache_mla_kpe, cache_swa_k, cache_swa_v, c, c_ctx, norm1, norm2, w_ada, b_ada, w_in, w_out, na_q_norm, na_k_norm, na_rpb, pool_w, pool_scale, mla_q_a_norm, mla_w_uq, mla_kv_norm, mla_w_ukv, mla_q_norm, mla_k_norm, swa_q_norm, swa_k_norm, swa_sink, mlp_w1, mlp_w2):
    bc, nc, _ = x_prompt.shape
    bl, nl, _ = x_sample.shape
    n_past = cache_na_k.shape[2]
    wts = _prepare(norm1, norm2, w_in, w_out, na_q_norm, na_k_norm, pool_w, pool_scale, mla_q_a_norm, mla_w_uq,
                   mla_kv_norm, mla_w_ukv, mla_q_norm, mla_k_norm, swa_q_norm, swa_k_norm, mlp_w1, mlp_w2)
    consts = _constants()
    rope_tabs = _rope_tables(nl, MLA_PAD, MLA_NOPE, MLA_ROPE) + _rope_tables(nl, HEAD_DIM, 0, HEAD_DIM)

    cond = jnp.concatenate([c_ctx[None, :], c, jnp.zeros((8 - 1 - bl, D_MODEL), c.dtype)], axis=0)
    mod = _adaln(cond, w_ada, b_ada).reshape(DEPTH, 8, 1, 6 * D_MODEL)

    def cache(a):
        return a.reshape(bl, DEPTH, n_past, -1).astype(BF16)

    c_nak, c_nav, c_sk, c_sv = cache(cache_na_k), cache(cache_na_v), cache(cache_swa_k), cache(cache_swa_v)
    c_mk, c_mv = _mla_cache(cache_mla_ckv, cache_mla_kpe, wts, consts)
    na_bias = _na_bias(na_rpb, nl // GRID_W)

    xp = x_prompt.reshape(bc * nc, D_MODEL)
    xs = x_sample.reshape(bl * nl, D_MODEL)
    states = None
    for l in range(DEPTH):
        sink = swa_sink[l]
        outs = _pre(xp, mod, l, wts, consts, None, tokens_per_mod=bc * nc, first_mod_row=0, state_seq=nc,
                    carried=states, name="pre_ctx")
        naq, nak, nav, pool_in, mq, mk, mv, sq, sk, sv = [o.reshape(bc, nc, -1) for o in outs[:10]]
        states = outs[10:]
        ctx_mixers = [_pool(pool_in, wts, l), _mla_attention(mq, mk, mv), _attention(naq, nak, nav, dl=HEAD_DIM),
                      _attention(sq, sk, sv, dl=HEAD_DIM, sink=sink)]
        outs = _pre(xs, mod, l, wts, consts, rope_tabs, tokens_per_mod=nl, first_mod_row=1, name="pre_lat")
        naq, nak, nav, pool_in, mq, mk, mv, sq, sk, sv = [o.reshape(bl, nl, -1) for o in outs]
        o_pool, o_mla, o_na, o_swa = _run_mixers(
            [_pool(pool_in, wts, l, tiles=nl // Q_TILE),
             _mla_attention(mq, mk, mv, ck=c_mk, cv=c_mv, ctx_layer=l),
             _attention(naq, nak, nav, dl=HEAD_DIM, mode="na", ck=c_nak, cv=c_nav, ctx_layer=l,
                        bias=na_bias, bias_layer=l),
             _attention(sq, sk, sv, dl=HEAD_DIM, mode="swa", ck=c_sk, cv=c_sv, ctx_layer=l, sink=sink)],
            "lat_mix")
        c_pool, c_mla, c_na, c_swa, xs = _run_mixers(
            ctx_mixers + [_post(xs, [p.reshape(bl * nl, 256) for p in (o_na, o_pool, o_mla, o_swa)], mod, l, wts,
                                tokens_per_mod=nl, first_mod_row=1, tile=bl * nl // bc)], "ctx_mix_post_lat")
        (xp,) = _run_mixers(
            [_post(xp, [p.reshape(bc * nc, 256) for p in (c_na, c_pool, c_mla, c_swa)], mod, l, wts,
                   tokens_per_mod=bc * nc, first_mod_row=0, tile=POST_TILE)], "post_ctx")

    st_nak, st_nav, st_ckv, st_kpe, st_sk, st_sv = states
    return (xp.reshape(bc, nc, D_MODEL), xs.reshape(bl, nl, D_MODEL),
            st_nak.reshape(bc, DEPTH, nc, NA_HEADS, HEAD_DIM), st_nav.reshape(bc, DEPTH, nc, NA_HEADS, HEAD_DIM),
            st_ckv, st_kpe,
            st_sk.reshape(bc, DEPTH, nc, SWA_KV_HEADS, HEAD_DIM), st_sv.reshape(bc, DEPTH, nc, SWA_KV_HEADS, HEAD_DIM))
```

```python
import functools

import numpy as np
import jax
import jax.numpy as jnp
from jax import lax
from jax.experimental import pallas as pl
from jax.experimental.pallas import tpu as pltpu

F32, BF16 = jnp.float32, jnp.bfloat16

D_MODEL = 1024
DEPTH = 4
GRID_W = 64
HEAD_DIM = 64
NA_HEADS = 4
NA_WIN_ROWS = 8
NA_WIN_COLS = 16
POOL_GROUPS = 4
POOL_GW = 64
POOL_WINDOWS = (2, 4, 8, 16)
MLA_HEADS = 4
MLA_Q_RANK = 256
MLA_KV_RANK = 128
MLA_NOPE = 64
MLA_ROPE = 32
MLA_V = 64
MLA_QK = MLA_NOPE + MLA_ROPE
MLA_PAD = 128
SWA_HEADS = 4
SWA_KV_HEADS = 2
SWA_WINDOW = 128
D_FF = 4 * D_MODEL
ROPE_THETA = 10000.0
EPS = 1e-6
NEG = -1e30
LOG2E = 1.4426950408889634

POST_TILE = 1024
POST_CHAIN = 256
PRE_TILE = 512
PRE_CHAIN = 128
Q_TILE = 256
KEY_BLOCK = 256
NA_TILE_ROWS = Q_TILE // GRID_W
NA_KEY_ROWS = NA_TILE_ROWS + NA_WIN_ROWS
SWA_KEYS = Q_TILE + 2 * SWA_WINDOW
ADA_COLS = 2048
FF_CHUNK = 1024
W_IN_COLS = 2048
VMEM_LIMIT = 48 * 1024 * 1024

_C_NA_Q, _C_NA_K, _C_NA_V, _C_POOL, _C_CQ, _C_CKV, _C_SWA_Q, _C_SWA_K, _C_SWA_V, _C_KPE_BLOCK = (
    0, 256, 512, 768, 1024, 1280, 1408, 1664, 1792, 1920)
_C_KPE = _C_KPE_BLOCK + MLA_NOPE


def _params(n_grid):
    return pltpu.CompilerParams(dimension_semantics=("arbitrary",) * n_grid, vmem_limit_bytes=VMEM_LIMIT)


def _layer_spec(arr, l, n_grid):
    nd = arr.ndim
    shape = (None,) + tuple(arr.shape[1:])
    if n_grid == 1:
        return pl.BlockSpec(shape, lambda i: (l,) + (0,) * (nd - 1))
    return pl.BlockSpec(shape, lambda b, i: (l,) + (0,) * (nd - 1))


def _const_spec(arr, n_grid):
    nd = arr.ndim
    if n_grid == 1:
        return pl.BlockSpec(arr.shape, lambda i: (0,) * nd)
    return pl.BlockSpec(arr.shape, lambda b, i: (0,) * nd)


def _rms_full(x, g):
    return x * lax.rsqrt(jnp.mean(x * x, axis=-1, keepdims=True) + EPS) * g


def _split_dot(s, m):
    hi = s.astype(BF16)
    lo = (s - hi.astype(F32)).astype(BF16)
    return jnp.dot(hi, m, preferred_element_type=F32) + jnp.dot(lo, m, preferred_element_type=F32)


def _rms_heads(x, g, m, dim):
    ss = jnp.dot((x * x).astype(BF16), m, preferred_element_type=F32)
    return x * lax.rsqrt(ss * (1.0 / dim) + EPS) * g


def _rms_lanes(x, g, dim):
    ss = jnp.sum(x * x, axis=-1, keepdims=True)
    return x * lax.rsqrt(ss * (1.0 / dim) + EPS) * g


def _rope(x, c_ref, sa_ref, sb_ref, quarter, rows):
    return (x * c_ref[rows, :] + pltpu.roll(x, 128 - quarter, 1) * sa_ref[rows, :]
            + pltpu.roll(x, quarter, 1) * sb_ref[rows, :])


def _adaln_kernel(cond_ref, w_ref, b_ref, o_ref):
    s = jax.nn.silu(cond_ref[...]).astype(BF16)
    o_ref[...] = jnp.dot(s, w_ref[...].astype(BF16), preferred_element_type=F32) + b_ref[...]


def _adaln(cond, w_ada, b_ada):
    n_out = w_ada.shape[-1]
    return pl.pallas_call(
        _adaln_kernel,
        grid=(DEPTH, n_out // ADA_COLS),
        in_specs=[pl.BlockSpec((8, D_MODEL), lambda l, j: (0, 0)),
                  pl.BlockSpec((None, D_MODEL, ADA_COLS), lambda l, j: (l, 0, j)),
                  pl.BlockSpec((None, 1, ADA_COLS), lambda l, j: (l, 0, j))],
        out_specs=pl.BlockSpec((None, 8, ADA_COLS), lambda l, j: (l, 0, j)),
        out_shape=jax.ShapeDtypeStruct((DEPTH, 8, n_out), F32),
        compiler_params=_params(2),
        name="adaln",
    )(cond, w_ada, b_ada.reshape(DEPTH, 1, n_out))


def _pre_kernel(*refs, rope, state_seq, n_carried, state_layer):
    (x_ref, mod_ref, n1_ref, win_ref, wuq_ref, wuk_ref, wuv_ref, ones_ref, m256_ref, m128_ref,
     g_naq, g_nak, g_qa, g_kv, g_mq, g_mk, g_sq, g_sk) = refs[:18]
    pos = 18
    if rope:
        cm, sam, sbm, cs, sas, sbs = refs[pos:pos + 6]
        pos += 6
    pos += n_carried
    (o_naq, o_nak, o_nav, o_pool, o_mq, o_mk, o_mv, o_sq, o_sk, o_sv) = refs[pos:pos + 10]
    pos += 10
    if state_seq:
        state_refs = refs[pos:pos + 6]

    mod = mod_ref[...]
    sh1 = mod[:, 0:D_MODEL]
    gain1 = n1_ref[...] * (1.0 + mod[:, D_MODEL:2 * D_MODEL])
    m256 = m256_ref[...]
    for r0 in range(0, x_ref.shape[0], PRE_CHAIN):
        rs = slice(r0, r0 + PRE_CHAIN)
        x = x_ref[rs, :]
        h = x * lax.rsqrt(jnp.mean(x * x, axis=-1, keepdims=True) + EPS) * gain1 + sh1
        y = jnp.dot(h.astype(BF16), win_ref[...], preferred_element_type=F32)

        na_k = _rms_heads(y[:, _C_NA_K:_C_NA_K + 256], g_nak[...], m256, HEAD_DIM)
        na_v = y[:, _C_NA_V:_C_NA_V + 256]
        o_naq[rs, :] = _rms_heads(y[:, _C_NA_Q:_C_NA_Q + 256], g_naq[...], m256, HEAD_DIM).astype(BF16)
        o_nak[rs, :] = na_k.astype(BF16)
        o_nav[rs, :] = na_v.astype(BF16)
        o_pool[rs, :] = y[:, _C_POOL:_C_POOL + 256]

        cq = _rms_full(y[:, _C_CQ:_C_CQ + MLA_Q_RANK], g_qa[...])
        mq = jnp.dot(cq.astype(BF16), wuq_ref[...], preferred_element_type=F32)
        ckv = _rms_full(y[:, _C_CKV:_C_CKV + MLA_KV_RANK], g_kv[...])
        ckv_b = ckv.astype(BF16)
        mk = jnp.dot(ckv_b, wuk_ref[...], preferred_element_type=F32)
        kpe_block = y[:, _C_KPE_BLOCK:_C_KPE_BLOCK + MLA_PAD]
        o_mv[rs, :] = (jnp.dot(ckv_b, wuv_ref[...], preferred_element_type=F32) + ones_ref[...]).astype(BF16)
        for hd in range(MLA_HEADS):
            cols = slice(hd * MLA_PAD, (hd + 1) * MLA_PAD)
            qh = _rms_lanes(mq[:, cols], g_mq[:, cols], MLA_QK)
            kh = _rms_lanes(mk[:, cols] + kpe_block, g_mk[:, cols], MLA_QK)
            if rope:
                qh = _rope(qh, cm, sam, sbm, MLA_ROPE // 4, rs)
                kh = _rope(kh, cm, sam, sbm, MLA_ROPE // 4, rs)
            o_mq[rs, cols] = qh.astype(BF16)
            o_mk[rs, cols] = kh.astype(BF16)

        sq = _rms_heads(y[:, _C_SWA_Q:_C_SWA_Q + 256], g_sq[...], m256, HEAD_DIM)
        sk = _rms_heads(y[:, _C_SWA_K:_C_SWA_K + 128], g_sk[...], m128_ref[...], HEAD_DIM)
        sv = y[:, _C_SWA_V:_C_SWA_V + 128]
        if state_seq:
            seq, s0 = r0 // state_seq, r0 % state_seq
            vals = (na_k, na_v, ckv, y[:, _C_KPE:_C_KPE + MLA_ROPE], sk, sv)
            for ref, val, tokens_minor in zip(state_refs, vals, _STATE_TOKENS_MINOR):
                if tokens_minor:
                    val, at = val.T, (slice(None), slice(s0, s0 + PRE_CHAIN))
                else:
                    at = (slice(s0, s0 + PRE_CHAIN), slice(None))
                if state_layer is None:
                    ref[(seq,) + at] = val
                else:
                    for k in range(DEPTH):
                        ref[(seq, k) + at] = val if k == state_layer else jnp.zeros_like(val)
        if rope:
            for blk in range(2):
                cols = slice(blk * 128, (blk + 1) * 128)
                o_sq[rs, cols] = _rope(sq[:, cols], cs, sas, sbs, HEAD_DIM // 4, rs).astype(BF16)
            sk = _rope(sk, cs, sas, sbs, HEAD_DIM // 4, rs)
        else:
            o_sq[rs, :] = sq.astype(BF16)
        o_sk[rs, :] = sk.astype(BF16)
        o_sv[rs, :] = sv.astype(BF16)


_STATE_WIDTHS = (NA_HEADS * HEAD_DIM, NA_HEADS * HEAD_DIM, MLA_KV_RANK, MLA_ROPE,
                 SWA_KV_HEADS * HEAD_DIM, SWA_KV_HEADS * HEAD_DIM)
_STATE_TOKENS_MINOR = (True, True, False, True, True, True)


def _pre(x, mod, l, wts, consts, rope_tabs, *, tokens_per_mod, first_mod_row, name, state_seq=0, carried=None):
    t = x.shape[0]
    tm = PRE_TILE
    rope = rope_tabs is not None
    tiles_per_mod = tokens_per_mod // tm
    n_pos = rope_tabs[0].shape[0] // tm if rope else 1

    def row(width, dtype):
        return jax.ShapeDtypeStruct((t, width), dtype)

    def tok(width):
        return pl.BlockSpec((tm, width), lambda i: (i, 0))

    in_specs = [tok(D_MODEL),
                pl.BlockSpec((None, None, 1, 6 * D_MODEL), lambda i: (l, first_mod_row + i // tiles_per_mod, 0, 0)),
                _layer_spec(wts["norm1"], l, 1), _layer_spec(wts["w_in"], l, 1), _layer_spec(wts["w_uq"], l, 1),
                _layer_spec(wts["w_uk"], l, 1), _layer_spec(wts["w_uv"], l, 1),
                _const_spec(consts["v_ones"], 1), _const_spec(consts["m256"], 1), _const_spec(consts["m128"], 1)]
    args = [x, mod, wts["norm1"], wts["w_in"], wts["w_uq"], wts["w_uk"], wts["w_uv"],
            consts["v_ones"], consts["m256"], consts["m128"]]
    for nm in ("g_naq", "g_nak", "g_qa", "g_kv", "g_mq", "g_mk", "g_sq", "g_sk"):
        in_specs.append(_layer_spec(wts[nm], l, 1))
        args.append(wts[nm])
    if rope:
        for tab in rope_tabs:
            in_specs.append(pl.BlockSpec((tm, tab.shape[1]), lambda i: (i % n_pos, 0)))
            args.append(tab)
    widths = (256, 256, 256, 256, 512, 512, 512, 256, 128, 128)
    dtypes = (BF16, BF16, BF16, F32, BF16, BF16, BF16, BF16, BF16, BF16)
    out_shape = [row(w, d) for w, d in zip(widths, dtypes)]
    out_specs = [tok(w) for w in widths]
    aliases = {}
    n_carried = 0
    if state_seq:
        seqs_per_tile = tm // state_seq
        if carried is not None:
            n_carried = len(carried)
            for k, buf in enumerate(carried):
                aliases[len(args)] = len(out_shape) + k
                in_specs.append(pl.BlockSpec(memory_space=pl.ANY))
                args.append(buf)
        for w, tokens_minor in zip(_STATE_WIDTHS, _STATE_TOKENS_MINOR):
            tail = (w, state_seq) if tokens_minor else (state_seq, w)
            out_shape.append(jax.ShapeDtypeStruct((t // state_seq, DEPTH) + tail, F32))
            if carried is None:
                out_specs.append(pl.BlockSpec((seqs_per_tile, DEPTH) + tail, lambda i: (i, 0, 0, 0)))
            else:
                out_specs.append(pl.BlockSpec((seqs_per_tile, None) + tail, lambda i: (i, l, 0, 0)))
    return pl.pallas_call(
        functools.partial(_pre_kernel, rope=rope, state_seq=state_seq, n_carried=n_carried,
                          state_layer=l if state_seq and carried is None else None),
        grid=(t // tm,), in_specs=in_specs, out_specs=out_specs, out_shape=out_shape,
        input_output_aliases=aliases, compiler_params=_params(1), name=name,
    )(*args)


def _mla_cache_kernel(ckv_ref, kpe_ref, wuk_ref, wuv_ref, e_ref, ones_ref, m512_ref, g_mk, ok_ref, ov_ref):
    ckv_b = ckv_ref[...].astype(BF16)
    mk = jnp.dot(ckv_b, wuk_ref[...], preferred_element_type=F32) + _split_dot(kpe_ref[...], e_ref[...])
    ok_ref[...] = _rms_heads(mk, g_mk[...], m512_ref[...], MLA_QK).astype(BF16)
    ov_ref[...] = (jnp.dot(ckv_b, wuv_ref[...], preferred_element_type=F32) + ones_ref[...]).astype(BF16)


def _mla_cache(c_ckv, c_kpe, wts, consts):
    b, _, n, _ = c_ckv.shape
    width = MLA_HEADS * MLA_PAD

    def lay(arr):
        nd = arr.ndim
        return pl.BlockSpec((None,) + tuple(arr.shape[1:]), lambda l, i: (l,) + (0,) * (nd - 1))

    return pl.pallas_call(
        _mla_cache_kernel,
        grid=(DEPTH, b),
        in_specs=[pl.BlockSpec((None, None, n, MLA_KV_RANK), lambda l, i: (i, l, 0, 0)),
                  pl.BlockSpec((None, None, n, MLA_ROPE), lambda l, i: (i, l, 0, 0)),
                  lay(wts["w_uk"]), lay(wts["w_uv"]), _const_spec(consts["e_kpe"], 2),
                  _const_spec(consts["v_ones"], 2), _const_spec(consts["m512"], 2), lay(wts["g_mk"])],
        out_specs=[pl.BlockSpec((None, None, n, width), lambda l, i: (i, l, 0, 0)),
                   pl.BlockSpec((None, None, n, width), lambda l, i: (i, l, 0, 0))],
        out_shape=[jax.ShapeDtypeStruct((b, DEPTH, n, width), BF16),
                   jax.ShapeDtypeStruct((b, DEPTH, n, width), BF16)],
        compiler_params=_params(2), name="mla_cache",
    )(c_ckv, c_kpe, wts["w_uk"], wts["w_uv"], consts["e_kpe"], consts["v_ones"], consts["m512"], wts["g_mk"])


_NA_VARIANTS = ((0, 0), (4, 0), (24, 20), (28, 20))


def _na_bias_kernel(rpb_ref, o_ref, *, grid_rows):
    l, hd = pl.program_id(0), pl.program_id(1)
    n_dr, n_dc = 2 * NA_WIN_ROWS - 1, 2 * NA_WIN_COLS - 1
    base = (l * NA_HEADS + hd) * (n_dr * n_dc)
    c = lax.broadcasted_iota(jnp.int32, (GRID_W, GRID_W), 0)
    kc = lax.broadcasted_iota(jnp.int32, (GRID_W, GRID_W), 1)
    dcv = kc - c + (NA_WIN_COLS - 1)
    col_start = jnp.clip(c - NA_WIN_COLS // 2, 0, GRID_W - NA_WIN_COLS)
    col_ok = (kc >= col_start) & (kc < col_start + NA_WIN_COLS)
    neg = jnp.full((GRID_W, GRID_W), NEG, F32)
    blocks = {}

    def block(dr):
        if dr not in blocks:
            t = jnp.zeros((GRID_W, GRID_W), F32)
            for dc in range(n_dc):
                t = jnp.where(dcv == dc, rpb_ref[base + dr * n_dc + dc] * LOG2E, t)
            blocks[dr] = jnp.where(col_ok, t, NEG)
        return blocks[dr]

    for v, (r0, w0) in enumerate(_NA_VARIANTS):
        for rl in range(NA_TILE_ROWS):
            r = r0 + rl
            kr0 = min(max(r - NA_WIN_ROWS // 2, 0), grid_rows - NA_WIN_ROWS)
            for j in range(NA_KEY_ROWS):
                kr = w0 + j
                inside = kr0 <= kr < kr0 + NA_WIN_ROWS
                o_ref[v, rl * GRID_W:(rl + 1) * GRID_W, j * GRID_W:(j + 1) * GRID_W] = (
                    block(kr - r + NA_WIN_ROWS - 1) if inside else neg)


def _na_bias(na_rpb, grid_rows):
    n_k = NA_KEY_ROWS * GRID_W
    return pl.pallas_call(
        functools.partial(_na_bias_kernel, grid_rows=grid_rows),
        grid=(DEPTH, NA_HEADS),
        in_specs=[pl.BlockSpec(memory_space=pltpu.SMEM)],
        out_specs=pl.BlockSpec((None, len(_NA_VARIANTS), None, Q_TILE, n_k), lambda l, h: (l, 0, h, 0, 0)),
        out_shape=jax.ShapeDtypeStruct((DEPTH, len(_NA_VARIANTS), NA_HEADS, Q_TILE, n_k), F32),
        compiler_params=_params(2), name="na_bias",
    )(na_rpb.reshape(-1))


def _attn_kernel(*refs, tq, wk, wv, dl, n_slabs, mode, has_ctx, has_sink, n_keys):
    q_ref, k_ref, v_ref = refs[:3]
    pos = 3
    if has_ctx:
        ck_ref, cv_ref = refs[pos:pos + 2]
        pos += 2
    if mode == "na":
        bias_ref = refs[pos]
        pos += 1
    if has_sink:
        sink_ref = refs[pos]
        pos += 1
    o_ref = refs[pos]
    i = pl.program_id(1)
    hs = wk // dl

    valid = None
    if mode == "full":
        kl, vl = k_ref[...], v_ref[...]
    elif mode == "swa":
        w0 = pl.multiple_of(jnp.clip(i * tq - SWA_WINDOW, 0, n_keys - SWA_KEYS), SWA_WINDOW)
        kl, vl = k_ref[pl.ds(w0, SWA_KEYS), :], v_ref[pl.ds(w0, SWA_KEYS), :]
        qpos = i * tq + lax.broadcasted_iota(jnp.int32, (tq, SWA_KEYS), 0)
        kpos = w0 + lax.broadcasted_iota(jnp.int32, (tq, SWA_KEYS), 1)
        valid = jnp.abs(kpos - qpos) <= SWA_WINDOW
    else:
        last = n_keys // GRID_W - NA_KEY_ROWS
        w0 = pl.multiple_of(jnp.clip(i * NA_TILE_ROWS - NA_WIN_ROWS // 2, 0, last) * GRID_W, GRID_W)
        kl, vl = k_ref[pl.ds(w0, NA_KEY_ROWS * GRID_W), :], v_ref[pl.ds(w0, NA_KEY_ROWS * GRID_W), :]
    if has_ctx:
        ck_t, cv_t = ck_ref[...].astype(BF16), cv_ref[...].astype(BF16)

    lane_k = lax.broadcasted_iota(jnp.int32, (1, wk), 1)
    lane_v = lax.broadcasted_iota(jnp.int32, (1, wv), 1)
    nt = (((1,), (1,)), ((), ()))
    for s in range(n_slabs):
        qs = q_ref[:, s * wk:(s + 1) * wk]
        acc = jnp.zeros((tq, wv), F32)
        for j in range(hs):
            head = j * n_slabs + s
            qm = jnp.where((lane_k >= j * dl) & (lane_k < (j + 1) * dl), qs, jnp.zeros_like(qs))
            s_loc = lax.dot_general(qm, kl, nt, preferred_element_type=F32)
            if mode == "na":
                s_loc = s_loc + bias_ref[head]
            if valid is not None:
                s_loc = jnp.where(valid, s_loc, NEG)
            m = jnp.max(s_loc, axis=-1, keepdims=True)
            if has_ctx:
                s_ctx = jnp.dot(qm, ck_t, preferred_element_type=F32)
                m = jnp.maximum(m, jnp.max(s_ctx, axis=-1, keepdims=True))
            if has_sink:
                sink = sink_ref[head] * LOG2E
                m = jnp.maximum(m, sink)
            p_loc = jnp.exp2(s_loc - m)
            den = jnp.sum(p_loc, axis=-1, keepdims=True)
            o = jnp.dot(p_loc.astype(BF16), vl, preferred_element_type=F32)
            if has_ctx:
                p_ctx = jnp.exp2(s_ctx - m)
                den = den + jnp.sum(p_ctx, axis=-1, keepdims=True)
                o = o + lax.dot_general(p_ctx.astype(BF16), cv_t, nt, preferred_element_type=F32)
            if has_sink:
                den = den + jnp.exp2(sink - m)
            out_lanes = (lane_v >= j * HEAD_DIM) & (lane_v < (j + 1) * HEAD_DIM)
            acc = acc + jnp.where(out_lanes, o * (1.0 / den), 0.0)
        o_ref[:, s * wv:(s + 1) * wv] = acc.astype(o_ref.dtype)


class _Mixer:
    def __init__(self, body, grid, in_specs, args, out_specs, out_shape, scratch):
        self.body, self.grid = body, grid
        self.in_specs, self.args = in_specs, args
        self.out_specs, self.out_shape, self.scratch = out_specs, out_shape, scratch


def _mix_kernel(*refs, bodies, n_in, n_out, n_scratch):
    ins, outs, scr = refs[:sum(n_in)], refs[sum(n_in):sum(n_in) + sum(n_out)], refs[sum(n_in) + sum(n_out):]
    i0 = o0 = s0 = 0
    for body, ni, no, ns in zip(bodies, n_in, n_out, n_scratch):
        body(*ins[i0:i0 + ni], *outs[o0:o0 + no], *scr[s0:s0 + ns])
        i0, o0, s0 = i0 + ni, o0 + no, s0 + ns


def _run_mixers(mixers, name):
    grid = mixers[0].grid
    assert all(m.grid == grid for m in mixers)
    return pl.pallas_call(
        functools.partial(_mix_kernel, bodies=[m.body for m in mixers], n_in=[len(m.args) for m in mixers],
                          n_out=[len(m.out_shape) for m in mixers], n_scratch=[len(m.scratch) for m in mixers]),
        grid=grid,
        in_specs=[s for m in mixers for s in m.in_specs],
        out_specs=[s for m in mixers for s in m.out_specs],
        out_shape=[s for m in mixers for s in m.out_shape],
        scratch_shapes=[s for m in mixers for s in m.scratch],
        compiler_params=_params(2), name=name,
    )(*[a for m in mixers for a in m.args])


def _attention(q, k, v, *, dl, mode="full", ck=None, cv=None, ctx_layer=None, bias=None, bias_layer=None,
               sink=None):
    b, n, wq = q.shape
    _, nk, wk = k.shape
    wv = v.shape[-1]
    n_slabs = wq // wk
    tq = min(Q_TILE, n)
    in_specs = [pl.BlockSpec((None, tq, wq), lambda bi, i: (bi, i, 0)),
                pl.BlockSpec((None, nk, wk), lambda bi, i: (bi, 0, 0)),
                pl.BlockSpec((None, nk, wv), lambda bi, i: (bi, 0, 0))]
    args = [q, k, v]
    if ck is not None:
        n_ctx = ck.shape[3]
        in_specs += [pl.BlockSpec((None, None, wk, n_ctx), lambda bi, i: (bi, ctx_layer, 0, 0)),
                     pl.BlockSpec((None, None, wv, n_ctx), lambda bi, i: (bi, ctx_layer, 0, 0))]
        args += [ck, cv]
    if mode == "na":
        n_tiles = n // tq

        def variant(bi, i):
            v_idx = jnp.where(i == 0, 0, jnp.where(i <= n_tiles - 3, 1, i - (n_tiles - 4)))
            return (bias_layer, v_idx, 0, 0, 0)

        in_specs.append(pl.BlockSpec((None, None) + tuple(bias.shape[2:]), variant))
        args.append(bias)
    if sink is not None:
        in_specs.append(pl.BlockSpec(memory_space=pltpu.SMEM))
        args.append(sink)
    body = functools.partial(_attn_kernel, tq=tq, wk=wk, wv=wv, dl=dl, n_slabs=n_slabs, mode=mode,
                             has_ctx=ck is not None, has_sink=sink is not None, n_keys=nk)
    return _Mixer(body, (b, n // tq), in_specs, args,
                  [pl.BlockSpec((None, tq, n_slabs * wv), lambda bi, i: (bi, i, 0))],
                  [jax.ShapeDtypeStruct((b, n, n_slabs * wv), BF16)], [])


def _mla_attn_kernel(*refs, tq, has_ctx):
    q_ref, k_ref, v_ref = refs[:3]
    pos = 3
    if has_ctx:
        ck_ref, cv_ref = refs[pos:pos + 2]
        pos += 2
    o_ref, s_ref = refs[pos], refs[pos + 1]
    blocks = [(k_ref, v_ref, off) for off in range(0, k_ref.shape[0], KEY_BLOCK)]
    if has_ctx:
        blocks += [(ck_ref, cv_ref, off) for off in range(0, ck_ref.shape[0], KEY_BLOCK)]
    nt = (((1,), (1,)), ((), ()))
    lane = lax.broadcasted_iota(jnp.int32, (1, MLA_PAD), 1)
    outs = []
    for hd in range(MLA_HEADS):
        cols = slice(hd * MLA_PAD, (hd + 1) * MLA_PAD)
        qh = q_ref[:, cols]
        m_part = jnp.full((tq, 128), NEG, F32)
        for bi, (kr, _, off) in enumerate(blocks):
            sb = lax.dot_general(qh, kr[off:off + KEY_BLOCK, cols], nt, preferred_element_type=F32)
            s_ref[:, bi * KEY_BLOCK:(bi + 1) * KEY_BLOCK] = sb
            m_part = jnp.maximum(m_part, jnp.maximum(sb[:, :128], sb[:, 128:]))
        m = jnp.max(m_part, axis=-1, keepdims=True)
        o = jnp.zeros((tq, MLA_PAD), F32)
        for bi, (_, vr, off) in enumerate(blocks):
            p = jnp.exp2(s_ref[:, bi * KEY_BLOCK:(bi + 1) * KEY_BLOCK] - m)
            o = o + jnp.dot(p.astype(BF16), vr[off:off + KEY_BLOCK, cols], preferred_element_type=F32)
        outs.append(o * pltpu.roll(1.0 / o, MLA_V, 1))
    for hd in range(0, MLA_HEADS, 2):
        pair = jnp.where(lane < MLA_V, outs[hd], pltpu.roll(outs[hd + 1], MLA_V, 1))
        o_ref[:, (hd // 2) * 128:(hd // 2 + 1) * 128] = pair.astype(o_ref.dtype)


def _mla_attention(q, k, v, *, ck=None, cv=None, ctx_layer=None):
    b, n, w = q.shape
    nk = k.shape[1]
    tq = min(Q_TILE, n)
    in_specs = [pl.BlockSpec((None, tq, w), lambda bi, i: (bi, i, 0)),
                pl.BlockSpec((None, nk, w), lambda bi, i: (bi, 0, 0)),
                pl.BlockSpec((None, nk, w), lambda bi, i: (bi, 0, 0))]
    args = [q, k, v]
    n_ctx = 0
    if ck is not None:
        n_ctx = ck.shape[2]
        in_specs += [pl.BlockSpec((None, None, n_ctx, w), lambda bi, i: (bi, ctx_layer, 0, 0)),
                     pl.BlockSpec((None, None, n_ctx, w), lambda bi, i: (bi, ctx_layer, 0, 0))]
        args += [ck, cv]
    return _Mixer(functools.partial(_mla_attn_kernel, tq=tq, has_ctx=ck is not None), (b, n // tq), in_specs, args,
                  [pl.BlockSpec((None, tq, MLA_HEADS * MLA_V), lambda bi, i: (bi, i, 0))],
                  [jax.ShapeDtypeStruct((b, n, MLA_HEADS * MLA_V), BF16)],
                  [pltpu.VMEM((tq, nk + n_ctx), F32)])


def _pool_kernel(x_ref, w_ref, s_ref, o_ref, *, n):
    x = x_ref[...]
    t = lax.broadcasted_iota(jnp.int32, x.shape, 0)
    lane = lax.broadcasted_iota(jnp.int32, x.shape, 1)

    def later(a, d):
        return jnp.where(t < n - d, pltpu.roll(a, n - d, 0), 0.0)

    def earlier(a, d):
        return jnp.where(t >= d, pltpu.roll(a, d, 0), 0.0)

    fwd, bwd = {1: x}, {1: x}
    for k in (1, 2, 4):
        fwd[2 * k] = fwd[k] + later(fwd[k], k)
        bwd[2 * k] = bwd[k] + earlier(bwd[k], k)
    pooled = jnp.zeros_like(x)
    for gi, w in enumerate(POOL_WINDOWS):
        total = fwd[w // 2] + earlier(bwd[w // 2], 1)
        count = (jnp.clip(t + w // 2, 0, n) - jnp.clip(t - w // 2, 0, n)).astype(F32)
        in_group = (lane >= gi * POOL_GW) & (lane < (gi + 1) * POOL_GW)
        pooled = jnp.where(in_group, total / count - x, pooled)
    y = jnp.dot(pooled.astype(BF16), w_ref[...], preferred_element_type=F32)
    o_ref[...] = (y * s_ref[...]).astype(o_ref.dtype)


def _pool_once_kernel(*refs, n):
    @pl.when(pl.program_id(1) == 0)
    def _():
        _pool_kernel(*refs, n=n)


def _pool(x, wts, l, tiles=1):
    b, n, w = x.shape
    return _Mixer(functools.partial(_pool_kernel if tiles == 1 else _pool_once_kernel, n=n), (b, tiles),
                  [pl.BlockSpec((None, n, w), lambda bi, i: (bi, 0, 0)),
                   _layer_spec(wts["pool_w"], l, 2), _layer_spec(wts["pool_scale"], l, 2)],
                  [x, wts["pool_w"], wts["pool_scale"]],
                  [pl.BlockSpec((None, n, w), lambda bi, i: (bi, 0, 0))],
                  [jax.ShapeDtypeStruct((b, n, w), BF16)], [])


def _post_kernel(x_ref, p0, p1, p2, p3, mod_ref, n2_ref, wo_ref, w1_ref, w2_ref, o_ref):
    mod = mod_ref[...]
    g1 = mod[:, 2 * D_MODEL:3 * D_MODEL]
    sh2, sc2, g2 = (mod[:, 3 * D_MODEL:4 * D_MODEL], mod[:, 4 * D_MODEL:5 * D_MODEL],
                    mod[:, 5 * D_MODEL:6 * D_MODEL])
    for r0 in range(0, x_ref.shape[0], POST_CHAIN):
        rs = slice(r0, r0 + POST_CHAIN)
        x = x_ref[rs, :]
        a = jnp.zeros(x.shape, F32)
        for i, p in enumerate((p0, p1, p2, p3)):
            a = a + jnp.dot(p[rs, :], wo_ref[i * 256:(i + 1) * 256, :], preferred_element_type=F32)
        x = x + g1 * a
        h = (_rms_full(x, n2_ref[...]) * (1.0 + sc2) + sh2).astype(BF16)
        acc = jnp.zeros(x.shape, F32)
        for c in range(D_FF // FF_CHUNK):
            z = jnp.maximum(jnp.dot(h, w1_ref[:, c * FF_CHUNK:(c + 1) * FF_CHUNK], preferred_element_type=F32), 0.0)
            acc = acc + jnp.dot((z * z).astype(BF16), w2_ref[c * FF_CHUNK:(c + 1) * FF_CHUNK, :],
                                preferred_element_type=F32)
        o_ref[rs, :] = x + g2 * acc


def _post(x, parts, mod, l, wts, *, tokens_per_mod, first_mod_row, tile):
    t = x.shape[0]
    tiles_per_mod = tokens_per_mod // tile

    def tok(width):
        return pl.BlockSpec((tile, width), lambda bi, i: (bi, 0))

    def resident(arr):
        nd = arr.ndim
        return pl.BlockSpec((None,) + tuple(arr.shape[1:]), lambda bi, i: (l,) + (0,) * (nd - 1),
                            pipeline_mode=pl.Buffered(1))

    in_specs = [tok(D_MODEL), tok(256), tok(256), tok(256), tok(256),
                pl.BlockSpec((None, None, 1, 6 * D_MODEL),
                             lambda bi, i: (l, first_mod_row + bi // tiles_per_mod, 0, 0)),
                _layer_spec(wts["norm2"], l, 2), resident(wts["w_out"]), resident(wts["w1"]), resident(wts["w2"])]
    return _Mixer(_post_kernel, (t // tile, 1), in_specs,
                  [x, *parts, mod, wts["norm2"], wts["w_out"], wts["w1"], wts["w2"]],
                  [tok(D_MODEL)], [jax.ShapeDtypeStruct((t, D_MODEL), F32)], [])


def _block_ones(width, group):
    idx = np.arange(width) // group
    return jnp.asarray(idx[:, None] == idx[None, :], BF16)


def _rope_tables(n_tokens, head_lanes, first, rope_dims):
    half, quarter = rope_dims // 2, rope_dims // 4
    t = jnp.arange(n_tokens, dtype=jnp.int32)
    rows, cols = t // GRID_W, t % GRID_W
    freqs = ROPE_THETA ** (-jnp.arange(quarter, dtype=F32) / quarter)
    lane = np.arange(128) % head_lanes - first
    active = (lane >= 0) & (lane < rope_dims)
    lane = np.where(active, lane, 0)
    use_cols = lane >= half
    within = lane % half
    upper = within >= quarter
    f_idx = within % quarter
    pos = jnp.where(jnp.asarray(use_cols)[None, :], cols[:, None], rows[:, None]).astype(F32)
    ang = pos * freqs[jnp.asarray(f_idx)][None, :]
    act = jnp.asarray(active)[None, :]
    up = jnp.asarray(upper)[None, :]
    cos = jnp.where(act, jnp.cos(ang), 1.0)
    sin = jnp.where(act, jnp.sin(ang), 0.0)
    return cos, jnp.where(up, 0.0, -sin), jnp.where(up, sin, 0.0)


_SWA_Q_ORDER = (0, 2, 1, 3)
def _regroup_w_in_kernel(wt_ref, o_ref):
    src_kpe = _C_CKV + MLA_KV_RANK
    src_sq = src_kpe + MLA_ROPE

    def put(dst, src, width):
        o_ref[:, dst:dst + width] = wt_ref[src:src + width, :].T.astype(BF16)

    for c in range(0, src_kpe, 128):
        put(c, c, 128)
    for slot, head in enumerate(_SWA_Q_ORDER):
        put(_C_SWA_Q + slot * HEAD_DIM, src_sq + head * HEAD_DIM, HEAD_DIM)
    put(_C_SWA_K, src_sq + 256, 128)
    put(_C_SWA_V, src_sq + 384, 128)
    o_ref[:, _C_KPE_BLOCK:_C_KPE_BLOCK + MLA_PAD] = jnp.zeros((o_ref.shape[0], MLA_PAD), BF16)
    put(_C_KPE, src_kpe, MLA_ROPE)


def _regroup_w_in(w_in):
    d, rows, cols = w_in.shape
    return pl.pallas_call(
        _regroup_w_in_kernel, grid=(d, 1),
        in_specs=[pl.BlockSpec((None, cols, rows), lambda l, i: (l, 0, 0))],
        out_specs=pl.BlockSpec((None, rows, W_IN_COLS), lambda l, i: (l, 0, 0)),
        out_shape=jax.ShapeDtypeStruct((d, rows, W_IN_COLS), BF16),
        compiler_params=_params(2), name="regroup_w_in",
    )(jnp.swapaxes(w_in, 1, 2))


def _regroup_w_out_kernel(w_ref, o_ref):
    first = 3 * 256
    o_ref[0:first, :] = w_ref[0:first, :].astype(BF16)
    for slot, head in enumerate(_SWA_Q_ORDER):
        o_ref[first + slot * HEAD_DIM:first + (slot + 1) * HEAD_DIM, :] = (
            w_ref[first + head * HEAD_DIM:first + (head + 1) * HEAD_DIM, :].astype(BF16))


def _regroup_w_out(w_out):
    d, rows, cols = w_out.shape
    return pl.pallas_call(
        _regroup_w_out_kernel, grid=(d, 1),
        in_specs=[pl.BlockSpec((None, rows, cols), lambda l, i: (l, 0, 0))],
        out_specs=pl.BlockSpec((None, rows, cols), lambda l, i: (l, 0, 0)),
        out_shape=jax.ShapeDtypeStruct((d, rows, cols), BF16),
        compiler_params=_params(2), name="regroup_w_out",
    )(w_out)


def _prepare(norm1, norm2, w_in, w_out, na_q_norm, na_k_norm, pool_w, pool_scale, mla_q_a_norm, mla_w_uq,
             mla_kv_norm, mla_w_ukv, mla_q_norm, mla_k_norm, swa_q_norm, swa_k_norm, mlp_w1, mlp_w2):
    d = DEPTH

    def pad_heads(a, width):
        a = a.reshape(a.shape[:-1] + (MLA_HEADS, width))
        a = jnp.pad(a, [(0, 0)] * (a.ndim - 1) + [(0, MLA_PAD - width)])
        return a.reshape(a.shape[:-2] + (MLA_HEADS * MLA_PAD,))

    ukv = mla_w_ukv.reshape(d, MLA_KV_RANK, MLA_HEADS, MLA_NOPE + MLA_V)
    eye = jnp.eye(POOL_GROUPS, dtype=pool_w.dtype)
    pool_bd = jnp.einsum("lgce,gh->lgche", pool_w, eye).reshape(d, 256, 256)

    def gain(g, reps, scale=1.0):
        return (jnp.tile(g, (1, reps)) * scale).reshape(d, 1, -1)

    return {
        "norm1": norm1.reshape(d, 1, D_MODEL), "norm2": norm2.reshape(d, 1, D_MODEL),
        "w_in": _regroup_w_in(w_in),
        "w_uq": pad_heads(mla_w_uq, MLA_QK).astype(BF16),
        "w_uk": pad_heads(ukv[..., :MLA_NOPE].reshape(d, MLA_KV_RANK, -1), MLA_NOPE).astype(BF16),
        "w_uv": pad_heads(ukv[..., MLA_NOPE:].reshape(d, MLA_KV_RANK, -1), MLA_V).astype(BF16),
        "w_out": _regroup_w_out(w_out),
        "w1": mlp_w1.astype(BF16), "w2": mlp_w2.astype(BF16),
        "pool_w": pool_bd.astype(BF16), "pool_scale": pool_scale.reshape(d, 1, 256),
        "g_naq": gain(na_q_norm, NA_HEADS, HEAD_DIM ** -0.5 * LOG2E), "g_nak": gain(na_k_norm, NA_HEADS),
        "g_qa": mla_q_a_norm.reshape(d, 1, MLA_Q_RANK), "g_kv": mla_kv_norm.reshape(d, 1, MLA_KV_RANK),
        "g_mq": pad_heads(jnp.tile(mla_q_norm, (1, MLA_HEADS)) * (MLA_QK ** -0.5 * LOG2E), MLA_QK).reshape(d, 1, -1),
        "g_mk": pad_heads(jnp.tile(mla_k_norm, (1, MLA_HEADS)), MLA_QK).reshape(d, 1, -1),
        "g_sq": gain(swa_q_norm, SWA_HEADS, HEAD_DIM ** -0.5 * LOG2E), "g_sk": gain(swa_k_norm, SWA_KV_HEADS),
    }


def _constants():
    width = MLA_HEADS * MLA_PAD
    e = np.zeros((MLA_ROPE, width), np.float32)
    for h in range(MLA_HEADS):
        e[np.arange(MLA_ROPE), h * MLA_PAD + MLA_NOPE + np.arange(MLA_ROPE)] = 1.0
    v_ones = (np.arange(width) % MLA_PAD >= MLA_V).astype(np.float32)[None, :]
    return {"e_kpe": jnp.asarray(e, BF16), "v_ones": jnp.asarray(v_ones), "m256": _block_ones(256, HEAD_DIM),
            "m128": _block_ones(128, HEAD_DIM), "m512": _block_ones(width, MLA_PAD)}


def kernel(x_prompt, x_sample, cache_na_k, cache_na_v, cache_mla_ckv, cache_mla_kpe, cache_swa_k, cache_swa_v, c, c_ctx, norm1, norm2, w_ada, b_ada, w_in, w_out, na_q_norm, na_k_norm, na_rpb, pool_w, pool_scale, mla_q_a_norm, mla_w_uq, mla_kv_norm, mla_w_ukv, mla_q_norm, mla_k_norm, swa_q_norm, swa_k_norm, swa_sink, mlp_w1, mlp_w2):
    bc, nc, _ = x_prompt.shape
    bl, nl, _ = x_sample.shape
    n_past = cache_na_k.shape[2]
    wts = _prepare(norm1, norm2, w_in, w_out, na_q_norm, na_k_norm, pool_w, pool_scale, mla_q_a_norm, mla_w_uq,
                   mla_kv_norm, mla_w_ukv, mla_q_norm, mla_k_norm, swa_q_norm, swa_k_norm, mlp_w1, mlp_w2)
    consts = _constants()
    rope_tabs = _rope_tables(nl, MLA_PAD, MLA_NOPE, MLA_ROPE) + _rope_tables(nl, HEAD_DIM, 0, HEAD_DIM)

    cond = jnp.concatenate([c_ctx[None, :], c, jnp.zeros((8 - 1 - bl, D_MODEL), c.dtype)], axis=0)
    mod = _adaln(cond, w_ada, b_ada).reshape(DEPTH, 8, 1, 6 * D_MODEL)

    def cache(a):
        return jnp.transpose(a, (0, 1, 3, 4, 2)).reshape(bl, DEPTH, -1, n_past)

    c_nak, c_nav, c_sk, c_sv = cache(cache_na_k), cache(cache_na_v), cache(cache_swa_k), cache(cache_swa_v)
    c_mk, c_mv = _mla_cache(cache_mla_ckv, cache_mla_kpe, wts, consts)
    na_bias = _na_bias(na_rpb, nl // GRID_W)

    xp = x_prompt.reshape(bc * nc, D_MODEL)
    xs = x_sample.reshape(bl * nl, D_MODEL)
    states = None
    for l in range(DEPTH):
        sink = swa_sink[l]
        outs = _pre(xp, mod, l, wts, consts, None, tokens_per_mod=bc * nc, first_mod_row=0, state_seq=nc,
                    carried=states, name="pre_ctx")
        naq, nak, nav, pool_in, mq, mk, mv, sq, sk, sv = [o.reshape(bc, nc, -1) for o in outs[:10]]
        states = outs[10:]
        ctx_mixers = [_pool(pool_in, wts, l), _mla_attention(mq, mk, mv), _attention(naq, nak, nav, dl=HEAD_DIM),
                      _attention(sq, sk, sv, dl=HEAD_DIM, sink=sink)]
        outs = _pre(xs, mod, l, wts, consts, rope_tabs, tokens_per_mod=nl, first_mod_row=1, name="pre_lat")
        naq, nak, nav, pool_in, mq, mk, mv, sq, sk, sv = [o.reshape(bl, nl, -1) for o in outs]
        o_pool, o_mla, o_na, o_swa = _run_mixers(
            [_pool(pool_in, wts, l, tiles=nl // Q_TILE),
             _mla_attention(mq, mk, mv, ck=c_mk, cv=c_mv, ctx_layer=l),
             _attention(naq, nak, nav, dl=HEAD_DIM, mode="na", ck=c_nak, cv=c_nav, ctx_layer=l,
                        bias=na_bias, bias_layer=l),
             _attention(sq, sk, sv, dl=HEAD_DIM, mode="swa", ck=c_sk, cv=c_sv, ctx_layer=l, sink=sink)],
            "lat_mix")
        c_pool, c_mla, c_na, c_swa, xs = _run_mixers(
            ctx_mixers + [_post(xs, [p.reshape(bl * nl, 256) for p in (o_na, o_pool, o_mla, o_swa)], mod, l, wts,
                                tokens_per_mod=nl, first_mod_row=1, tile=bl * nl // bc)], "ctx_mix_post_lat")
        (xp,) = _run_mixers(
            [_post(xp, [p.reshape(bc * nc, 256) for p in (c_na, c_pool, c_mla, c_swa)], mod, l, wts,
                   tokens_per_mod=bc * nc, first_mod_row=0, tile=POST_TILE)], "post_ctx")

    def heads(st, n_heads):
        return jnp.transpose(st.reshape(bc, DEPTH, n_heads, HEAD_DIM, nc), (0, 1, 4, 2, 3))

    st_nak, st_nav, st_ckv, st_kpe, st_sk, st_sv = states
    return (xp.reshape(bc, nc, D_MODEL), xs.reshape(bl, nl, D_MODEL),
            heads(st_nak, NA_HEADS), heads(st_nav, NA_HEADS),
            st_ckv, jnp.swapaxes(st_kpe, 2, 3),
            heads(st_sk, SWA_KV_HEADS), heads(st_sv, SWA_KV_HEADS))
```

```python
import functools

import numpy as np
import jax
import jax.numpy as jnp
from jax import lax
from jax.experimental import pallas as pl
from jax.experimental.pallas import tpu as pltpu

F32, BF16 = jnp.float32, jnp.bfloat16

D_MODEL = 1024
DEPTH = 4
GRID_W = 64
HEAD_DIM = 64
NA_HEADS = 4
NA_WIN_ROWS = 8
NA_WIN_COLS = 16
POOL_GROUPS = 4
POOL_GW = 64
POOL_WINDOWS = (2, 4, 8, 16)
MLA_HEADS = 4
MLA_Q_RANK = 256
MLA_KV_RANK = 128
MLA_NOPE = 64
MLA_ROPE = 32
MLA_V = 64
MLA_QK = MLA_NOPE + MLA_ROPE
MLA_PAD = 128
SWA_HEADS = 4
SWA_KV_HEADS = 2
SWA_WINDOW = 128
D_FF = 4 * D_MODEL
ROPE_THETA = 10000.0
EPS = 1e-6
NEG = -1e30
LOG2E = 1.4426950408889634

POST_TILE = 1024
POST_CHAIN = 256
PRE_TILE = 512
PRE_CHAIN = 128
Q_TILE = 256
KEY_BLOCK = 256
NA_TILE_ROWS = Q_TILE // GRID_W
NA_KEY_ROWS = NA_TILE_ROWS + NA_WIN_ROWS
SWA_KEYS = Q_TILE + 2 * SWA_WINDOW
ADA_COLS = 2048
FF_CHUNK = 1024
W_IN_COLS = 2048
VMEM_LIMIT = 48 * 1024 * 1024

_C_NA_Q, _C_NA_K, _C_NA_V, _C_POOL, _C_CQ, _C_CKV, _C_SWA_Q, _C_SWA_K, _C_SWA_V, _C_KPE_BLOCK = (
    0, 256, 512, 768, 1024, 1280, 1408, 1664, 1792, 1920)
_C_KPE = _C_KPE_BLOCK + MLA_NOPE


def _params(n_grid):
    return pltpu.CompilerParams(dimension_semantics=("arbitrary",) * n_grid, vmem_limit_bytes=VMEM_LIMIT)


def _layer_spec(arr, l, n_grid):
    nd = arr.ndim
    shape = (None,) + tuple(arr.shape[1:])
    if n_grid == 1:
        return pl.BlockSpec(shape, lambda i: (l,) + (0,) * (nd - 1))
    return pl.BlockSpec(shape, lambda b, i: (l,) + (0,) * (nd - 1))


def _const_spec(arr, n_grid):
    nd = arr.ndim
    if n_grid == 1:
        return pl.BlockSpec(arr.shape, lambda i: (0,) * nd)
    return pl.BlockSpec(arr.shape, lambda b, i: (0,) * nd)


def _rms_full(x, g):
    return x * lax.rsqrt(jnp.mean(x * x, axis=-1, keepdims=True) + EPS) * g


def _split_dot(s, m):
    hi = s.astype(BF16)
    lo = (s - hi.astype(F32)).astype(BF16)
    return jnp.dot(hi, m, preferred_element_type=F32) + jnp.dot(lo, m, preferred_element_type=F32)


def _rms_heads(x, g, m, dim):
    ss = jnp.dot((x * x).astype(BF16), m, preferred_element_type=F32)
    return x * lax.rsqrt(ss * (1.0 / dim) + EPS) * g


def _rms_lanes(x, g, dim):
    ss = jnp.sum(x * x, axis=-1, keepdims=True)
    return x * lax.rsqrt(ss * (1.0 / dim) + EPS) * g


def _rope(x, c_ref, sa_ref, sb_ref, quarter, rows):
    return (x * c_ref[rows, :] + pltpu.roll(x, 128 - quarter, 1) * sa_ref[rows, :]
            + pltpu.roll(x, quarter, 1) * sb_ref[rows, :])


def _adaln_kernel(cond_ref, w_ref, b_ref, o_ref):
    s = jax.nn.silu(cond_ref[...]).astype(BF16)
    o_ref[...] = jnp.dot(s, w_ref[...].astype(BF16), preferred_element_type=F32) + b_ref[...]


def _adaln(cond, w_ada, b_ada):
    n_out = w_ada.shape[-1]
    return pl.pallas_call(
        _adaln_kernel,
        grid=(DEPTH, n_out // ADA_COLS),
        in_specs=[pl.BlockSpec((8, D_MODEL), lambda l, j: (0, 0)),
                  pl.BlockSpec((None, D_MODEL, ADA_COLS), lambda l, j: (l, 0, j)),
                  pl.BlockSpec((None, 1, ADA_COLS), lambda l, j: (l, 0, j))],
        out_specs=pl.BlockSpec((None, 8, ADA_COLS), lambda l, j: (l, 0, j)),
        out_shape=jax.ShapeDtypeStruct((DEPTH, 8, n_out), F32),
        compiler_params=_params(2),
        name="adaln",
    )(cond, w_ada, b_ada.reshape(DEPTH, 1, n_out))


def _pre_kernel(*refs, rope, state_seq, n_carried, state_layer):
    (x_ref, mod_ref, n1_ref, win_ref, wuq_ref, wuk_ref, wuv_ref, ones_ref, m256_ref, m128_ref,
     g_naq, g_nak, g_qa, g_kv, g_mq, g_mk, g_sq, g_sk) = refs[:18]
    pos = 18
    if rope:
        cm, sam, sbm, cs, sas, sbs = refs[pos:pos + 6]
        pos += 6
    pos += n_carried
    (o_naq, o_nak, o_nav, o_pool, o_mq, o_mk, o_mv, o_sq, o_sk, o_sv) = refs[pos:pos + 10]
    pos += 10
    if state_seq:
        state_refs = refs[pos:pos + 6]

    mod = mod_ref[...]
    sh1 = mod[:, 0:D_MODEL]
    gain1 = n1_ref[...] * (1.0 + mod[:, D_MODEL:2 * D_MODEL])
    m256 = m256_ref[...]
    for r0 in range(0, x_ref.shape[0], PRE_CHAIN):
        rs = slice(r0, r0 + PRE_CHAIN)
        x = x_ref[rs, :]
        h = x * lax.rsqrt(jnp.mean(x * x, axis=-1, keepdims=True) + EPS) * gain1 + sh1
        y = jnp.dot(h.astype(BF16), win_ref[...], preferred_element_type=F32)

        na_k = _rms_heads(y[:, _C_NA_K:_C_NA_K + 256], g_nak[...], m256, HEAD_DIM)
        na_v = y[:, _C_NA_V:_C_NA_V + 256]
        o_naq[rs, :] = _rms_heads(y[:, _C_NA_Q:_C_NA_Q + 256], g_naq[...], m256, HEAD_DIM).astype(BF16)
        o_nak[rs, :] = na_k.astype(BF16)
        o_nav[rs, :] = na_v.astype(BF16)
        o_pool[rs, :] = y[:, _C_POOL:_C_POOL + 256]

        cq = _rms_full(y[:, _C_CQ:_C_CQ + MLA_Q_RANK], g_qa[...])
        mq = jnp.dot(cq.astype(BF16), wuq_ref[...], preferred_element_type=F32)
        ckv = _rms_full(y[:, _C_CKV:_C_CKV + MLA_KV_RANK], g_kv[...])
        ckv_b = ckv.astype(BF16)
        mk = jnp.dot(ckv_b, wuk_ref[...], preferred_element_type=F32)
        kpe_block = y[:, _C_KPE_BLOCK:_C_KPE_BLOCK + MLA_PAD]
        o_mv[rs, :] = (jnp.dot(ckv_b, wuv_ref[...], preferred_element_type=F32) + ones_ref[...]).astype(BF16)
        for hd in range(MLA_HEADS):
            cols = slice(hd * MLA_PAD, (hd + 1) * MLA_PAD)
            qh = _rms_lanes(mq[:, cols], g_mq[:, cols], MLA_QK)
            kh = _rms_lanes(mk[:, cols] + kpe_block, g_mk[:, cols], MLA_QK)
            if rope:
                qh = _rope(qh, cm, sam, sbm, MLA_ROPE // 4, rs)
                kh = _rope(kh, cm, sam, sbm, MLA_ROPE // 4, rs)
            o_mq[rs, cols] = qh.astype(BF16)
            o_mk[rs, cols] = kh.astype(BF16)

        sq = _rms_heads(y[:, _C_SWA_Q:_C_SWA_Q + 256], g_sq[...], m256, HEAD_DIM)
        sk = _rms_heads(y[:, _C_SWA_K:_C_SWA_K + 128], g_sk[...], m128_ref[...], HEAD_DIM)
        sv = y[:, _C_SWA_V:_C_SWA_V + 128]
        if state_seq:
            seq, s0 = r0 // state_seq, r0 % state_seq
            vals = (na_k, na_v, ckv, y[:, _C_KPE:_C_KPE + MLA_ROPE], sk, sv)
            for ref, val, tokens_minor in zip(state_refs, vals, _STATE_TOKENS_MINOR):
                if tokens_minor:
                    val, at = val.T, (slice(None), slice(s0, s0 + PRE_CHAIN))
                else:
                    at = (slice(s0, s0 + PRE_CHAIN), slice(None))
                if state_layer is None:
                    ref[(seq,) + at] = val
                else:
                    for k in range(DEPTH):
                        ref[(seq, k) + at] = val if k == state_layer else jnp.zeros_like(val)
        if rope:
            for blk in range(2):
                cols = slice(blk * 128, (blk + 1) * 128)
                o_sq[rs, cols] = _rope(sq[:, cols], cs, sas, sbs, HEAD_DIM // 4, rs).astype(BF16)
            sk = _rope(sk, cs, sas, sbs, HEAD_DIM // 4, rs)
        else:
            o_sq[rs, :] = sq.astype(BF16)
        o_sk[rs, :] = sk.astype(BF16)
        o_sv[rs, :] = sv.astype(BF16)


_STATE_WIDTHS = (NA_HEADS * HEAD_DIM, NA_HEADS * HEAD_DIM, MLA_KV_RANK, MLA_ROPE,
                 SWA_KV_HEADS * HEAD_DIM, SWA_KV_HEADS * HEAD_DIM)
_STATE_TOKENS_MINOR = (True, True, False, True, True, True)


def _pre(x, mod, l, wts, consts, rope_tabs, *, tokens_per_mod, first_mod_row, name, state_seq=0, carried=None):
    t = x.shape[0]
    tm = PRE_TILE
    rope = rope_tabs is not None
    tiles_per_mod = tokens_per_mod // tm
    n_pos = rope_tabs[0].shape[0] // tm if rope else 1

    def row(width, dtype):
        return jax.ShapeDtypeStruct((t, width), dtype)

    def tok(width):
        return pl.BlockSpec((tm, width), lambda i: (i, 0))

    in_specs = [tok(D_MODEL),
                pl.BlockSpec((None, None, 1, 6 * D_MODEL), lambda i: (l, first_mod_row + i // tiles_per_mod, 0, 0)),
                _layer_spec(wts["norm1"], l, 1), _layer_spec(wts["w_in"], l, 1), _layer_spec(wts["w_uq"], l, 1),
                _layer_spec(wts["w_uk"], l, 1), _layer_spec(wts["w_uv"], l, 1),
                _const_spec(consts["v_ones"], 1), _const_spec(consts["m256"], 1), _const_spec(consts["m128"], 1)]
    args = [x, mod, wts["norm1"], wts["w_in"], wts["w_uq"], wts["w_uk"], wts["w_uv"],
            consts["v_ones"], consts["m256"], consts["m128"]]
    for nm in ("g_naq", "g_nak", "g_qa", "g_kv", "g_mq", "g_mk", "g_sq", "g_sk"):
        in_specs.append(_layer_spec(wts[nm], l, 1))
        args.append(wts[nm])
    if rope:
        for tab in rope_tabs:
            in_specs.append(pl.BlockSpec((tm, tab.shape[1]), lambda i: (i % n_pos, 0)))
            args.append(tab)
    widths = (256, 256, 256, 256, 512, 512, 512, 256, 128, 128)
    dtypes = (BF16, BF16, BF16, F32, BF16, BF16, BF16, BF16, BF16, BF16)
    out_shape = [row(w, d) for w, d in zip(widths, dtypes)]
    out_specs = [tok(w) for w in widths]
    aliases = {}
    n_carried = 0
    if state_seq:
        seqs_per_tile = tm // state_seq
        if carried is not None:
            n_carried = len(carried)
            for k, buf in enumerate(carried):
                aliases[len(args)] = len(out_shape) + k
                in_specs.append(pl.BlockSpec(memory_space=pl.ANY))
                args.append(buf)
        for w, tokens_minor in zip(_STATE_WIDTHS, _STATE_TOKENS_MINOR):
            tail = (w, state_seq) if tokens_minor else (state_seq, w)
            out_shape.append(jax.ShapeDtypeStruct((t // state_seq, DEPTH) + tail, F32))
            if carried is None:
                out_specs.append(pl.BlockSpec((seqs_per_tile, DEPTH) + tail, lambda i: (i, 0, 0, 0)))
            else:
                out_specs.append(pl.BlockSpec((seqs_per_tile, None) + tail, lambda i: (i, l, 0, 0)))
    return pl.pallas_call(
        functools.partial(_pre_kernel, rope=rope, state_seq=state_seq, n_carried=n_carried,
                          state_layer=l if state_seq and carried is None else None),
        grid=(t // tm,), in_specs=in_specs, out_specs=out_specs, out_shape=out_shape,
        input_output_aliases=aliases, compiler_params=_params(1), name=name,
    )(*args)


def _mla_cache_kernel(ckv_ref, kpe_ref, wuk_ref, wuv_ref, e_ref, ones_ref, m512_ref, g_mk, ok_ref, ov_ref):
    ckv_b = ckv_ref[...].astype(BF16)
    mk = jnp.dot(ckv_b, wuk_ref[...], preferred_element_type=F32) + _split_dot(kpe_ref[...], e_ref[...])
    ok_ref[...] = _rms_heads(mk, g_mk[...], m512_ref[...], MLA_QK).astype(BF16)
    ov_ref[...] = (jnp.dot(ckv_b, wuv_ref[...], preferred_element_type=F32) + ones_ref[...]).astype(BF16)


def _mla_cache(c_ckv, c_kpe, wts, consts):
    b, _, n, _ = c_ckv.shape
    width = MLA_HEADS * MLA_PAD

    def lay(arr):
        nd = arr.ndim
        return pl.BlockSpec((None,) + tuple(arr.shape[1:]), lambda l, i: (l,) + (0,) * (nd - 1))

    return pl.pallas_call(
        _mla_cache_kernel,
        grid=(DEPTH, b),
        in_specs=[pl.BlockSpec((None, None, n, MLA_KV_RANK), lambda l, i: (i, l, 0, 0)),
                  pl.BlockSpec((None, None, n, MLA_ROPE), lambda l, i: (i, l, 0, 0)),
                  lay(wts["w_uk"]), lay(wts["w_uv"]), _const_spec(consts["e_kpe"], 2),
                  _const_spec(consts["v_ones"], 2), _const_spec(consts["m512"], 2), lay(wts["g_mk"])],
        out_specs=[pl.BlockSpec((None, None, n, width), lambda l, i: (i, l, 0, 0)),
                   pl.BlockSpec((None, None, n, width), lambda l, i: (i, l, 0, 0))],
        out_shape=[jax.ShapeDtypeStruct((b, DEPTH, n, width), BF16),
                   jax.ShapeDtypeStruct((b, DEPTH, n, width), BF16)],
        compiler_params=_params(2), name="mla_cache",
    )(c_ckv, c_kpe, wts["w_uk"], wts["w_uv"], consts["e_kpe"], consts["v_ones"], consts["m512"], wts["g_mk"])


_NA_VARIANTS = ((0, 0), (4, 0), (24, 20), (28, 20))


def _na_bias_kernel(rpb_ref, o_ref, *, grid_rows):
    l, hd = pl.program_id(0), pl.program_id(1)
    n_dr, n_dc = 2 * NA_WIN_ROWS - 1, 2 * NA_WIN_COLS - 1
    base = (l * NA_HEADS + hd) * (n_dr * n_dc)
    c = lax.broadcasted_iota(jnp.int32, (GRID_W, GRID_W), 0)
    kc = lax.broadcasted_iota(jnp.int32, (GRID_W, GRID_W), 1)
    dcv = kc - c + (NA_WIN_COLS - 1)
    col_start = jnp.clip(c - NA_WIN_COLS // 2, 0, GRID_W - NA_WIN_COLS)
    col_ok = (kc >= col_start) & (kc < col_start + NA_WIN_COLS)
    neg = jnp.full((GRID_W, GRID_W), NEG, F32)
    blocks = {}

    def block(dr):
        if dr not in blocks:
            t = jnp.zeros((GRID_W, GRID_W), F32)
            for dc in range(n_dc):
                t = jnp.where(dcv == dc, rpb_ref[base + dr * n_dc + dc] * LOG2E, t)
            blocks[dr] = jnp.where(col_ok, t, NEG)
        return blocks[dr]

    for v, (r0, w0) in enumerate(_NA_VARIANTS):
        for rl in range(NA_TILE_ROWS):
            r = r0 + rl
            kr0 = min(max(r - NA_WIN_ROWS // 2, 0), grid_rows - NA_WIN_ROWS)
            for j in range(NA_KEY_ROWS):
                kr = w0 + j
                inside = kr0 <= kr < kr0 + NA_WIN_ROWS
                o_ref[v, rl * GRID_W:(rl + 1) * GRID_W, j * GRID_W:(j + 1) * GRID_W] = (
                    block(kr - r + NA_WIN_ROWS - 1) if inside else neg)


def _na_bias(na_rpb, grid_rows):
    n_k = NA_KEY_ROWS * GRID_W
    return pl.pallas_call(
        functools.partial(_na_bias_kernel, grid_rows=grid_rows),
        grid=(DEPTH, NA_HEADS),
        in_specs=[pl.BlockSpec(memory_space=pltpu.SMEM)],
        out_specs=pl.BlockSpec((None, len(_NA_VARIANTS), None, Q_TILE, n_k), lambda l, h: (l, 0, h, 0, 0)),
        out_shape=jax.ShapeDtypeStruct((DEPTH, len(_NA_VARIANTS), NA_HEADS, Q_TILE, n_k), F32),
        compiler_params=_params(2), name="na_bias",
    )(na_rpb.reshape(-1))


def _attn_kernel(*refs, tq, wk, wv, dl, n_slabs, mode, has_ctx, has_sink, n_keys):
    q_ref, k_ref, v_ref = refs[:3]
    pos = 3
    if has_ctx:
        ck_ref, cv_ref = refs[pos:pos + 2]
        pos += 2
    if mode == "na":
        bias_ref = refs[pos]
        pos += 1
    if has_sink:
        sink_ref = refs[pos]
        pos += 1
    o_ref = refs[pos]
    i = pl.program_id(1)
    hs = wk // dl

    valid = None
    if mode == "full":
        kl, vl = k_ref[...], v_ref[...]
    elif mode == "swa":
        w0 = pl.multiple_of(jnp.clip(i * tq - SWA_WINDOW, 0, n_keys - SWA_KEYS), SWA_WINDOW)
        kl, vl = k_ref[pl.ds(w0, SWA_KEYS), :], v_ref[pl.ds(w0, SWA_KEYS), :]
        qpos = i * tq + lax.broadcasted_iota(jnp.int32, (tq, SWA_KEYS), 0)
        kpos = w0 + lax.broadcasted_iota(jnp.int32, (tq, SWA_KEYS), 1)
        valid = jnp.abs(kpos - qpos) <= SWA_WINDOW
    else:
        last = n_keys // GRID_W - NA_KEY_ROWS
        w0 = pl.multiple_of(jnp.clip(i * NA_TILE_ROWS - NA_WIN_ROWS // 2, 0, last) * GRID_W, GRID_W)
        kl, vl = k_ref[pl.ds(w0, NA_KEY_ROWS * GRID_W), :], v_ref[pl.ds(w0, NA_KEY_ROWS * GRID_W), :]
    if has_ctx:
        ck_t, cv_t = ck_ref[...].astype(BF16), cv_ref[...].astype(BF16)

    lane_k = lax.broadcasted_iota(jnp.int32, (1, wk), 1)
    lane_v = lax.broadcasted_iota(jnp.int32, (1, wv), 1)
    nt = (((1,), (1,)), ((), ()))
    for s in range(n_slabs):
        qs = q_ref[:, s * wk:(s + 1) * wk]
        acc = jnp.zeros((tq, wv), F32)
        for j in range(hs):
            head = j * n_slabs + s
            qm = jnp.where((lane_k >= j * dl) & (lane_k < (j + 1) * dl), qs, jnp.zeros_like(qs))
            s_loc = lax.dot_general(qm, kl, nt, preferred_element_type=F32)
            if mode == "na":
                s_loc = s_loc + bias_ref[head]
            if valid is not None:
                s_loc = jnp.where(valid, s_loc, NEG)
            m = jnp.max(s_loc, axis=-1, keepdims=True)
            if has_ctx:
                s_ctx = jnp.dot(qm, ck_t, preferred_element_type=F32)
                m = jnp.maximum(m, jnp.max(s_ctx, axis=-1, keepdims=True))
            if has_sink:
                sink = sink_ref[head] * LOG2E
                m = jnp.maximum(m, sink)
            p_loc = jnp.exp2(s_loc - m)
            den = jnp.sum(p_loc, axis=-1, keepdims=True)
            o = jnp.dot(p_loc.astype(BF16), vl, preferred_element_type=F32)
            if has_ctx:
                p_ctx = jnp.exp2(s_ctx - m)
                den = den + jnp.sum(p_ctx, axis=-1, keepdims=True)
                o = o + lax.dot_general(p_ctx.astype(BF16), cv_t, nt, preferred_element_type=F32)
            if has_sink:
                den = den + jnp.exp2(sink - m)
            out_lanes = (lane_v >= j * HEAD_DIM) & (lane_v < (j + 1) * HEAD_DIM)
            acc = acc + jnp.where(out_lanes, o * (1.0 / den), 0.0)
        o_ref[:, s * wv:(s + 1) * wv] = acc.astype(o_ref.dtype)


class _Mixer:
    def __init__(self, body, grid, in_specs, args, out_specs, out_shape, scratch):
        self.body, self.grid = body, grid
        self.in_specs, self.args = in_specs, args
        self.out_specs, self.out_shape, self.scratch = out_specs, out_shape, scratch


def _mix_kernel(*refs, bodies, n_in, n_out, n_scratch):
    ins, outs, scr = refs[:sum(n_in)], refs[sum(n_in):sum(n_in) + sum(n_out)], refs[sum(n_in) + sum(n_out):]
    i0 = o0 = s0 = 0
    for body, ni, no, ns in zip(bodies, n_in, n_out, n_scratch):
        body(*ins[i0:i0 + ni], *outs[o0:o0 + no], *scr[s0:s0 + ns])
        i0, o0, s0 = i0 + ni, o0 + no, s0 + ns


def _run_mixers(mixers, name):
    grid = mixers[0].grid
    assert all(m.grid == grid for m in mixers)
    return pl.pallas_call(
        functools.partial(_mix_kernel, bodies=[m.body for m in mixers], n_in=[len(m.args) for m in mixers],
                          n_out=[len(m.out_shape) for m in mixers], n_scratch=[len(m.scratch) for m in mixers]),
        grid=grid,
        in_specs=[s for m in mixers for s in m.in_specs],
        out_specs=[s for m in mixers for s in m.out_specs],
        out_shape=[s for m in mixers for s in m.out_shape],
        scratch_shapes=[s for m in mixers for s in m.scratch],
        compiler_params=_params(2), name=name,
    )(*[a for m in mixers for a in m.args])


def _attention(q, k, v, *, dl, mode="full", ck=None, cv=None, ctx_layer=None, bias=None, bias_layer=None,
               sink=None):
    b, n, wq = q.shape
    _, nk, wk = k.shape
    wv = v.shape[-1]
    n_slabs = wq // wk
    tq = min(Q_TILE, n)
    in_specs = [pl.BlockSpec((None, tq, wq), lambda bi, i: (bi, i, 0)),
                pl.BlockSpec((None, nk, wk), lambda bi, i: (bi, 0, 0)),
                pl.BlockSpec((None, nk, wv), lambda bi, i: (bi, 0, 0))]
    args = [q, k, v]
    if ck is not None:
        n_ctx = ck.shape[3]
        in_specs += [pl.BlockSpec((None, None, wk, n_ctx), lambda bi, i: (bi, ctx_layer, 0, 0)),
                     pl.BlockSpec((None, None, wv, n_ctx), lambda bi, i: (bi, ctx_layer, 0, 0))]
        args += [ck, cv]
    if mode == "na":
        n_tiles = n // tq

        def variant(bi, i):
            v_idx = jnp.where(i == 0, 0, jnp.where(i <= n_tiles - 3, 1, i - (n_tiles - 4)))
            return (bias_layer, v_idx, 0, 0, 0)

        in_specs.append(pl.BlockSpec((None, None) + tuple(bias.shape[2:]), variant))
        args.append(bias)
    if sink is not None:
        in_specs.append(pl.BlockSpec(memory_space=pltpu.SMEM))
        args.append(sink)
    body = functools.partial(_attn_kernel, tq=tq, wk=wk, wv=wv, dl=dl, n_slabs=n_slabs, mode=mode,
                             has_ctx=ck is not None, has_sink=sink is not None, n_keys=nk)
    return _Mixer(body, (b, n // tq), in_specs, args,
                  [pl.BlockSpec((None, tq, n_slabs * wv), lambda bi, i: (bi, i, 0))],
                  [jax.ShapeDtypeStruct((b, n, n_slabs * wv), BF16)], [])


def _mla_attn_kernel(*refs, tq, has_ctx):
    q_ref, k_ref, v_ref = refs[:3]
    pos = 3
    if has_ctx:
        ck_ref, cv_ref = refs[pos:pos + 2]
        pos += 2
    o_ref, s_ref = refs[pos], refs[pos + 1]
    blocks = [(k_ref, v_ref, off) for off in range(0, k_ref.shape[0], KEY_BLOCK)]
    if has_ctx:
        blocks += [(ck_ref, cv_ref, off) for off in range(0, ck_ref.shape[0], KEY_BLOCK)]
    nt = (((1,), (1,)), ((), ()))
    lane = lax.broadcasted_iota(jnp.int32, (1, MLA_PAD), 1)
    outs = []
    for hd in range(MLA_HEADS):
        cols = slice(hd * MLA_PAD, (hd + 1) * MLA_PAD)
        qh = q_ref[:, cols]
        m_part = jnp.full((tq, 128), NEG, F32)
        for bi, (kr, _, off) in enumerate(blocks):
            sb = lax.dot_general(qh, kr[off:off + KEY_BLOCK, cols], nt, preferred_element_type=F32)
            s_ref[:, bi * KEY_BLOCK:(bi + 1) * KEY_BLOCK] = sb
            m_part = jnp.maximum(m_part, jnp.maximum(sb[:, :128], sb[:, 128:]))
        m = jnp.max(m_part, axis=-1, keepdims=True)
        o = jnp.zeros((tq, MLA_PAD), F32)
        for bi, (_, vr, off) in enumerate(blocks):
            p = jnp.exp2(s_ref[:, bi * KEY_BLOCK:(bi + 1) * KEY_BLOCK] - m)
            o = o + jnp.dot(p.astype(BF16), vr[off:off + KEY_BLOCK, cols], preferred_element_type=F32)
        outs.append(o * pltpu.roll(1.0 / o, MLA_V, 1))
    for hd in range(0, MLA_HEADS, 2):
        pair = jnp.where(lane < MLA_V, outs[hd], pltpu.roll(outs[hd + 1], MLA_V, 1))
        o_ref[:, (hd // 2) * 128:(hd // 2 + 1) * 128] = pair.astype(o_ref.dtype)


def _mla_attention(q, k, v, *, ck=None, cv=None, ctx_layer=None):
    b, n, w = q.shape
    nk = k.shape[1]
    tq = min(Q_TILE, n)
    in_specs = [pl.BlockSpec((None, tq, w), lambda bi, i: (bi, i, 0)),
                pl.BlockSpec((None, nk, w), lambda bi, i: (bi, 0, 0)),
                pl.BlockSpec((None, nk, w), lambda bi, i: (bi, 0, 0))]
    args = [q, k, v]
    n_ctx = 0
    if ck is not None:
        n_ctx = ck.shape[2]
        in_specs += [pl.BlockSpec((None, None, n_ctx, w), lambda bi, i: (bi, ctx_layer, 0, 0)),
                     pl.BlockSpec((None, None, n_ctx, w), lambda bi, i: (bi, ctx_layer, 0, 0))]
        args += [ck, cv]
    return _Mixer(functools.partial(_mla_attn_kernel, tq=tq, has_ctx=ck is not None), (b, n // tq), in_specs, args,
                  [pl.BlockSpec((None, tq, MLA_HEADS * MLA_V), lambda bi, i: (bi, i, 0))],
                  [jax.ShapeDtypeStruct((b, n, MLA_HEADS * MLA_V), BF16)],
                  [pltpu.VMEM((tq, nk + n_ctx), F32)])


def _pool_kernel(x_ref, w_ref, s_ref, o_ref, *, n):
    x = x_ref[...]
    t = lax.broadcasted_iota(jnp.int32, x.shape, 0)
    lane = lax.broadcasted_iota(jnp.int32, x.shape, 1)

    def later(a, d):
        return jnp.where(t < n - d, pltpu.roll(a, n - d, 0), 0.0)

    def earlier(a, d):
        return jnp.where(t >= d, pltpu.roll(a, d, 0), 0.0)

    fwd, bwd = {1: x}, {1: x}
    for k in (1, 2, 4):
        fwd[2 * k] = fwd[k] + later(fwd[k], k)
        bwd[2 * k] = bwd[k] + earlier(bwd[k], k)
    pooled = jnp.zeros_like(x)
    for gi, w in enumerate(POOL_WINDOWS):
        total = fwd[w // 2] + earlier(bwd[w // 2], 1)
        count = (jnp.clip(t + w // 2, 0, n) - jnp.clip(t - w // 2, 0, n)).astype(F32)
        in_group = (lane >= gi * POOL_GW) & (lane < (gi + 1) * POOL_GW)
        pooled = jnp.where(in_group, total / count - x, pooled)
    y = jnp.dot(pooled.astype(BF16), w_ref[...], preferred_element_type=F32)
    o_ref[...] = (y * s_ref[...]).astype(o_ref.dtype)


def _pool_once_kernel(*refs, n):
    @pl.when(pl.program_id(1) == 0)
    def _():
        _pool_kernel(*refs, n=n)


def _pool(x, wts, l, tiles=1):
    b, n, w = x.shape
    return _Mixer(functools.partial(_pool_kernel if tiles == 1 else _pool_once_kernel, n=n), (b, tiles),
                  [pl.BlockSpec((None, n, w), lambda bi, i: (bi, 0, 0)),
                   _layer_spec(wts["pool_w"], l, 2), _layer_spec(wts["pool_scale"], l, 2)],
                  [x, wts["pool_w"], wts["pool_scale"]],
                  [pl.BlockSpec((None, n, w), lambda bi, i: (bi, 0, 0))],
                  [jax.ShapeDtypeStruct((b, n, w), BF16)], [])


def _post_kernel(x_ref, p0, p1, p2, p3, mod_ref, n2_ref, wo_ref, w1_ref, w2_ref, o_ref):
    mod = mod_ref[...]
    g1 = mod[:, 2 * D_MODEL:3 * D_MODEL]
    sh2, sc2, g2 = (mod[:, 3 * D_MODEL:4 * D_MODEL], mod[:, 4 * D_MODEL:5 * D_MODEL],
                    mod[:, 5 * D_MODEL:6 * D_MODEL])
    for r0 in range(0, x_ref.shape[0], POST_CHAIN):
        rs = slice(r0, r0 + POST_CHAIN)
        x = x_ref[rs, :]
        a = jnp.zeros(x.shape, F32)
        for i, p in enumerate((p0, p1, p2, p3)):
            a = a + jnp.dot(p[rs, :], wo_ref[i * 256:(i + 1) * 256, :], preferred_element_type=F32)
        x = x + g1 * a
        h = (_rms_full(x, n2_ref[...]) * (1.0 + sc2) + sh2).astype(BF16)
        acc = jnp.zeros(x.shape, F32)
        for c in range(D_FF // FF_CHUNK):
            z = jnp.maximum(jnp.dot(h, w1_ref[:, c * FF_CHUNK:(c + 1) * FF_CHUNK], preferred_element_type=F32), 0.0)
            acc = acc + jnp.dot((z * z).astype(BF16), w2_ref[c * FF_CHUNK:(c + 1) * FF_CHUNK, :],
                                preferred_element_type=F32)
        o_ref[rs, :] = x + g2 * acc


def _post(x, parts, mod, l, wts, *, tokens_per_mod, first_mod_row, tile):
    t = x.shape[0]
    tiles_per_mod = tokens_per_mod // tile

    def tok(width):
        return pl.BlockSpec((tile, width), lambda bi, i: (bi, 0))

    def resident(arr):
        nd = arr.ndim
        return pl.BlockSpec((None,) + tuple(arr.shape[1:]), lambda bi, i: (l,) + (0,) * (nd - 1),
                            pipeline_mode=pl.Buffered(1))

    in_specs = [tok(D_MODEL), tok(256), tok(256), tok(256), tok(256),
                pl.BlockSpec((None, None, 1, 6 * D_MODEL),
                             lambda bi, i: (l, first_mod_row + bi // tiles_per_mod, 0, 0)),
                _layer_spec(wts["norm2"], l, 2), resident(wts["w_out"]), resident(wts["w1"]), resident(wts["w2"])]
    return _Mixer(_post_kernel, (t // tile, 1), in_specs,
                  [x, *parts, mod, wts["norm2"], wts["w_out"], wts["w1"], wts["w2"]],
                  [tok(D_MODEL)], [jax.ShapeDtypeStruct((t, D_MODEL), F32)], [])


def _block_ones(width, group):
    idx = np.arange(width) // group
    return jnp.asarray(idx[:, None] == idx[None, :], BF16)


def _rope_tables(n_tokens, head_lanes, first, rope_dims):
    half, quarter = rope_dims // 2, rope_dims // 4
    n_rows = n_tokens // GRID_W
    freqs = ROPE_THETA ** (-jnp.arange(quarter, dtype=F32) / quarter)
    lane = np.arange(128) % head_lanes - first
    active = (lane >= 0) & (lane < rope_dims)
    lane = np.where(active, lane, 0)
    use_cols = jnp.asarray(lane >= half)
    within = lane % half
    upper = within >= quarter
    lane_freqs = freqs[jnp.asarray(within % quarter)]

    def by_position(n, fn):
        return fn(jnp.arange(n, dtype=F32)[:, None] * lane_freqs[None, :])

    def per_token(fn):
        t = jnp.where(use_cols, by_position(GRID_W, fn)[None, :, :], by_position(n_rows, fn)[:, None, :])
        return t.reshape(n_tokens, 128)

    act = jnp.asarray(active)[None, :]
    up = jnp.asarray(upper)[None, :]
    cos = jnp.where(act, per_token(jnp.cos), 1.0)
    sin = jnp.where(act, per_token(jnp.sin), 0.0)
    return cos, jnp.where(up, 0.0, -sin), jnp.where(up, sin, 0.0)


_SWA_Q_ORDER = (0, 2, 1, 3)
def _regroup_w_in_kernel(wt_ref, o_ref):
    src_kpe = _C_CKV + MLA_KV_RANK
    src_sq = src_kpe + MLA_ROPE

    def put(dst, src, width):
        o_ref[:, dst:dst + width] = wt_ref[src:src + width, :].T.astype(BF16)

    for c in range(0, src_kpe, 128):
        put(c, c, 128)
    for slot, head in enumerate(_SWA_Q_ORDER):
        put(_C_SWA_Q + slot * HEAD_DIM, src_sq + head * HEAD_DIM, HEAD_DIM)
    put(_C_SWA_K, src_sq + 256, 128)
    put(_C_SWA_V, src_sq + 384, 128)
    o_ref[:, _C_KPE_BLOCK:_C_KPE_BLOCK + MLA_PAD] = jnp.zeros((o_ref.shape[0], MLA_PAD), BF16)
    put(_C_KPE, src_kpe, MLA_ROPE)


def _regroup_w_in(w_in):
    d, rows, cols = w_in.shape
    return pl.pallas_call(
        _regroup_w_in_kernel, grid=(d, 1),
        in_specs=[pl.BlockSpec((None, cols, rows), lambda l, i: (l, 0, 0))],
        out_specs=pl.BlockSpec((None, rows, W_IN_COLS), lambda l, i: (l, 0, 0)),
        out_shape=jax.ShapeDtypeStruct((d, rows, W_IN_COLS), BF16),
        compiler_params=_params(2), name="regroup_w_in",
    )(jnp.swapaxes(w_in, 1, 2))


def _regroup_w_out_kernel(w_ref, o_ref):
    first = 3 * 256
    o_ref[0:first, :] = w_ref[0:first, :].astype(BF16)
    for slot, head in enumerate(_SWA_Q_ORDER):
        o_ref[first + slot * HEAD_DIM:first + (slot + 1) * HEAD_DIM, :] = (
            w_ref[first + head * HEAD_DIM:first + (head + 1) * HEAD_DIM, :].astype(BF16))


def _regroup_w_out(w_out):
    d, rows, cols = w_out.shape
    return pl.pallas_call(
        _regroup_w_out_kernel, grid=(d, 1),
        in_specs=[pl.BlockSpec((None, rows, cols), lambda l, i: (l, 0, 0))],
        out_specs=pl.BlockSpec((None, rows, cols), lambda l, i: (l, 0, 0)),
        out_shape=jax.ShapeDtypeStruct((d, rows, cols), BF16),
        compiler_params=_params(2), name="regroup_w_out",
    )(w_out)


def _prepare(norm1, norm2, w_in, w_out, na_q_norm, na_k_norm, pool_w, pool_scale, mla_q_a_norm, mla_w_uq,
             mla_kv_norm, mla_w_ukv, mla_q_norm, mla_k_norm, swa_q_norm, swa_k_norm, mlp_w1, mlp_w2):
    d = DEPTH

    def pad_heads(a, width):
        a = a.reshape(a.shape[:-1] + (MLA_HEADS, width))
        a = jnp.pad(a, [(0, 0)] * (a.ndim - 1) + [(0, MLA_PAD - width)])
        return a.reshape(a.shape[:-2] + (MLA_HEADS * MLA_PAD,))

    ukv = mla_w_ukv.reshape(d, MLA_KV_RANK, MLA_HEADS, MLA_NOPE + MLA_V)
    eye = jnp.eye(POOL_GROUPS, dtype=pool_w.dtype)
    pool_bd = jnp.einsum("lgce,gh->lgche", pool_w, eye).reshape(d, 256, 256)

    def gain(g, reps, scale=1.0):
        return (jnp.tile(g, (1, reps)) * scale).reshape(d, 1, -1)

    return {
        "norm1": norm1.reshape(d, 1, D_MODEL), "norm2": norm2.reshape(d, 1, D_MODEL),
        "w_in": _regroup_w_in(w_in),
        "w_uq": pad_heads(mla_w_uq, MLA_QK).astype(BF16),
        "w_uk": pad_heads(ukv[..., :MLA_NOPE].reshape(d, MLA_KV_RANK, -1), MLA_NOPE).astype(BF16),
        "w_uv": pad_heads(ukv[..., MLA_NOPE:].reshape(d, MLA_KV_RANK, -1), MLA_V).astype(BF16),
        "w_out": _regroup_w_out(w_out),
        "w1": mlp_w1.astype(BF16), "w2": mlp_w2.astype(BF16),
        "pool_w": pool_bd.astype(BF16), "pool_scale": pool_scale.reshape(d, 1, 256),
        "g_naq": gain(na_q_norm, NA_HEADS, HEAD_DIM ** -0.5 * LOG2E), "g_nak": gain(na_k_norm, NA_HEADS),
        "g_qa": mla_q_a_norm.reshape(d, 1, MLA_Q_RANK), "g_kv": mla_kv_norm.reshape(d, 1, MLA_KV_RANK),
        "g_mq": pad_heads(jnp.tile(mla_q_norm, (1, MLA_HEADS)) * (MLA_QK ** -0.5 * LOG2E), MLA_QK).reshape(d, 1, -1),
        "g_mk": pad_heads(jnp.tile(mla_k_norm, (1, MLA_HEADS)), MLA_QK).reshape(d, 1, -1),
        "g_sq": gain(swa_q_norm, SWA_HEADS, HEAD_DIM ** -0.5 * LOG2E), "g_sk": gain(swa_k_norm, SWA_KV_HEADS),
    }


def _constants():
    width = MLA_HEADS * MLA_PAD
    e = np.zeros((MLA_ROPE, width), np.float32)
    for h in range(MLA_HEADS):
        e[np.arange(MLA_ROPE), h * MLA_PAD + MLA_NOPE + np.arange(MLA_ROPE)] = 1.0
    v_ones = (np.arange(width) % MLA_PAD >= MLA_V).astype(np.float32)[None, :]
    return {"e_kpe": jnp.asarray(e, BF16), "v_ones": jnp.asarray(v_ones), "m256": _block_ones(256, HEAD_DIM),
            "m128": _block_ones(128, HEAD_DIM), "m512": _block_ones(width, MLA_PAD)}


def kernel(x_prompt, x_sample, cache_na_k, cache_na_v, cache_mla_ckv, cache_mla_kpe, cache_swa_k, cache_swa_v, c, c_ctx, norm1, norm2, w_ada, b_ada, w_in, w_out, na_q_norm, na_k_norm, na_rpb, pool_w, pool_scale, mla_q_a_norm, mla_w_uq, mla_kv_norm, mla_w_ukv, mla_q_norm, mla_k_norm, swa_q_norm, swa_k_norm, swa_sink, mlp_w1, mlp_w2):
    bc, nc, _ = x_prompt.shape
    bl, nl, _ = x_sample.shape
    n_past = cache_na_k.shape[2]
    wts = _prepare(norm1, norm2, w_in, w_out, na_q_norm, na_k_norm, pool_w, pool_scale, mla_q_a_norm, mla_w_uq,
                   mla_kv_norm, mla_w_ukv, mla_q_norm, mla_k_norm, swa_q_norm, swa_k_norm, mlp_w1, mlp_w2)
    consts = _constants()
    rope_tabs = _rope_tables(nl, MLA_PAD, MLA_NOPE, MLA_ROPE) + _rope_tables(nl, HEAD_DIM, 0, HEAD_DIM)

    cond = jnp.concatenate([c_ctx[None, :], c, jnp.zeros((8 - 1 - bl, D_MODEL), c.dtype)], axis=0)
    mod = _adaln(cond, w_ada, b_ada).reshape(DEPTH, 8, 1, 6 * D_MODEL)

    def cache(a):
        return jnp.transpose(a, (0, 1, 3, 4, 2)).reshape(bl, DEPTH, -1, n_past)

    c_nak, c_nav, c_sk, c_sv = cache(cache_na_k), cache(cache_na_v), cache(cache_swa_k), cache(cache_swa_v)
    c_mk, c_mv = _mla_cache(cache_mla_ckv, cache_mla_kpe, wts, consts)
    na_bias = _na_bias(na_rpb, nl // GRID_W)

    xp = x_prompt.reshape(bc * nc, D_MODEL)
    xs = x_sample.reshape(bl * nl, D_MODEL)
    states = None
    for l in range(DEPTH):
        sink = swa_sink[l]
        outs = _pre(xp, mod, l, wts, consts, None, tokens_per_mod=bc * nc, first_mod_row=0, state_seq=nc,
                    carried=states, name="pre_ctx")
        naq, nak, nav, pool_in, mq, mk, mv, sq, sk, sv = [o.reshape(bc, nc, -1) for o in outs[:10]]
        states = outs[10:]
        ctx_mixers = [_pool(pool_in, wts, l), _mla_attention(mq, mk, mv), _attention(naq, nak, nav, dl=HEAD_DIM),
                      _attention(sq, sk, sv, dl=HEAD_DIM, sink=sink)]
        outs = _pre(xs, mod, l, wts, consts, rope_tabs, tokens_per_mod=nl, first_mod_row=1, name="pre_lat")
        naq, nak, nav, pool_in, mq, mk, mv, sq, sk, sv = [o.reshape(bl, nl, -1) for o in outs]
        o_pool, o_mla, o_na, o_swa = _run_mixers(
            [_pool(pool_in, wts, l, tiles=nl // Q_TILE),
             _mla_attention(mq, mk, mv, ck=c_mk, cv=c_mv, ctx_layer=l),
             _attention(naq, nak, nav, dl=HEAD_DIM, mode="na", ck=c_nak, cv=c_nav, ctx_layer=l,
                        bias=na_bias, bias_layer=l),
             _attention(sq, sk, sv, dl=HEAD_DIM, mode="swa", ck=c_sk, cv=c_sv, ctx_layer=l, sink=sink)],
            "lat_mix")
        c_pool, c_mla, c_na, c_swa, xs = _run_mixers(
            ctx_mixers + [_post(xs, [p.reshape(bl * nl, 256) for p in (o_na, o_pool, o_mla, o_swa)], mod, l, wts,
                                tokens_per_mod=nl, first_mod_row=1, tile=bl * nl // bc)], "ctx_mix_post_lat")
        (xp,) = _run_mixers(
            [_post(xp, [p.reshape(bc * nc, 256) for p in (c_na, c_pool, c_mla, c_swa)], mod, l, wts,
                   tokens_per_mod=bc * nc, first_mod_row=0, tile=POST_TILE)], "post_ctx")

    def heads(st, n_heads):
        return jnp.transpose(st.reshape(bc, DEPTH, n_heads, HEAD_DIM, nc), (0, 1, 4, 2, 3))

    st_nak, st_nav, st_ckv, st_kpe, st_sk, st_sv = states
    return (xp.reshape(bc, nc, D_MODEL), xs.reshape(bl, nl, D_MODEL),
            heads(st_nak, NA_HEADS), heads(st_nav, NA_HEADS),
            st_ckv, jnp.swapaxes(st_kpe, 2, 3),
            heads(st_sk, SWA_KV_HEADS), heads(st_sv, SWA_KV_HEADS))
```

```python
import functools

import numpy as np
import jax
import jax.numpy as jnp
from jax import lax
from jax.experimental import pallas as pl
from jax.experimental.pallas import tpu as pltpu

F32, BF16 = jnp.float32, jnp.bfloat16

D_MODEL = 1024
DEPTH = 4
GRID_W = 64
HEAD_DIM = 64
NA_HEADS = 4
NA_WIN_ROWS = 8
NA_WIN_COLS = 16
POOL_GROUPS = 4
POOL_GW = 64
POOL_WINDOWS = (2, 4, 8, 16)
MLA_HEADS = 4
MLA_Q_RANK = 256
MLA_KV_RANK = 128
MLA_NOPE = 64
MLA_ROPE = 32
MLA_V = 64
MLA_QK = MLA_NOPE + MLA_ROPE
MLA_PAD = 128
SWA_HEADS = 4
SWA_KV_HEADS = 2
SWA_WINDOW = 128
D_FF = 4 * D_MODEL
ROPE_THETA = 10000.0
EPS = 1e-6
NEG = -1e30
LOG2E = 1.4426950408889634

POST_TILE = 1024
POST_CHAIN = 256
PRE_TILE = 512
PRE_TILE_LATENT = 1024
PRE_CHAIN = 128
Q_TILE = 256
KEY_BLOCK = 256
NA_TILE_ROWS = Q_TILE // GRID_W
NA_KEY_ROWS = NA_TILE_ROWS + NA_WIN_ROWS
SWA_KEYS = Q_TILE + 2 * SWA_WINDOW
ADA_COLS = 2048
FF_CHUNK = 1024
W_IN_COLS = 2048
VMEM_LIMIT = 48 * 1024 * 1024

_C_NA_Q, _C_NA_K, _C_NA_V, _C_POOL, _C_CQ, _C_CKV, _C_SWA_Q, _C_SWA_K, _C_SWA_V, _C_KPE_BLOCK = (
    0, 256, 512, 768, 1024, 1280, 1408, 1664, 1792, 1920)
_C_KPE = _C_KPE_BLOCK + MLA_NOPE


def _params(n_grid):
    return pltpu.CompilerParams(dimension_semantics=("arbitrary",) * n_grid, vmem_limit_bytes=VMEM_LIMIT)


def _layer_spec(arr, l, n_grid):
    nd = arr.ndim
    shape = (None,) + tuple(arr.shape[1:])
    if n_grid == 1:
        return pl.BlockSpec(shape, lambda i: (l,) + (0,) * (nd - 1))
    return pl.BlockSpec(shape, lambda b, i: (l,) + (0,) * (nd - 1))


def _const_spec(arr, n_grid):
    nd = arr.ndim
    if n_grid == 1:
        return pl.BlockSpec(arr.shape, lambda i: (0,) * nd)
    return pl.BlockSpec(arr.shape, lambda b, i: (0,) * nd)


def _rms_full(x, g):
    return x * lax.rsqrt(jnp.mean(x * x, axis=-1, keepdims=True) + EPS) * g


def _split_dot(s, m):
    hi = s.astype(BF16)
    lo = (s - hi.astype(F32)).astype(BF16)
    return jnp.dot(hi, m, preferred_element_type=F32) + jnp.dot(lo, m, preferred_element_type=F32)


def _rms_heads(x, g, m, dim):
    ss = jnp.dot((x * x).astype(BF16), m, preferred_element_type=F32)
    return x * lax.rsqrt(ss * (1.0 / dim) + EPS) * g


def _rms_lanes(x, g, dim):
    ss = jnp.sum(x * x, axis=-1, keepdims=True)
    return x * lax.rsqrt(ss * (1.0 / dim) + EPS) * g


def _rope(x, c_ref, sa_ref, sb_ref, quarter, rows):
    return (x * c_ref[rows, :] + pltpu.roll(x, 128 - quarter, 1) * sa_ref[rows, :]
            + pltpu.roll(x, quarter, 1) * sb_ref[rows, :])


def _adaln_kernel(cond_ref, w_ref, b_ref, o_ref):
    s = jax.nn.silu(cond_ref[...]).astype(BF16)
    o_ref[...] = jnp.dot(s, w_ref[...].astype(BF16), preferred_element_type=F32) + b_ref[...]


def _adaln(cond, w_ada, b_ada):
    n_out = w_ada.shape[-1]
    return pl.pallas_call(
        _adaln_kernel,
        grid=(DEPTH, n_out // ADA_COLS),
        in_specs=[pl.BlockSpec((8, D_MODEL), lambda l, j: (0, 0)),
                  pl.BlockSpec((None, D_MODEL, ADA_COLS), lambda l, j: (l, 0, j)),
                  pl.BlockSpec((None, 1, ADA_COLS), lambda l, j: (l, 0, j))],
        out_specs=pl.BlockSpec((None, 8, ADA_COLS), lambda l, j: (l, 0, j)),
        out_shape=jax.ShapeDtypeStruct((DEPTH, 8, n_out), F32),
        compiler_params=_params(2),
        name="adaln",
    )(cond, w_ada, b_ada.reshape(DEPTH, 1, n_out))


def _pre_kernel(*refs, rope, state_seq, n_carried, state_layer):
    (x_ref, mod_ref, n1_ref, win_ref, wuq_ref, wuk_ref, wuv_ref, ones_ref, m256_ref, m128_ref,
     g_naq, g_nak, g_qa, g_kv, g_mq, g_mk, g_sq, g_sk) = refs[:18]
    pos = 18
    if rope:
        cm, sam, sbm, cs, sas, sbs = refs[pos:pos + 6]
        pos += 6
    pos += n_carried
    (o_naq, o_nak, o_nav, o_pool, o_mq, o_mk, o_mv, o_sq, o_sk, o_sv) = refs[pos:pos + 10]
    pos += 10
    if state_seq:
        state_refs = refs[pos:pos + 6]

    mod = mod_ref[...]
    sh1 = mod[:, 0:D_MODEL]
    gain1 = n1_ref[...] * (1.0 + mod[:, D_MODEL:2 * D_MODEL])
    m256 = m256_ref[...]
    for r0 in range(0, x_ref.shape[0], PRE_CHAIN):
        rs = slice(r0, r0 + PRE_CHAIN)
        x = x_ref[rs, :]
        h = x * lax.rsqrt(jnp.mean(x * x, axis=-1, keepdims=True) + EPS) * gain1 + sh1
        y = jnp.dot(h.astype(BF16), win_ref[...], preferred_element_type=F32)

        na_k = _rms_heads(y[:, _C_NA_K:_C_NA_K + 256], g_nak[...], m256, HEAD_DIM)
        na_v = y[:, _C_NA_V:_C_NA_V + 256]
        o_naq[rs, :] = _rms_heads(y[:, _C_NA_Q:_C_NA_Q + 256], g_naq[...], m256, HEAD_DIM).astype(BF16)
        o_nak[rs, :] = na_k.astype(BF16)
        o_nav[rs, :] = na_v.astype(BF16)
        o_pool[rs, :] = y[:, _C_POOL:_C_POOL + 256]

        cq = _rms_full(y[:, _C_CQ:_C_CQ + MLA_Q_RANK], g_qa[...])
        mq = jnp.dot(cq.astype(BF16), wuq_ref[...], preferred_element_type=F32)
        ckv = _rms_full(y[:, _C_CKV:_C_CKV + MLA_KV_RANK], g_kv[...])
        ckv_b = ckv.astype(BF16)
        mk = jnp.dot(ckv_b, wuk_ref[...], preferred_element_type=F32)
        kpe_block = y[:, _C_KPE_BLOCK:_C_KPE_BLOCK + MLA_PAD]
        o_mv[rs, :] = (jnp.dot(ckv_b, wuv_ref[...], preferred_element_type=F32) + ones_ref[...]).astype(BF16)
        for hd in range(MLA_HEADS):
            cols = slice(hd * MLA_PAD, (hd + 1) * MLA_PAD)
            qh = _rms_lanes(mq[:, cols], g_mq[:, cols], MLA_QK)
            kh = _rms_lanes(mk[:, cols] + kpe_block, g_mk[:, cols], MLA_QK)
            if rope:
                qh = _rope(qh, cm, sam, sbm, MLA_ROPE // 4, rs)
                kh = _rope(kh, cm, sam, sbm, MLA_ROPE // 4, rs)
            o_mq[rs, cols] = qh.astype(BF16)
            o_mk[rs, cols] = kh.astype(BF16)

        sq = _rms_heads(y[:, _C_SWA_Q:_C_SWA_Q + 256], g_sq[...], m256, HEAD_DIM)
        sk = _rms_heads(y[:, _C_SWA_K:_C_SWA_K + 128], g_sk[...], m128_ref[...], HEAD_DIM)
        sv = y[:, _C_SWA_V:_C_SWA_V + 128]
        if state_seq:
            seq, s0 = r0 // state_seq, r0 % state_seq
            vals = (na_k, na_v, ckv, y[:, _C_KPE:_C_KPE + MLA_ROPE], sk, sv)
            for ref, val, tokens_minor in zip(state_refs, vals, _STATE_TOKENS_MINOR):
                if tokens_minor:
                    val, at = val.T, (slice(None), slice(s0, s0 + PRE_CHAIN))
                else:
                    at = (slice(s0, s0 + PRE_CHAIN), slice(None))
                if state_layer is None:
                    ref[(seq,) + at] = val
                else:
                    for k in range(DEPTH):
                        ref[(seq, k) + at] = val if k == state_layer else jnp.zeros_like(val)
        if rope:
            for blk in range(2):
                cols = slice(blk * 128, (blk + 1) * 128)
                o_sq[rs, cols] = _rope(sq[:, cols], cs, sas, sbs, HEAD_DIM // 4, rs).astype(BF16)
            sk = _rope(sk, cs, sas, sbs, HEAD_DIM // 4, rs)
        else:
            o_sq[rs, :] = sq.astype(BF16)
        o_sk[rs, :] = sk.astype(BF16)
        o_sv[rs, :] = sv.astype(BF16)


_STATE_WIDTHS = (NA_HEADS * HEAD_DIM, NA_HEADS * HEAD_DIM, MLA_KV_RANK, MLA_ROPE,
                 SWA_KV_HEADS * HEAD_DIM, SWA_KV_HEADS * HEAD_DIM)
_STATE_TOKENS_MINOR = (True, True, False, True, True, True)


def _pre(x, mod, l, wts, consts, rope_tabs, *, tokens_per_mod, first_mod_row, name, tile, state_seq=0,
         carried=None):
    t = x.shape[0]
    tm = tile
    rope = rope_tabs is not None
    tiles_per_mod = tokens_per_mod // tm
    n_pos = rope_tabs[0].shape[0] // tm if rope else 1

    def row(width, dtype):
        return jax.ShapeDtypeStruct((t, width), dtype)

    def tok(width):
        return pl.BlockSpec((tm, width), lambda i: (i, 0))

    in_specs = [tok(D_MODEL),
                pl.BlockSpec((None, None, 1, 6 * D_MODEL), lambda i: (l, first_mod_row + i // tiles_per_mod, 0, 0)),
                _layer_spec(wts["norm1"], l, 1), _layer_spec(wts["w_in"], l, 1), _layer_spec(wts["w_uq"], l, 1),
                _layer_spec(wts["w_uk"], l, 1), _layer_spec(wts["w_uv"], l, 1),
                _const_spec(consts["v_ones"], 1), _const_spec(consts["m256"], 1), _const_spec(consts["m128"], 1)]
    args = [x, mod, wts["norm1"], wts["w_in"], wts["w_uq"], wts["w_uk"], wts["w_uv"],
            consts["v_ones"], consts["m256"], consts["m128"]]
    for nm in ("g_naq", "g_nak", "g_qa", "g_kv", "g_mq", "g_mk", "g_sq", "g_sk"):
        in_specs.append(_layer_spec(wts[nm], l, 1))
        args.append(wts[nm])
    if rope:
        for tab in rope_tabs:
            in_specs.append(pl.BlockSpec((tm, tab.shape[1]), lambda i: (i % n_pos, 0)))
            args.append(tab)
    widths = (256, 256, 256, 256, 512, 512, 512, 256, 128, 128)
    dtypes = (BF16, BF16, BF16, F32, BF16, BF16, BF16, BF16, BF16, BF16)
    out_shape = [row(w, d) for w, d in zip(widths, dtypes)]
    out_specs = [tok(w) for w in widths]
    aliases = {}
    n_carried = 0
    if state_seq:
        seqs_per_tile = tm // state_seq
        if carried is not None:
            n_carried = len(carried)
            for k, buf in enumerate(carried):
                aliases[len(args)] = len(out_shape) + k
                in_specs.append(pl.BlockSpec(memory_space=pl.ANY))
                args.append(buf)
        for w, tokens_minor in zip(_STATE_WIDTHS, _STATE_TOKENS_MINOR):
            tail = (w, state_seq) if tokens_minor else (state_seq, w)
            out_shape.append(jax.ShapeDtypeStruct((t // state_seq, DEPTH) + tail, F32))
            if carried is None:
                out_specs.append(pl.BlockSpec((seqs_per_tile, DEPTH) + tail, lambda i: (i, 0, 0, 0)))
            else:
                out_specs.append(pl.BlockSpec((seqs_per_tile, None) + tail, lambda i: (i, l, 0, 0)))
    return pl.pallas_call(
        functools.partial(_pre_kernel, rope=rope, state_seq=state_seq, n_carried=n_carried,
                          state_layer=l if state_seq and carried is None else None),
        grid=(t // tm,), in_specs=in_specs, out_specs=out_specs, out_shape=out_shape,
        input_output_aliases=aliases, compiler_params=_params(1), name=name,
    )(*args)


def _mla_cache_kernel(ckv_ref, kpe_ref, wuk_ref, wuv_ref, e_ref, ones_ref, m512_ref, g_mk, ok_ref, ov_ref):
    ckv_b = ckv_ref[...].astype(BF16)
    mk = jnp.dot(ckv_b, wuk_ref[...], preferred_element_type=F32) + _split_dot(kpe_ref[...], e_ref[...])
    ok_ref[...] = _rms_heads(mk, g_mk[...], m512_ref[...], MLA_QK).astype(BF16)
    ov_ref[...] = (jnp.dot(ckv_b, wuv_ref[...], preferred_element_type=F32) + ones_ref[...]).astype(BF16)


def _mla_cache(c_ckv, c_kpe, wts, consts):
    b, _, n, _ = c_ckv.shape
    width = MLA_HEADS * MLA_PAD

    def lay(arr):
        nd = arr.ndim
        return pl.BlockSpec((None,) + tuple(arr.shape[1:]), lambda l, i: (l,) + (0,) * (nd - 1))

    return pl.pallas_call(
        _mla_cache_kernel,
        grid=(DEPTH, b),
        in_specs=[pl.BlockSpec((None, None, n, MLA_KV_RANK), lambda l, i: (i, l, 0, 0)),
                  pl.BlockSpec((None, None, n, MLA_ROPE), lambda l, i: (i, l, 0, 0)),
                  lay(wts["w_uk"]), lay(wts["w_uv"]), _const_spec(consts["e_kpe"], 2),
                  _const_spec(consts["v_ones"], 2), _const_spec(consts["m512"], 2), lay(wts["g_mk"])],
        out_specs=[pl.BlockSpec((None, None, n, width), lambda l, i: (i, l, 0, 0)),
                   pl.BlockSpec((None, None, n, width), lambda l, i: (i, l, 0, 0))],
        out_shape=[jax.ShapeDtypeStruct((b, DEPTH, n, width), BF16),
                   jax.ShapeDtypeStruct((b, DEPTH, n, width), BF16)],
        compiler_params=_params(2), name="mla_cache",
    )(c_ckv, c_kpe, wts["w_uk"], wts["w_uv"], consts["e_kpe"], consts["v_ones"], consts["m512"], wts["g_mk"])


_NA_VARIANTS = ((0, 0), (4, 0), (24, 20), (28, 20))


def _na_bias_kernel(rpb_ref, o_ref, *, grid_rows):
    l, hd = pl.program_id(0), pl.program_id(1)
    n_dr, n_dc = 2 * NA_WIN_ROWS - 1, 2 * NA_WIN_COLS - 1
    base = (l * NA_HEADS + hd) * (n_dr * n_dc)
    c = lax.broadcasted_iota(jnp.int32, (GRID_W, GRID_W), 0)
    kc = lax.broadcasted_iota(jnp.int32, (GRID_W, GRID_W), 1)
    dcv = kc - c + (NA_WIN_COLS - 1)
    col_start = jnp.clip(c - NA_WIN_COLS // 2, 0, GRID_W - NA_WIN_COLS)
    col_ok = (kc >= col_start) & (kc < col_start + NA_WIN_COLS)
    neg = jnp.full((GRID_W, GRID_W), NEG, F32)
    blocks = {}

    def block(dr):
        if dr not in blocks:
            t = jnp.zeros((GRID_W, GRID_W), F32)
            for dc in range(n_dc):
                t = jnp.where(dcv == dc, rpb_ref[base + dr * n_dc + dc] * LOG2E, t)
            blocks[dr] = jnp.where(col_ok, t, NEG)
        return blocks[dr]

    for v, (r0, w0) in enumerate(_NA_VARIANTS):
        for rl in range(NA_TILE_ROWS):
            r = r0 + rl
            kr0 = min(max(r - NA_WIN_ROWS // 2, 0), grid_rows - NA_WIN_ROWS)
            for j in range(NA_KEY_ROWS):
                kr = w0 + j
                inside = kr0 <= kr < kr0 + NA_WIN_ROWS
                o_ref[v, rl * GRID_W:(rl + 1) * GRID_W, j * GRID_W:(j + 1) * GRID_W] = (
                    block(kr - r + NA_WIN_ROWS - 1) if inside else neg)


def _na_bias(na_rpb, grid_rows):
    n_k = NA_KEY_ROWS * GRID_W
    return pl.pallas_call(
        functools.partial(_na_bias_kernel, grid_rows=grid_rows),
        grid=(DEPTH, NA_HEADS),
        in_specs=[pl.BlockSpec(memory_space=pltpu.SMEM)],
        out_specs=pl.BlockSpec((None, len(_NA_VARIANTS), None, Q_TILE, n_k), lambda l, h: (l, 0, h, 0, 0)),
        out_shape=jax.ShapeDtypeStruct((DEPTH, len(_NA_VARIANTS), NA_HEADS, Q_TILE, n_k), F32),
        compiler_params=_params(2), name="na_bias",
    )(na_rpb.reshape(-1))


def _attn_kernel(*refs, tq, wk, wv, dl, n_slabs, mode, has_ctx, has_sink, n_keys):
    q_ref, k_ref, v_ref = refs[:3]
    pos = 3
    if has_ctx:
        ck_ref, cv_ref = refs[pos:pos + 2]
        pos += 2
    if mode == "na":
        bias_ref = refs[pos]
        pos += 1
    if has_sink:
        sink_ref = refs[pos]
        pos += 1
    o_ref = refs[pos]
    i = pl.program_id(1)
    hs = wk // dl

    valid = None
    if mode == "full":
        kl, vl = k_ref[...], v_ref[...]
    elif mode == "swa":
        w0 = pl.multiple_of(jnp.clip(i * tq - SWA_WINDOW, 0, n_keys - SWA_KEYS), SWA_WINDOW)
        kl, vl = k_ref[pl.ds(w0, SWA_KEYS), :], v_ref[pl.ds(w0, SWA_KEYS), :]
        qpos = i * tq + lax.broadcasted_iota(jnp.int32, (tq, SWA_KEYS), 0)
        kpos = w0 + lax.broadcasted_iota(jnp.int32, (tq, SWA_KEYS), 1)
        valid = jnp.abs(kpos - qpos) <= SWA_WINDOW
    else:
        last = n_keys // GRID_W - NA_KEY_ROWS
        w0 = pl.multiple_of(jnp.clip(i * NA_TILE_ROWS - NA_WIN_ROWS // 2, 0, last) * GRID_W, GRID_W)
        kl, vl = k_ref[pl.ds(w0, NA_KEY_ROWS * GRID_W), :], v_ref[pl.ds(w0, NA_KEY_ROWS * GRID_W), :]
    if has_ctx:
        ck_t, cv_t = ck_ref[...].astype(BF16), cv_ref[...].astype(BF16)

    lane_k = lax.broadcasted_iota(jnp.int32, (1, wk), 1)
    lane_v = lax.broadcasted_iota(jnp.int32, (1, wv), 1)
    nt = (((1,), (1,)), ((), ()))
    for s in range(n_slabs):
        qs = q_ref[:, s * wk:(s + 1) * wk]
        acc = jnp.zeros((tq, wv), F32)
        for j in range(hs):
            head = j * n_slabs + s
            qm = jnp.where((lane_k >= j * dl) & (lane_k < (j + 1) * dl), qs, jnp.zeros_like(qs))
            s_loc = lax.dot_general(qm, kl, nt, preferred_element_type=F32)
            if mode == "na":
                s_loc = s_loc + bias_ref[head]
            if valid is not None:
                s_loc = jnp.where(valid, s_loc, NEG)
            m = jnp.max(s_loc, axis=-1, keepdims=True)
            if has_ctx:
                s_ctx = jnp.dot(qm, ck_t, preferred_element_type=F32)
                m = jnp.maximum(m, jnp.max(s_ctx, axis=-1, keepdims=True))
            if has_sink:
                sink = sink_ref[head] * LOG2E
                m = jnp.maximum(m, sink)
            p_loc = jnp.exp2(s_loc - m)
            den = jnp.sum(p_loc, axis=-1, keepdims=True)
            o = jnp.dot(p_loc.astype(BF16), vl, preferred_element_type=F32)
            if has_ctx:
                p_ctx = jnp.exp2(s_ctx - m)
                den = den + jnp.sum(p_ctx, axis=-1, keepdims=True)
                o = o + lax.dot_general(p_ctx.astype(BF16), cv_t, nt, preferred_element_type=F32)
            if has_sink:
                den = den + jnp.exp2(sink - m)
            out_lanes = (lane_v >= j * HEAD_DIM) & (lane_v < (j + 1) * HEAD_DIM)
            acc = acc + jnp.where(out_lanes, o * (1.0 / den), 0.0)
        o_ref[:, s * wv:(s + 1) * wv] = acc.astype(o_ref.dtype)


class _Mixer:
    def __init__(self, body, grid, in_specs, args, out_specs, out_shape, scratch):
        self.body, self.grid = body, grid
        self.in_specs, self.args = in_specs, args
        self.out_specs, self.out_shape, self.scratch = out_specs, out_shape, scratch


def _mix_kernel(*refs, bodies, n_in, n_out, n_scratch):
    ins, outs, scr = refs[:sum(n_in)], refs[sum(n_in):sum(n_in) + sum(n_out)], refs[sum(n_in) + sum(n_out):]
    i0 = o0 = s0 = 0
    for body, ni, no, ns in zip(bodies, n_in, n_out, n_scratch):
        body(*ins[i0:i0 + ni], *outs[o0:o0 + no], *scr[s0:s0 + ns])
        i0, o0, s0 = i0 + ni, o0 + no, s0 + ns


def _run_mixers(mixers, name):
    grid = mixers[0].grid
    assert all(m.grid == grid for m in mixers)
    return pl.pallas_call(
        functools.partial(_mix_kernel, bodies=[m.body for m in mixers], n_in=[len(m.args) for m in mixers],
                          n_out=[len(m.out_shape) for m in mixers], n_scratch=[len(m.scratch) for m in mixers]),
        grid=grid,
        in_specs=[s for m in mixers for s in m.in_specs],
        out_specs=[s for m in mixers for s in m.out_specs],
        out_shape=[s for m in mixers for s in m.out_shape],
        scratch_shapes=[s for m in mixers for s in m.scratch],
        compiler_params=_params(2), name=name,
    )(*[a for m in mixers for a in m.args])


def _attention(q, k, v, *, dl, mode="full", ck=None, cv=None, ctx_layer=None, bias=None, bias_layer=None,
               sink=None):
    b, n, wq = q.shape
    _, nk, wk = k.shape
    wv = v.shape[-1]
    n_slabs = wq // wk
    tq = min(Q_TILE, n)
    in_specs = [pl.BlockSpec((None, tq, wq), lambda bi, i: (bi, i, 0)),
                pl.BlockSpec((None, nk, wk), lambda bi, i: (bi, 0, 0)),
                pl.BlockSpec((None, nk, wv), lambda bi, i: (bi, 0, 0))]
    args = [q, k, v]
    if ck is not None:
        n_ctx = ck.shape[3]
        in_specs += [pl.BlockSpec((None, None, wk, n_ctx), lambda bi, i: (bi, ctx_layer, 0, 0)),
                     pl.BlockSpec((None, None, wv, n_ctx), lambda bi, i: (bi, ctx_layer, 0, 0))]
        args += [ck, cv]
    if mode == "na":
        n_tiles = n // tq

        def variant(bi, i):
            v_idx = jnp.where(i == 0, 0, jnp.where(i <= n_tiles - 3, 1, i - (n_tiles - 4)))
            return (bias_layer, v_idx, 0, 0, 0)

        in_specs.append(pl.BlockSpec((None, None) + tuple(bias.shape[2:]), variant))
        args.append(bias)
    if sink is not None:
        in_specs.append(pl.BlockSpec(memory_space=pltpu.SMEM))
        args.append(sink)
    body = functools.partial(_attn_kernel, tq=tq, wk=wk, wv=wv, dl=dl, n_slabs=n_slabs, mode=mode,
                             has_ctx=ck is not None, has_sink=sink is not None, n_keys=nk)
    return _Mixer(body, (b, n // tq), in_specs, args,
                  [pl.BlockSpec((None, tq, n_slabs * wv), lambda bi, i: (bi, i, 0))],
                  [jax.ShapeDtypeStruct((b, n, n_slabs * wv), BF16)], [])


def _mla_attn_kernel(*refs, tq, has_ctx):
    q_ref, k_ref, v_ref = refs[:3]
    pos = 3
    if has_ctx:
        ck_ref, cv_ref = refs[pos:pos + 2]
        pos += 2
    o_ref, s_ref = refs[pos], refs[pos + 1]
    blocks = [(k_ref, v_ref, off) for off in range(0, k_ref.shape[0], KEY_BLOCK)]
    if has_ctx:
        blocks += [(ck_ref, cv_ref, off) for off in range(0, ck_ref.shape[0], KEY_BLOCK)]
    nt = (((1,), (1,)), ((), ()))
    lane = lax.broadcasted_iota(jnp.int32, (1, MLA_PAD), 1)
    outs = []
    for hd in range(MLA_HEADS):
        cols = slice(hd * MLA_PAD, (hd + 1) * MLA_PAD)
        qh = q_ref[:, cols]
        m_part = jnp.full((tq, 128), NEG, F32)
        for bi, (kr, _, off) in enumerate(blocks):
            sb = lax.dot_general(qh, kr[off:off + KEY_BLOCK, cols], nt, preferred_element_type=F32)
            s_ref[:, bi * KEY_BLOCK:(bi + 1) * KEY_BLOCK] = sb
            m_part = jnp.maximum(m_part, jnp.maximum(sb[:, :128], sb[:, 128:]))
        m = jnp.max(m_part, axis=-1, keepdims=True)
        o = jnp.zeros((tq, MLA_PAD), F32)
        for bi, (_, vr, off) in enumerate(blocks):
            p = jnp.exp2(s_ref[:, bi * KEY_BLOCK:(bi + 1) * KEY_BLOCK] - m)
            o = o + jnp.dot(p.astype(BF16), vr[off:off + KEY_BLOCK, cols], preferred_element_type=F32)
        outs.append(o * pltpu.roll(1.0 / o, MLA_V, 1))
    for hd in range(0, MLA_HEADS, 2):
        pair = jnp.where(lane < MLA_V, outs[hd], pltpu.roll(outs[hd + 1], MLA_V, 1))
        o_ref[:, (hd // 2) * 128:(hd // 2 + 1) * 128] = pair.astype(o_ref.dtype)


def _mla_attention(q, k, v, *, ck=None, cv=None, ctx_layer=None):
    b, n, w = q.shape
    nk = k.shape[1]
    tq = min(Q_TILE, n)
    in_specs = [pl.BlockSpec((None, tq, w), lambda bi, i: (bi, i, 0)),
                pl.BlockSpec((None, nk, w), lambda bi, i: (bi, 0, 0)),
                pl.BlockSpec((None, nk, w), lambda bi, i: (bi, 0, 0))]
    args = [q, k, v]
    n_ctx = 0
    if ck is not None:
        n_ctx = ck.shape[2]
        in_specs += [pl.BlockSpec((None, None, n_ctx, w), lambda bi, i: (bi, ctx_layer, 0, 0)),
                     pl.BlockSpec((None, None, n_ctx, w), lambda bi, i: (bi, ctx_layer, 0, 0))]
        args += [ck, cv]
    return _Mixer(functools.partial(_mla_attn_kernel, tq=tq, has_ctx=ck is not None), (b, n // tq), in_specs, args,
                  [pl.BlockSpec((None, tq, MLA_HEADS * MLA_V), lambda bi, i: (bi, i, 0))],
                  [jax.ShapeDtypeStruct((b, n, MLA_HEADS * MLA_V), BF16)],
                  [pltpu.VMEM((tq, nk + n_ctx), F32)])


def _pool_kernel(x_ref, w_ref, s_ref, o_ref, *, n):
    x = x_ref[...]
    t = lax.broadcasted_iota(jnp.int32, x.shape, 0)
    lane = lax.broadcasted_iota(jnp.int32, x.shape, 1)

    def later(a, d):
        return jnp.where(t < n - d, pltpu.roll(a, n - d, 0), 0.0)

    def earlier(a, d):
        return jnp.where(t >= d, pltpu.roll(a, d, 0), 0.0)

    fwd, bwd = {1: x}, {1: x}
    for k in (1, 2, 4):
        fwd[2 * k] = fwd[k] + later(fwd[k], k)
        bwd[2 * k] = bwd[k] + earlier(bwd[k], k)
    pooled = jnp.zeros_like(x)
    for gi, w in enumerate(POOL_WINDOWS):
        total = fwd[w // 2] + earlier(bwd[w // 2], 1)
        count = (jnp.clip(t + w // 2, 0, n) - jnp.clip(t - w // 2, 0, n)).astype(F32)
        in_group = (lane >= gi * POOL_GW) & (lane < (gi + 1) * POOL_GW)
        pooled = jnp.where(in_group, total / count - x, pooled)
    y = jnp.dot(pooled.astype(BF16), w_ref[...], preferred_element_type=F32)
    o_ref[...] = (y * s_ref[...]).astype(o_ref.dtype)


def _pool_once_kernel(*refs, n):
    @pl.when(pl.program_id(1) == 0)
    def _():
        _pool_kernel(*refs, n=n)


def _pool(x, wts, l, tiles=1):
    b, n, w = x.shape
    return _Mixer(functools.partial(_pool_kernel if tiles == 1 else _pool_once_kernel, n=n), (b, tiles),
                  [pl.BlockSpec((None, n, w), lambda bi, i: (bi, 0, 0)),
                   _layer_spec(wts["pool_w"], l, 2), _layer_spec(wts["pool_scale"], l, 2)],
                  [x, wts["pool_w"], wts["pool_scale"]],
                  [pl.BlockSpec((None, n, w), lambda bi, i: (bi, 0, 0))],
                  [jax.ShapeDtypeStruct((b, n, w), BF16)], [])


def _post_kernel(x_ref, p0, p1, p2, p3, mod_ref, n2_ref, wo_ref, w1_ref, w2_ref, o_ref):
    mod = mod_ref[...]
    g1 = mod[:, 2 * D_MODEL:3 * D_MODEL]
    sh2, sc2, g2 = (mod[:, 3 * D_MODEL:4 * D_MODEL], mod[:, 4 * D_MODEL:5 * D_MODEL],
                    mod[:, 5 * D_MODEL:6 * D_MODEL])
    for r0 in range(0, x_ref.shape[0], POST_CHAIN):
        rs = slice(r0, r0 + POST_CHAIN)
        x = x_ref[rs, :]
        a = jnp.zeros(x.shape, F32)
        for i, p in enumerate((p0, p1, p2, p3)):
            a = a + jnp.dot(p[rs, :], wo_ref[i * 256:(i + 1) * 256, :], preferred_element_type=F32)
        x = x + g1 * a
        h = (_rms_full(x, n2_ref[...]) * (1.0 + sc2) + sh2).astype(BF16)
        acc = jnp.zeros(x.shape, F32)
        for c in range(D_FF // FF_CHUNK):
            z = jnp.maximum(jnp.dot(h, w1_ref[:, c * FF_CHUNK:(c + 1) * FF_CHUNK], preferred_element_type=F32), 0.0)
            acc = acc + jnp.dot((z * z).astype(BF16), w2_ref[c * FF_CHUNK:(c + 1) * FF_CHUNK, :],
                                preferred_element_type=F32)
        o_ref[rs, :] = x + g2 * acc


def _post(x, parts, mod, l, wts, *, tokens_per_mod, first_mod_row, tile):
    t = x.shape[0]
    tiles_per_mod = tokens_per_mod // tile

    def tok(width):
        return pl.BlockSpec((tile, width), lambda bi, i: (bi, 0))

    def resident(arr):
        nd = arr.ndim
        return pl.BlockSpec((None,) + tuple(arr.shape[1:]), lambda bi, i: (l,) + (0,) * (nd - 1),
                            pipeline_mode=pl.Buffered(1))

    in_specs = [tok(D_MODEL), tok(256), tok(256), tok(256), tok(256),
                pl.BlockSpec((None, None, 1, 6 * D_MODEL),
                             lambda bi, i: (l, first_mod_row + bi // tiles_per_mod, 0, 0)),
                _layer_spec(wts["norm2"], l, 2), resident(wts["w_out"]), resident(wts["w1"]), resident(wts["w2"])]
    return _Mixer(_post_kernel, (t // tile, 1), in_specs,
                  [x, *parts, mod, wts["norm2"], wts["w_out"], wts["w1"], wts["w2"]],
                  [tok(D_MODEL)], [jax.ShapeDtypeStruct((t, D_MODEL), F32)], [])


def _block_ones(width, group):
    idx = np.arange(width) // group
    return jnp.asarray(idx[:, None] == idx[None, :], BF16)


def _rope_tables(n_tokens, head_lanes, first, rope_dims):
    half, quarter = rope_dims // 2, rope_dims // 4
    n_rows = n_tokens // GRID_W
    freqs = ROPE_THETA ** (-jnp.arange(quarter, dtype=F32) / quarter)
    lane = np.arange(128) % head_lanes - first
    active = (lane >= 0) & (lane < rope_dims)
    lane = np.where(active, lane, 0)
    use_cols = jnp.asarray(lane >= half)
    within = lane % half
    upper = within >= quarter
    lane_freqs = freqs[jnp.asarray(within % quarter)]

    def by_position(n, fn):
        return fn(jnp.arange(n, dtype=F32)[:, None] * lane_freqs[None, :])

    def per_token(fn):
        t = jnp.where(use_cols, by_position(GRID_W, fn)[None, :, :], by_position(n_rows, fn)[:, None, :])
        return t.reshape(n_tokens, 128)

    act = jnp.asarray(active)[None, :]
    up = jnp.asarray(upper)[None, :]
    cos = jnp.where(act, per_token(jnp.cos), 1.0)
    sin = jnp.where(act, per_token(jnp.sin), 0.0)
    return cos, jnp.where(up, 0.0, -sin), jnp.where(up, sin, 0.0)


_SWA_Q_ORDER = (0, 2, 1, 3)
def _regroup_w_in_kernel(wt_ref, o_ref):
    src_kpe = _C_CKV + MLA_KV_RANK
    src_sq = src_kpe + MLA_ROPE

    def put(dst, src, width):
        o_ref[:, dst:dst + width] = wt_ref[src:src + width, :].T.astype(BF16)

    for c in range(0, src_kpe, 128):
        put(c, c, 128)
    for slot, head in enumerate(_SWA_Q_ORDER):
        put(_C_SWA_Q + slot * HEAD_DIM, src_sq + head * HEAD_DIM, HEAD_DIM)
    put(_C_SWA_K, src_sq + 256, 128)
    put(_C_SWA_V, src_sq + 384, 128)
    o_ref[:, _C_KPE_BLOCK:_C_KPE_BLOCK + MLA_PAD] = jnp.zeros((o_ref.shape[0], MLA_PAD), BF16)
    put(_C_KPE, src_kpe, MLA_ROPE)


def _regroup_w_in(w_in):
    d, rows, cols = w_in.shape
    return pl.pallas_call(
        _regroup_w_in_kernel, grid=(d, 1),
        in_specs=[pl.BlockSpec((None, cols, rows), lambda l, i: (l, 0, 0))],
        out_specs=pl.BlockSpec((None, rows, W_IN_COLS), lambda l, i: (l, 0, 0)),
        out_shape=jax.ShapeDtypeStruct((d, rows, W_IN_COLS), BF16),
        compiler_params=_params(2), name="regroup_w_in",
    )(jnp.swapaxes(w_in, 1, 2))


def _regroup_w_out_kernel(w_ref, o_ref):
    first = 3 * 256
    o_ref[0:first, :] = w_ref[0:first, :].astype(BF16)
    for slot, head in enumerate(_SWA_Q_ORDER):
        o_ref[first + slot * HEAD_DIM:first + (slot + 1) * HEAD_DIM, :] = (
            w_ref[first + head * HEAD_DIM:first + (head + 1) * HEAD_DIM, :].astype(BF16))


def _regroup_w_out(w_out):
    d, rows, cols = w_out.shape
    return pl.pallas_call(
        _regroup_w_out_kernel, grid=(d, 1),
        in_specs=[pl.BlockSpec((None, rows, cols), lambda l, i: (l, 0, 0))],
        out_specs=pl.BlockSpec((None, rows, cols), lambda l, i: (l, 0, 0)),
        out_shape=jax.ShapeDtypeStruct((d, rows, cols), BF16),
        compiler_params=_params(2), name="regroup_w_out",
    )(w_out)


def _prepare(norm1, norm2, w_in, w_out, na_q_norm, na_k_norm, pool_w, pool_scale, mla_q_a_norm, mla_w_uq,
             mla_kv_norm, mla_w_ukv, mla_q_norm, mla_k_norm, swa_q_norm, swa_k_norm, mlp_w1, mlp_w2):
    d = DEPTH

    def pad_heads(a, width):
        a = a.reshape(a.shape[:-1] + (MLA_HEADS, width))
        a = jnp.pad(a, [(0, 0)] * (a.ndim - 1) + [(0, MLA_PAD - width)])
        return a.reshape(a.shape[:-2] + (MLA_HEADS * MLA_PAD,))

    ukv = mla_w_ukv.reshape(d, MLA_KV_RANK, MLA_HEADS, MLA_NOPE + MLA_V)
    eye = jnp.eye(POOL_GROUPS, dtype=pool_w.dtype)
    pool_bd = jnp.einsum("lgce,gh->lgche", pool_w, eye).reshape(d, 256, 256)

    def gain(g, reps, scale=1.0):
        return (jnp.tile(g, (1, reps)) * scale).reshape(d, 1, -1)

    return {
        "norm1": norm1.reshape(d, 1, D_MODEL), "norm2": norm2.reshape(d, 1, D_MODEL),
        "w_in": _regroup_w_in(w_in),
        "w_uq": pad_heads(mla_w_uq, MLA_QK).astype(BF16),
        "w_uk": pad_heads(ukv[..., :MLA_NOPE].reshape(d, MLA_KV_RANK, -1), MLA_NOPE).astype(BF16),
        "w_uv": pad_heads(ukv[..., MLA_NOPE:].reshape(d, MLA_KV_RANK, -1), MLA_V).astype(BF16),
        "w_out": _regroup_w_out(w_out),
        "w1": mlp_w1.astype(BF16), "w2": mlp_w2.astype(BF16),
        "pool_w": pool_bd.astype(BF16), "pool_scale": pool_scale.reshape(d, 1, 256),
        "g_naq": gain(na_q_norm, NA_HEADS, HEAD_DIM ** -0.5 * LOG2E), "g_nak": gain(na_k_norm, NA_HEADS),
        "g_qa": mla_q_a_norm.reshape(d, 1, MLA_Q_RANK), "g_kv": mla_kv_norm.reshape(d, 1, MLA_KV_RANK),
        "g_mq": pad_heads(jnp.tile(mla_q_norm, (1, MLA_HEADS)) * (MLA_QK ** -0.5 * LOG2E), MLA_QK).reshape(d, 1, -1),
        "g_mk": pad_heads(jnp.tile(mla_k_norm, (1, MLA_HEADS)), MLA_QK).reshape(d, 1, -1),
        "g_sq": gain(swa_q_norm, SWA_HEADS, HEAD_DIM ** -0.5 * LOG2E), "g_sk": gain(swa_k_norm, SWA_KV_HEADS),
    }


def _constants():
    width = MLA_HEADS * MLA_PAD
    e = np.zeros((MLA_ROPE, width), np.float32)
    for h in range(MLA_HEADS):
        e[np.arange(MLA_ROPE), h * MLA_PAD + MLA_NOPE + np.arange(MLA_ROPE)] = 1.0
    v_ones = (np.arange(width) % MLA_PAD >= MLA_V).astype(np.float32)[None, :]
    return {"e_kpe": jnp.asarray(e, BF16), "v_ones": jnp.asarray(v_ones), "m256": _block_ones(256, HEAD_DIM),
            "m128": _block_ones(128, HEAD_DIM), "m512": _block_ones(width, MLA_PAD)}


def kernel(x_prompt, x_sample, cache_na_k, cache_na_v, cache_mla_ckv, cache_mla_kpe, cache_swa_k, cache_swa_v, c, c_ctx, norm1, norm2, w_ada, b_ada, w_in, w_out, na_q_norm, na_k_norm, na_rpb, pool_w, pool_scale, mla_q_a_norm, mla_w_uq, mla_kv_norm, mla_w_ukv, mla_q_norm, mla_k_norm, swa_q_norm, swa_k_norm, swa_sink, mlp_w1, mlp_w2):
    bc, nc, _ = x_prompt.shape
    bl, nl, _ = x_sample.shape
    n_past = cache_na_k.shape[2]
    wts = _prepare(norm1, norm2, w_in, w_out, na_q_norm, na_k_norm, pool_w, pool_scale, mla_q_a_norm, mla_w_uq,
                   mla_kv_norm, mla_w_ukv, mla_q_norm, mla_k_norm, swa_q_norm, swa_k_norm, mlp_w1, mlp_w2)
    consts = _constants()
    rope_tabs = _rope_tables(nl, MLA_PAD, MLA_NOPE, MLA_ROPE) + _rope_tables(nl, HEAD_DIM, 0, HEAD_DIM)

    cond = jnp.concatenate([c_ctx[None, :], c, jnp.zeros((8 - 1 - bl, D_MODEL), c.dtype)], axis=0)
    mod = _adaln(cond, w_ada, b_ada).reshape(DEPTH, 8, 1, 6 * D_MODEL)

    def cache(a):
        return jnp.transpose(a, (0, 1, 3, 4, 2)).reshape(bl, DEPTH, -1, n_past)

    c_nak, c_nav, c_sk, c_sv = cache(cache_na_k), cache(cache_na_v), cache(cache_swa_k), cache(cache_swa_v)
    c_mk, c_mv = _mla_cache(cache_mla_ckv, cache_mla_kpe, wts, consts)
    na_bias = _na_bias(na_rpb, nl // GRID_W)

    xp = x_prompt.reshape(bc * nc, D_MODEL)
    xs = x_sample.reshape(bl * nl, D_MODEL)
    states = None
    for l in range(DEPTH):
        sink = swa_sink[l]
        outs = _pre(xp, mod, l, wts, consts, None, tokens_per_mod=bc * nc, first_mod_row=0, tile=PRE_TILE,
                    state_seq=nc, carried=states, name="pre_ctx")
        naq, nak, nav, pool_in, mq, mk, mv, sq, sk, sv = [o.reshape(bc, nc, -1) for o in outs[:10]]
        states = outs[10:]
        ctx_mixers = [_pool(pool_in, wts, l), _mla_attention(mq, mk, mv), _attention(naq, nak, nav, dl=HEAD_DIM),
                      _attention(sq, sk, sv, dl=HEAD_DIM, sink=sink)]
        outs = _pre(xs, mod, l, wts, consts, rope_tabs, tokens_per_mod=nl, first_mod_row=1, tile=PRE_TILE_LATENT,
                    name="pre_lat")
        naq, nak, nav, pool_in, mq, mk, mv, sq, sk, sv = [o.reshape(bl, nl, -1) for o in outs]
        o_pool, o_mla, o_na, o_swa = _run_mixers(
            [_pool(pool_in, wts, l, tiles=nl // Q_TILE),
             _mla_attention(mq, mk, mv, ck=c_mk, cv=c_mv, ctx_layer=l),
             _attention(naq, nak, nav, dl=HEAD_DIM, mode="na", ck=c_nak, cv=c_nav, ctx_layer=l,
                        bias=na_bias, bias_layer=l),
             _attention(sq, sk, sv, dl=HEAD_DIM, mode="swa", ck=c_sk, cv=c_sv, ctx_layer=l, sink=sink)],
            "lat_mix")
        c_pool, c_mla, c_na, c_swa, xs = _run_mixers(
            ctx_mixers + [_post(xs, [p.reshape(bl * nl, 256) for p in (o_na, o_pool, o_mla, o_swa)], mod, l, wts,
                                tokens_per_mod=nl, first_mod_row=1, tile=bl * nl // bc)], "ctx_mix_post_lat")
        (xp,) = _run_mixers(
            [_post(xp, [p.reshape(bc * nc, 256) for p in (c_na, c_pool, c_mla, c_swa)], mod, l, wts,
                   tokens_per_mod=bc * nc, first_mod_row=0, tile=POST_TILE)], "post_ctx")

    def heads(st, n_heads):
        return jnp.transpose(st.reshape(bc, DEPTH, n_heads, HEAD_DIM, nc), (0, 1, 4, 2, 3))

    st_nak, st_nav, st_ckv, st_kpe, st_sk, st_sv = states
    return (xp.reshape(bc, nc, D_MODEL), xs.reshape(bl, nl, D_MODEL),
            heads(st_nak, NA_HEADS), heads(st_nav, NA_HEADS),
            st_ckv, jnp.swapaxes(st_kpe, 2, 3),
            heads(st_sk, SWA_KV_HEADS), heads(st_sv, SWA_KV_HEADS))
```

```python
import functools

import numpy as np
import jax
import jax.numpy as jnp
from jax import lax
from jax.experimental import pallas as pl
from jax.experimental.pallas import tpu as pltpu

F32, BF16 = jnp.float32, jnp.bfloat16

D_MODEL = 1024
DEPTH = 4
GRID_W = 64
HEAD_DIM = 64
NA_HEADS = 4
NA_WIN_ROWS = 8
NA_WIN_COLS = 16
POOL_GROUPS = 4
POOL_GW = 64
POOL_WINDOWS = (2, 4, 8, 16)
MLA_HEADS = 4
MLA_Q_RANK = 256
MLA_KV_RANK = 128
MLA_NOPE = 64
MLA_ROPE = 32
MLA_V = 64
MLA_QK = MLA_NOPE + MLA_ROPE
MLA_PAD = 128
SWA_HEADS = 4
SWA_KV_HEADS = 2
SWA_WINDOW = 128
D_FF = 4 * D_MODEL
ROPE_THETA = 10000.0
EPS = 1e-6
NEG = -1e30
LOG2E = 1.4426950408889634

POST_TILE = 1024
POST_CHAIN = 256
PRE_TILE = 512
PRE_CHAIN = 128
Q_TILE = 256
TILES_PER_STEP = 2
KEY_BLOCK = 256
NA_TILE_ROWS = Q_TILE // GRID_W
NA_KEY_ROWS = NA_TILE_ROWS + NA_WIN_ROWS
SWA_KEYS = Q_TILE + 2 * SWA_WINDOW
ADA_COLS = 2048
FF_CHUNK = 1024
W_IN_COLS = 2048
VMEM_LIMIT = 48 * 1024 * 1024

_C_NA_Q, _C_NA_K, _C_NA_V, _C_POOL, _C_CQ, _C_CKV, _C_SWA_Q, _C_SWA_K, _C_SWA_V, _C_KPE_BLOCK = (
    0, 256, 512, 768, 1024, 1280, 1408, 1664, 1792, 1920)
_C_KPE = _C_KPE_BLOCK + MLA_NOPE


def _params(n_grid):
    return pltpu.CompilerParams(dimension_semantics=("arbitrary",) * n_grid, vmem_limit_bytes=VMEM_LIMIT)


def _layer_spec(arr, l, n_grid):
    nd = arr.ndim
    shape = (None,) + tuple(arr.shape[1:])
    if n_grid == 1:
        return pl.BlockSpec(shape, lambda i: (l,) + (0,) * (nd - 1))
    return pl.BlockSpec(shape, lambda b, i: (l,) + (0,) * (nd - 1))


def _const_spec(arr, n_grid):
    nd = arr.ndim
    if n_grid == 1:
        return pl.BlockSpec(arr.shape, lambda i: (0,) * nd)
    return pl.BlockSpec(arr.shape, lambda b, i: (0,) * nd)


def _rms_full(x, g):
    return x * lax.rsqrt(jnp.mean(x * x, axis=-1, keepdims=True) + EPS) * g


def _split_dot(s, m):
    hi = s.astype(BF16)
    lo = (s - hi.astype(F32)).astype(BF16)
    return jnp.dot(hi, m, preferred_element_type=F32) + jnp.dot(lo, m, preferred_element_type=F32)


def _rms_heads(x, g, m, dim):
    ss = jnp.dot((x * x).astype(BF16), m, preferred_element_type=F32)
    return x * lax.rsqrt(ss * (1.0 / dim) + EPS) * g


def _rms_lanes(x, g, dim):
    ss = jnp.sum(x * x, axis=-1, keepdims=True)
    return x * lax.rsqrt(ss * (1.0 / dim) + EPS) * g


def _rope(x, c_ref, sa_ref, sb_ref, quarter, rows):
    return (x * c_ref[rows, :] + pltpu.roll(x, 128 - quarter, 1) * sa_ref[rows, :]
            + pltpu.roll(x, quarter, 1) * sb_ref[rows, :])


def _adaln_kernel(cond_ref, w_ref, b_ref, o_ref):
    s = jax.nn.silu(cond_ref[...]).astype(BF16)
    o_ref[...] = jnp.dot(s, w_ref[...].astype(BF16), preferred_element_type=F32) + b_ref[...]


def _adaln(cond, w_ada, b_ada):
    n_out = w_ada.shape[-1]
    return pl.pallas_call(
        _adaln_kernel,
        grid=(DEPTH, n_out // ADA_COLS),
        in_specs=[pl.BlockSpec((8, D_MODEL), lambda l, j: (0, 0)),
                  pl.BlockSpec((None, D_MODEL, ADA_COLS), lambda l, j: (l, 0, j)),
                  pl.BlockSpec((None, 1, ADA_COLS), lambda l, j: (l, 0, j))],
        out_specs=pl.BlockSpec((None, 8, ADA_COLS), lambda l, j: (l, 0, j)),
        out_shape=jax.ShapeDtypeStruct((DEPTH, 8, n_out), F32),
        compiler_params=_params(2),
        name="adaln",
    )(cond, w_ada, b_ada.reshape(DEPTH, 1, n_out))


def _pre_kernel(*refs, rope, state_seq, n_carried, state_layer):
    (x_ref, mod_ref, n1_ref, win_ref, wuq_ref, wuk_ref, wuv_ref, ones_ref, m256_ref, m128_ref,
     g_naq, g_nak, g_qa, g_kv, g_mq, g_mk, g_sq, g_sk) = refs[:18]
    pos = 18
    if rope:
        cm, sam, sbm, cs, sas, sbs = refs[pos:pos + 6]
        pos += 6
    pos += n_carried
    (o_naq, o_nak, o_nav, o_pool, o_mq, o_mk, o_mv, o_sq, o_sk, o_sv) = refs[pos:pos + 10]
    pos += 10
    if state_seq:
        state_refs = refs[pos:pos + 6]

    mod = mod_ref[...]
    sh1 = mod[:, 0:D_MODEL]
    gain1 = n1_ref[...] * (1.0 + mod[:, D_MODEL:2 * D_MODEL])
    m256 = m256_ref[...]
    for r0 in range(0, x_ref.shape[0], PRE_CHAIN):
        rs = slice(r0, r0 + PRE_CHAIN)
        x = x_ref[rs, :]
        h = x * lax.rsqrt(jnp.mean(x * x, axis=-1, keepdims=True) + EPS) * gain1 + sh1
        y = jnp.dot(h.astype(BF16), win_ref[...], preferred_element_type=F32)

        na_k = _rms_heads(y[:, _C_NA_K:_C_NA_K + 256], g_nak[...], m256, HEAD_DIM)
        na_v = y[:, _C_NA_V:_C_NA_V + 256]
        o_naq[rs, :] = _rms_heads(y[:, _C_NA_Q:_C_NA_Q + 256], g_naq[...], m256, HEAD_DIM).astype(BF16)
        o_nak[rs, :] = na_k.astype(BF16)
        o_nav[rs, :] = na_v.astype(BF16)
        o_pool[rs, :] = y[:, _C_POOL:_C_POOL + 256]

        cq = _rms_full(y[:, _C_CQ:_C_CQ + MLA_Q_RANK], g_qa[...])
        mq = jnp.dot(cq.astype(BF16), wuq_ref[...], preferred_element_type=F32)
        ckv = _rms_full(y[:, _C_CKV:_C_CKV + MLA_KV_RANK], g_kv[...])
        ckv_b = ckv.astype(BF16)
        mk = jnp.dot(ckv_b, wuk_ref[...], preferred_element_type=F32)
        kpe_block = y[:, _C_KPE_BLOCK:_C_KPE_BLOCK + MLA_PAD]
        o_mv[rs, :] = (jnp.dot(ckv_b, wuv_ref[...], preferred_element_type=F32) + ones_ref[...]).astype(BF16)
        for hd in range(MLA_HEADS):
            cols = slice(hd * MLA_PAD, (hd + 1) * MLA_PAD)
            qh = _rms_lanes(mq[:, cols], g_mq[:, cols], MLA_QK)
            kh = _rms_lanes(mk[:, cols] + kpe_block, g_mk[:, cols], MLA_QK)
            if rope:
                qh = _rope(qh, cm, sam, sbm, MLA_ROPE // 4, rs)
                kh = _rope(kh, cm, sam, sbm, MLA_ROPE // 4, rs)
            o_mq[rs, cols] = qh.astype(BF16)
            o_mk[rs, cols] = kh.astype(BF16)

        sq = _rms_heads(y[:, _C_SWA_Q:_C_SWA_Q + 256], g_sq[...], m256, HEAD_DIM)
        sk = _rms_heads(y[:, _C_SWA_K:_C_SWA_K + 128], g_sk[...], m128_ref[...], HEAD_DIM)
        sv = y[:, _C_SWA_V:_C_SWA_V + 128]
        if state_seq:
            seq, s0 = r0 // state_seq, r0 % state_seq
            vals = (na_k, na_v, ckv, y[:, _C_KPE:_C_KPE + MLA_ROPE], sk, sv)
            for ref, val, tokens_minor in zip(state_refs, vals, _STATE_TOKENS_MINOR):
                if tokens_minor:
                    val, at = val.T, (slice(None), slice(s0, s0 + PRE_CHAIN))
                else:
                    at = (slice(s0, s0 + PRE_CHAIN), slice(None))
                if state_layer is None:
                    ref[(seq,) + at] = val
                else:
                    for k in range(DEPTH):
                        ref[(seq, k) + at] = val if k == state_layer else jnp.zeros_like(val)
        if rope:
            for blk in range(2):
                cols = slice(blk * 128, (blk + 1) * 128)
                o_sq[rs, cols] = _rope(sq[:, cols], cs, sas, sbs, HEAD_DIM // 4, rs).astype(BF16)
            sk = _rope(sk, cs, sas, sbs, HEAD_DIM // 4, rs)
        else:
            o_sq[rs, :] = sq.astype(BF16)
        o_sk[rs, :] = sk.astype(BF16)
        o_sv[rs, :] = sv.astype(BF16)


_STATE_WIDTHS = (NA_HEADS * HEAD_DIM, NA_HEADS * HEAD_DIM, MLA_KV_RANK, MLA_ROPE,
                 SWA_KV_HEADS * HEAD_DIM, SWA_KV_HEADS * HEAD_DIM)
_STATE_TOKENS_MINOR = (True, True, False, True, True, True)


def _pre(x, mod, l, wts, consts, rope_tabs, *, tokens_per_mod, first_mod_row, name, state_seq=0, carried=None):
    t = x.shape[0]
    tm = PRE_TILE
    rope = rope_tabs is not None
    tiles_per_mod = tokens_per_mod // tm
    n_pos = rope_tabs[0].shape[0] // tm if rope else 1

    def row(width, dtype):
        return jax.ShapeDtypeStruct((t, width), dtype)

    def tok(width):
        return pl.BlockSpec((tm, width), lambda i: (i, 0))

    in_specs = [tok(D_MODEL),
                pl.BlockSpec((None, None, 1, 6 * D_MODEL), lambda i: (l, first_mod_row + i // tiles_per_mod, 0, 0)),
                _layer_spec(wts["norm1"], l, 1), _layer_spec(wts["w_in"], l, 1), _layer_spec(wts["w_uq"], l, 1),
                _layer_spec(wts["w_uk"], l, 1), _layer_spec(wts["w_uv"], l, 1),
                _const_spec(consts["v_ones"], 1), _const_spec(consts["m256"], 1), _const_spec(consts["m128"], 1)]
    args = [x, mod, wts["norm1"], wts["w_in"], wts["w_uq"], wts["w_uk"], wts["w_uv"],
            consts["v_ones"], consts["m256"], consts["m128"]]
    for nm in ("g_naq", "g_nak", "g_qa", "g_kv", "g_mq", "g_mk", "g_sq", "g_sk"):
        in_specs.append(_layer_spec(wts[nm], l, 1))
        args.append(wts[nm])
    if rope:
        for tab in rope_tabs:
            in_specs.append(pl.BlockSpec((tm, tab.shape[1]), lambda i: (i % n_pos, 0)))
            args.append(tab)
    widths = (256, 256, 256, 256, 512, 512, 512, 256, 128, 128)
    dtypes = (BF16, BF16, BF16, F32, BF16, BF16, BF16, BF16, BF16, BF16)
    out_shape = [row(w, d) for w, d in zip(widths, dtypes)]
    out_specs = [tok(w) for w in widths]
    aliases = {}
    n_carried = 0
    if state_seq:
        seqs_per_tile = tm // state_seq
        if carried is not None:
            n_carried = len(carried)
            for k, buf in enumerate(carried):
                aliases[len(args)] = len(out_shape) + k
                in_specs.append(pl.BlockSpec(memory_space=pl.ANY))
                args.append(buf)
        for w, tokens_minor in zip(_STATE_WIDTHS, _STATE_TOKENS_MINOR):
            tail = (w, state_seq) if tokens_minor else (state_seq, w)
            out_shape.append(jax.ShapeDtypeStruct((t // state_seq, DEPTH) + tail, F32))
            if carried is None:
                out_specs.append(pl.BlockSpec((seqs_per_tile, DEPTH) + tail, lambda i: (i, 0, 0, 0)))
            else:
                out_specs.append(pl.BlockSpec((seqs_per_tile, None) + tail, lambda i: (i, l, 0, 0)))
    return pl.pallas_call(
        functools.partial(_pre_kernel, rope=rope, state_seq=state_seq, n_carried=n_carried,
                          state_layer=l if state_seq and carried is None else None),
        grid=(t // tm,), in_specs=in_specs, out_specs=out_specs, out_shape=out_shape,
        input_output_aliases=aliases, compiler_params=_params(1), name=name,
    )(*args)


def _mla_cache_kernel(ckv_ref, kpe_ref, wuk_ref, wuv_ref, e_ref, ones_ref, m512_ref, g_mk, ok_ref, ov_ref):
    ckv_b = ckv_ref[...].astype(BF16)
    mk = jnp.dot(ckv_b, wuk_ref[...], preferred_element_type=F32) + _split_dot(kpe_ref[...], e_ref[...])
    ok_ref[...] = _rms_heads(mk, g_mk[...], m512_ref[...], MLA_QK).astype(BF16)
    ov_ref[...] = (jnp.dot(ckv_b, wuv_ref[...], preferred_element_type=F32) + ones_ref[...]).astype(BF16)


def _mla_cache(c_ckv, c_kpe, wts, consts):
    b, _, n, _ = c_ckv.shape
    width = MLA_HEADS * MLA_PAD

    def lay(arr):
        nd = arr.ndim
        return pl.BlockSpec((None,) + tuple(arr.shape[1:]), lambda l, i: (l,) + (0,) * (nd - 1))

    return pl.pallas_call(
        _mla_cache_kernel,
        grid=(DEPTH, b),
        in_specs=[pl.BlockSpec((None, None, n, MLA_KV_RANK), lambda l, i: (i, l, 0, 0)),
                  pl.BlockSpec((None, None, n, MLA_ROPE), lambda l, i: (i, l, 0, 0)),
                  lay(wts["w_uk"]), lay(wts["w_uv"]), _const_spec(consts["e_kpe"], 2),
                  _const_spec(consts["v_ones"], 2), _const_spec(consts["m512"], 2), lay(wts["g_mk"])],
        out_specs=[pl.BlockSpec((None, None, n, width), lambda l, i: (i, l, 0, 0)),
                   pl.BlockSpec((None, None, n, width), lambda l, i: (i, l, 0, 0))],
        out_shape=[jax.ShapeDtypeStruct((b, DEPTH, n, width), BF16),
                   jax.ShapeDtypeStruct((b, DEPTH, n, width), BF16)],
        compiler_params=_params(2), name="mla_cache",
    )(c_ckv, c_kpe, wts["w_uk"], wts["w_uv"], consts["e_kpe"], consts["v_ones"], consts["m512"], wts["g_mk"])


_NA_VARIANTS = ((0, 0), (4, 0), (24, 20), (28, 20))


def _na_bias_kernel(rpb_ref, o_ref, *, grid_rows):
    l, hd = pl.program_id(0), pl.program_id(1)
    n_dr, n_dc = 2 * NA_WIN_ROWS - 1, 2 * NA_WIN_COLS - 1
    base = (l * NA_HEADS + hd) * (n_dr * n_dc)
    c = lax.broadcasted_iota(jnp.int32, (GRID_W, GRID_W), 0)
    kc = lax.broadcasted_iota(jnp.int32, (GRID_W, GRID_W), 1)
    dcv = kc - c + (NA_WIN_COLS - 1)
    col_start = jnp.clip(c - NA_WIN_COLS // 2, 0, GRID_W - NA_WIN_COLS)
    col_ok = (kc >= col_start) & (kc < col_start + NA_WIN_COLS)
    neg = jnp.full((GRID_W, GRID_W), NEG, F32)
    blocks = {}

    def block(dr):
        if dr not in blocks:
            t = jnp.zeros((GRID_W, GRID_W), F32)
            for dc in range(n_dc):
                t = jnp.where(dcv == dc, rpb_ref[base + dr * n_dc + dc] * LOG2E, t)
            blocks[dr] = jnp.where(col_ok, t, NEG)
        return blocks[dr]

    for v, (r0, w0) in enumerate(_NA_VARIANTS):
        for rl in range(NA_TILE_ROWS):
            r = r0 + rl
            kr0 = min(max(r - NA_WIN_ROWS // 2, 0), grid_rows - NA_WIN_ROWS)
            for j in range(NA_KEY_ROWS):
                kr = w0 + j
                inside = kr0 <= kr < kr0 + NA_WIN_ROWS
                o_ref[v, rl * GRID_W:(rl + 1) * GRID_W, j * GRID_W:(j + 1) * GRID_W] = (
                    block(kr - r + NA_WIN_ROWS - 1) if inside else neg)


def _na_bias(na_rpb, grid_rows):
    n_k = NA_KEY_ROWS * GRID_W
    return pl.pallas_call(
        functools.partial(_na_bias_kernel, grid_rows=grid_rows),
        grid=(DEPTH, NA_HEADS),
        in_specs=[pl.BlockSpec(memory_space=pltpu.SMEM)],
        out_specs=pl.BlockSpec((None, len(_NA_VARIANTS), None, Q_TILE, n_k), lambda l, h: (l, 0, h, 0, 0)),
        out_shape=jax.ShapeDtypeStruct((DEPTH, len(_NA_VARIANTS), NA_HEADS, Q_TILE, n_k), F32),
        compiler_params=_params(2), name="na_bias",
    )(na_rpb.reshape(-1))


def _attn_kernel(*refs, tq, wk, wv, dl, n_slabs, mode, has_ctx, has_sink, n_keys, sub=0, n_sub=1):
    q_ref, k_ref, v_ref = refs[:3]
    pos = 3
    if has_ctx:
        ck_ref, cv_ref = refs[pos:pos + 2]
        pos += 2
    if mode == "na":
        bias_ref = refs[pos + sub]
        pos += n_sub
    if has_sink:
        sink_ref = refs[pos]
        pos += 1
    o_ref = refs[pos]
    i = pl.program_id(1) * n_sub + sub
    rows = slice(sub * tq, (sub + 1) * tq)
    hs = wk // dl

    valid = None
    if mode == "full":
        kl, vl = k_ref[...], v_ref[...]
    elif mode == "swa":
        w0 = pl.multiple_of(jnp.clip(i * tq - SWA_WINDOW, 0, n_keys - SWA_KEYS), SWA_WINDOW)
        kl, vl = k_ref[pl.ds(w0, SWA_KEYS), :], v_ref[pl.ds(w0, SWA_KEYS), :]
        qpos = i * tq + lax.broadcasted_iota(jnp.int32, (tq, SWA_KEYS), 0)
        kpos = w0 + lax.broadcasted_iota(jnp.int32, (tq, SWA_KEYS), 1)
        valid = jnp.abs(kpos - qpos) <= SWA_WINDOW
    else:
        last = n_keys // GRID_W - NA_KEY_ROWS
        w0 = pl.multiple_of(jnp.clip(i * NA_TILE_ROWS - NA_WIN_ROWS // 2, 0, last) * GRID_W, GRID_W)
        kl, vl = k_ref[pl.ds(w0, NA_KEY_ROWS * GRID_W), :], v_ref[pl.ds(w0, NA_KEY_ROWS * GRID_W), :]
    if has_ctx:
        ck_t, cv_t = ck_ref[...].astype(BF16), cv_ref[...].astype(BF16)

    lane_k = lax.broadcasted_iota(jnp.int32, (1, wk), 1)
    lane_v = lax.broadcasted_iota(jnp.int32, (1, wv), 1)
    nt = (((1,), (1,)), ((), ()))
    for s in range(n_slabs):
        qs = q_ref[rows, s * wk:(s + 1) * wk]
        acc = jnp.zeros((tq, wv), F32)
        for j in range(hs):
            head = j * n_slabs + s
            qm = jnp.where((lane_k >= j * dl) & (lane_k < (j + 1) * dl), qs, jnp.zeros_like(qs))
            s_loc = lax.dot_general(qm, kl, nt, preferred_element_type=F32)
            if mode == "na":
                s_loc = s_loc + bias_ref[head]
            if valid is not None:
                s_loc = jnp.where(valid, s_loc, NEG)
            m = jnp.max(s_loc, axis=-1, keepdims=True)
            if has_ctx:
                s_ctx = jnp.dot(qm, ck_t, preferred_element_type=F32)
                m = jnp.maximum(m, jnp.max(s_ctx, axis=-1, keepdims=True))
            if has_sink:
                sink = sink_ref[head] * LOG2E
                m = jnp.maximum(m, sink)
            p_loc = jnp.exp2(s_loc - m)
            den = jnp.sum(p_loc, axis=-1, keepdims=True)
            o = jnp.dot(p_loc.astype(BF16), vl, preferred_element_type=F32)
            if has_ctx:
                p_ctx = jnp.exp2(s_ctx - m)
                den = den + jnp.sum(p_ctx, axis=-1, keepdims=True)
                o = o + lax.dot_general(p_ctx.astype(BF16), cv_t, nt, preferred_element_type=F32)
            if has_sink:
                den = den + jnp.exp2(sink - m)
            out_lanes = (lane_v >= j * HEAD_DIM) & (lane_v < (j + 1) * HEAD_DIM)
            acc = acc + jnp.where(out_lanes, o * (1.0 / den), 0.0)
        o_ref[rows, s * wv:(s + 1) * wv] = acc.astype(o_ref.dtype)


class _Mixer:
    def __init__(self, body, grid, in_specs, args, out_specs, out_shape, scratch):
        self.body, self.grid = body, grid
        self.in_specs, self.args = in_specs, args
        self.out_specs, self.out_shape, self.scratch = out_specs, out_shape, scratch


def _mix_kernel(*refs, bodies, n_in, n_out, n_scratch):
    ins, outs, scr = refs[:sum(n_in)], refs[sum(n_in):sum(n_in) + sum(n_out)], refs[sum(n_in) + sum(n_out):]
    i0 = o0 = s0 = 0
    for body, ni, no, ns in zip(bodies, n_in, n_out, n_scratch):
        body(*ins[i0:i0 + ni], *outs[o0:o0 + no], *scr[s0:s0 + ns])
        i0, o0, s0 = i0 + ni, o0 + no, s0 + ns


def _run_mixers(mixers, name):
    grid = mixers[0].grid
    assert all(m.grid == grid for m in mixers)
    return pl.pallas_call(
        functools.partial(_mix_kernel, bodies=[m.body for m in mixers], n_in=[len(m.args) for m in mixers],
                          n_out=[len(m.out_shape) for m in mixers], n_scratch=[len(m.scratch) for m in mixers]),
        grid=grid,
        in_specs=[s for m in mixers for s in m.in_specs],
        out_specs=[s for m in mixers for s in m.out_specs],
        out_shape=[s for m in mixers for s in m.out_shape],
        scratch_shapes=[s for m in mixers for s in m.scratch],
        compiler_params=_params(2), name=name,
    )(*[a for m in mixers for a in m.args])


def _attention(q, k, v, *, dl, mode="full", ck=None, cv=None, ctx_layer=None, bias=None, bias_layer=None,
               sink=None):
    b, n, wq = q.shape
    _, nk, wk = k.shape
    wv = v.shape[-1]
    n_slabs = wq // wk
    tq = min(Q_TILE, n)
    n_sub = min(TILES_PER_STEP, n // tq)
    in_specs = [pl.BlockSpec((None, tq * n_sub, wq), lambda bi, i: (bi, i, 0)),
                pl.BlockSpec((None, nk, wk), lambda bi, i: (bi, 0, 0)),
                pl.BlockSpec((None, nk, wv), lambda bi, i: (bi, 0, 0))]
    args = [q, k, v]
    if ck is not None:
        n_ctx = ck.shape[3]
        in_specs += [pl.BlockSpec((None, None, wk, n_ctx), lambda bi, i: (bi, ctx_layer, 0, 0)),
                     pl.BlockSpec((None, None, wv, n_ctx), lambda bi, i: (bi, ctx_layer, 0, 0))]
        args += [ck, cv]
    if mode == "na":
        n_tiles = n // tq

        def variant(sub):
            def index(bi, step):
                i = step * n_sub + sub
                v_idx = jnp.where(i == 0, 0, jnp.where(i <= n_tiles - 3, 1, i - (n_tiles - 4)))
                return (bias_layer, v_idx, 0, 0, 0)
            return index

        for sub in range(n_sub):
            in_specs.append(pl.BlockSpec((None, None) + tuple(bias.shape[2:]), variant(sub)))
            args.append(bias)
    if sink is not None:
        in_specs.append(pl.BlockSpec(memory_space=pltpu.SMEM))
        args.append(sink)
    def body(*refs):
        for sub in range(n_sub):
            _attn_kernel(*refs, tq=tq, wk=wk, wv=wv, dl=dl, n_slabs=n_slabs, mode=mode, has_ctx=ck is not None,
                         has_sink=sink is not None, n_keys=nk, sub=sub, n_sub=n_sub)

    return _Mixer(body, (b, n // (tq * n_sub)), in_specs, args,
                  [pl.BlockSpec((None, tq * n_sub, n_slabs * wv), lambda bi, i: (bi, i, 0))],
                  [jax.ShapeDtypeStruct((b, n, n_slabs * wv), BF16)], [])


def _mla_attn_kernel(*refs, tq, has_ctx, sub=0):
    q_ref, k_ref, v_ref = refs[:3]
    pos = 3
    if has_ctx:
        ck_ref, cv_ref = refs[pos:pos + 2]
        pos += 2
    o_ref, s_ref = refs[pos], refs[pos + 1]
    blocks = [(k_ref, v_ref, off) for off in range(0, k_ref.shape[0], KEY_BLOCK)]
    if has_ctx:
        blocks += [(ck_ref, cv_ref, off) for off in range(0, ck_ref.shape[0], KEY_BLOCK)]
    nt = (((1,), (1,)), ((), ()))
    lane = lax.broadcasted_iota(jnp.int32, (1, MLA_PAD), 1)
    outs = []
    for hd in range(MLA_HEADS):
        cols = slice(hd * MLA_PAD, (hd + 1) * MLA_PAD)
        qh = q_ref[sub * tq:(sub + 1) * tq, cols]
        m_part = jnp.full((tq, 128), NEG, F32)
        for bi, (kr, _, off) in enumerate(blocks):
            sb = lax.dot_general(qh, kr[off:off + KEY_BLOCK, cols], nt, preferred_element_type=F32)
            s_ref[:, bi * KEY_BLOCK:(bi + 1) * KEY_BLOCK] = sb
            m_part = jnp.maximum(m_part, jnp.maximum(sb[:, :128], sb[:, 128:]))
        m = jnp.max(m_part, axis=-1, keepdims=True)
        o = jnp.zeros((tq, MLA_PAD), F32)
        for bi, (_, vr, off) in enumerate(blocks):
            p = jnp.exp2(s_ref[:, bi * KEY_BLOCK:(bi + 1) * KEY_BLOCK] - m)
            o = o + jnp.dot(p.astype(BF16), vr[off:off + KEY_BLOCK, cols], preferred_element_type=F32)
        outs.append(o * pltpu.roll(1.0 / o, MLA_V, 1))
    for hd in range(0, MLA_HEADS, 2):
        pair = jnp.where(lane < MLA_V, outs[hd], pltpu.roll(outs[hd + 1], MLA_V, 1))
        o_ref[sub * tq:(sub + 1) * tq, (hd // 2) * 128:(hd // 2 + 1) * 128] = pair.astype(o_ref.dtype)


def _mla_attention(q, k, v, *, ck=None, cv=None, ctx_layer=None):
    b, n, w = q.shape
    nk = k.shape[1]
    tq = min(Q_TILE, n)
    n_sub = min(TILES_PER_STEP, n // tq)
    in_specs = [pl.BlockSpec((None, tq * n_sub, w), lambda bi, i: (bi, i, 0)),
                pl.BlockSpec((None, nk, w), lambda bi, i: (bi, 0, 0)),
                pl.BlockSpec((None, nk, w), lambda bi, i: (bi, 0, 0))]
    args = [q, k, v]
    n_ctx = 0
    if ck is not None:
        n_ctx = ck.shape[2]
        in_specs += [pl.BlockSpec((None, None, n_ctx, w), lambda bi, i: (bi, ctx_layer, 0, 0)),
                     pl.BlockSpec((None, None, n_ctx, w), lambda bi, i: (bi, ctx_layer, 0, 0))]
        args += [ck, cv]
    def body(*refs):
        for sub in range(n_sub):
            _mla_attn_kernel(*refs, tq=tq, has_ctx=ck is not None, sub=sub)

    return _Mixer(body, (b, n // (tq * n_sub)), in_specs, args,
                  [pl.BlockSpec((None, tq * n_sub, MLA_HEADS * MLA_V), lambda bi, i: (bi, i, 0))],
                  [jax.ShapeDtypeStruct((b, n, MLA_HEADS * MLA_V), BF16)],
                  [pltpu.VMEM((tq, nk + n_ctx), F32)])


def _pool_kernel(x_ref, w_ref, s_ref, o_ref, *, n):
    x = x_ref[...]
    t = lax.broadcasted_iota(jnp.int32, x.shape, 0)
    lane = lax.broadcasted_iota(jnp.int32, x.shape, 1)

    def later(a, d):
        return jnp.where(t < n - d, pltpu.roll(a, n - d, 0), 0.0)

    def earlier(a, d):
        return jnp.where(t >= d, pltpu.roll(a, d, 0), 0.0)

    fwd, bwd = {1: x}, {1: x}
    for k in (1, 2, 4):
        fwd[2 * k] = fwd[k] + later(fwd[k], k)
        bwd[2 * k] = bwd[k] + earlier(bwd[k], k)
    pooled = jnp.zeros_like(x)
    for gi, w in enumerate(POOL_WINDOWS):
        total = fwd[w // 2] + earlier(bwd[w // 2], 1)
        count = (jnp.clip(t + w // 2, 0, n) - jnp.clip(t - w // 2, 0, n)).astype(F32)
        in_group = (lane >= gi * POOL_GW) & (lane < (gi + 1) * POOL_GW)
        pooled = jnp.where(in_group, total / count - x, pooled)
    y = jnp.dot(pooled.astype(BF16), w_ref[...], preferred_element_type=F32)
    o_ref[...] = (y * s_ref[...]).astype(o_ref.dtype)


def _pool_once_kernel(*refs, n):
    @pl.when(pl.program_id(1) == 0)
    def _():
        _pool_kernel(*refs, n=n)


def _pool(x, wts, l, tiles=1):
    b, n, w = x.shape
    return _Mixer(functools.partial(_pool_kernel if tiles == 1 else _pool_once_kernel, n=n), (b, tiles),
                  [pl.BlockSpec((None, n, w), lambda bi, i: (bi, 0, 0)),
                   _layer_spec(wts["pool_w"], l, 2), _layer_spec(wts["pool_scale"], l, 2)],
                  [x, wts["pool_w"], wts["pool_scale"]],
                  [pl.BlockSpec((None, n, w), lambda bi, i: (bi, 0, 0))],
                  [jax.ShapeDtypeStruct((b, n, w), BF16)], [])


def _post_kernel(x_ref, p0, p1, p2, p3, mod_ref, n2_ref, wo_ref, w1_ref, w2_ref, o_ref):
    mod = mod_ref[...]
    g1 = mod[:, 2 * D_MODEL:3 * D_MODEL]
    sh2, sc2, g2 = (mod[:, 3 * D_MODEL:4 * D_MODEL], mod[:, 4 * D_MODEL:5 * D_MODEL],
                    mod[:, 5 * D_MODEL:6 * D_MODEL])
    for r0 in range(0, x_ref.shape[0], POST_CHAIN):
        rs = slice(r0, r0 + POST_CHAIN)
        x = x_ref[rs, :]
        a = jnp.zeros(x.shape, F32)
        for i, p in enumerate((p0, p1, p2, p3)):
            a = a + jnp.dot(p[rs, :], wo_ref[i * 256:(i + 1) * 256, :], preferred_element_type=F32)
        x = x + g1 * a
        h = (_rms_full(x, n2_ref[...]) * (1.0 + sc2) + sh2).astype(BF16)
        acc = jnp.zeros(x.shape, F32)
        for c in range(D_FF // FF_CHUNK):
            z = jnp.maximum(jnp.dot(h, w1_ref[:, c * FF_CHUNK:(c + 1) * FF_CHUNK], preferred_element_type=F32), 0.0)
            acc = acc + jnp.dot((z * z).astype(BF16), w2_ref[c * FF_CHUNK:(c + 1) * FF_CHUNK, :],
                                preferred_element_type=F32)
        o_ref[rs, :] = x + g2 * acc


def _post(x, parts, mod, l, wts, *, tokens_per_mod, first_mod_row, tile):
    t = x.shape[0]
    tiles_per_mod = tokens_per_mod // tile

    def tok(width):
        return pl.BlockSpec((tile, width), lambda bi, i: (bi, 0))

    def resident(arr):
        nd = arr.ndim
        return pl.BlockSpec((None,) + tuple(arr.shape[1:]), lambda bi, i: (l,) + (0,) * (nd - 1),
                            pipeline_mode=pl.Buffered(1))

    in_specs = [tok(D_MODEL), tok(256), tok(256), tok(256), tok(256),
                pl.BlockSpec((None, None, 1, 6 * D_MODEL),
                             lambda bi, i: (l, first_mod_row + bi // tiles_per_mod, 0, 0)),
                _layer_spec(wts["norm2"], l, 2), resident(wts["w_out"]), resident(wts["w1"]), resident(wts["w2"])]
    return _Mixer(_post_kernel, (t // tile, 1), in_specs,
                  [x, *parts, mod, wts["norm2"], wts["w_out"], wts["w1"], wts["w2"]],
                  [tok(D_MODEL)], [jax.ShapeDtypeStruct((t, D_MODEL), F32)], [])


def _block_ones(width, group):
    idx = np.arange(width) // group
    return jnp.asarray(idx[:, None] == idx[None, :], BF16)


def _rope_tables(n_tokens, head_lanes, first, rope_dims):
    half, quarter = rope_dims // 2, rope_dims // 4
    t = jnp.arange(n_tokens, dtype=jnp.int32)
    rows, cols = t // GRID_W, t % GRID_W
    freqs = ROPE_THETA ** (-jnp.arange(quarter, dtype=F32) / quarter)
    lane = np.arange(128) % head_lanes - first
    active = (lane >= 0) & (lane < rope_dims)
    lane = np.where(active, lane, 0)
    use_cols = lane >= half
    within = lane % half
    upper = within >= quarter
    f_idx = within % quarter
    pos = jnp.where(jnp.asarray(use_cols)[None, :], cols[:, None], rows[:, None]).astype(F32)
    ang = pos * freqs[jnp.asarray(f_idx)][None, :]
    act = jnp.asarray(active)[None, :]
    up = jnp.asarray(upper)[None, :]
    cos = jnp.where(act, jnp.cos(ang), 1.0)
    sin = jnp.where(act, jnp.sin(ang), 0.0)
    return cos, jnp.where(up, 0.0, -sin), jnp.where(up, sin, 0.0)


_SWA_Q_ORDER = (0, 2, 1, 3)
def _regroup_w_in_kernel(wt_ref, o_ref):
    src_kpe = _C_CKV + MLA_KV_RANK
    src_sq = src_kpe + MLA_ROPE

    def put(dst, src, width):
        o_ref[:, dst:dst + width] = wt_ref[src:src + width, :].T.astype(BF16)

    for c in range(0, src_kpe, 128):
        put(c, c, 128)
    for slot, head in enumerate(_SWA_Q_ORDER):
        put(_C_SWA_Q + slot * HEAD_DIM, src_sq + head * HEAD_DIM, HEAD_DIM)
    put(_C_SWA_K, src_sq + 256, 128)
    put(_C_SWA_V, src_sq + 384, 128)
    o_ref[:, _C_KPE_BLOCK:_C_KPE_BLOCK + MLA_PAD] = jnp.zeros((o_ref.shape[0], MLA_PAD), BF16)
    put(_C_KPE, src_kpe, MLA_ROPE)


def _regroup_w_in(w_in):
    d, rows, cols = w_in.shape
    return pl.pallas_call(
        _regroup_w_in_kernel, grid=(d, 1),
        in_specs=[pl.BlockSpec((None, cols, rows), lambda l, i: (l, 0, 0))],
        out_specs=pl.BlockSpec((None, rows, W_IN_COLS), lambda l, i: (l, 0, 0)),
        out_shape=jax.ShapeDtypeStruct((d, rows, W_IN_COLS), BF16),
        compiler_params=_params(2), name="regroup_w_in",
    )(jnp.swapaxes(w_in, 1, 2))


def _regroup_w_out_kernel(w_ref, o_ref):
    first = 3 * 256
    o_ref[0:first, :] = w_ref[0:first, :].astype(BF16)
    for slot, head in enumerate(_SWA_Q_ORDER):
        o_ref[first + slot * HEAD_DIM:first + (slot + 1) * HEAD_DIM, :] = (
            w_ref[first + head * HEAD_DIM:first + (head + 1) * HEAD_DIM, :].astype(BF16))


def _regroup_w_out(w_out):
    d, rows, cols = w_out.shape
    return pl.pallas_call(
        _regroup_w_out_kernel, grid=(d, 1),
        in_specs=[pl.BlockSpec((None, rows, cols), lambda l, i: (l, 0, 0))],
        out_specs=pl.BlockSpec((None, rows, cols), lambda l, i: (l, 0, 0)),
        out_shape=jax.ShapeDtypeStruct((d, rows, cols), BF16),
        compiler_params=_params(2), name="regroup_w_out",
    )(w_out)


def _prepare(norm1, norm2, w_in, w_out, na_q_norm, na_k_norm, pool_w, pool_scale, mla_q_a_norm, mla_w_uq,
             mla_kv_norm, mla_w_ukv, mla_q_norm, mla_k_norm, swa_q_norm, swa_k_norm, mlp_w1, mlp_w2):
    d = DEPTH

    def pad_heads(a, width):
        a = a.reshape(a.shape[:-1] + (MLA_HEADS, width))
        a = jnp.pad(a, [(0, 0)] * (a.ndim - 1) + [(0, MLA_PAD - width)])
        return a.reshape(a.shape[:-2] + (MLA_HEADS * MLA_PAD,))

    ukv = mla_w_ukv.reshape(d, MLA_KV_RANK, MLA_HEADS, MLA_NOPE + MLA_V)
    eye = jnp.eye(POOL_GROUPS, dtype=pool_w.dtype)
    pool_bd = jnp.einsum("lgce,gh->lgche", pool_w, eye).reshape(d, 256, 256)

    def gain(g, reps, scale=1.0):
        return (jnp.tile(g, (1, reps)) * scale).reshape(d, 1, -1)

    return {
        "norm1": norm1.reshape(d, 1, D_MODEL), "norm2": norm2.reshape(d, 1, D_MODEL),
        "w_in": _regroup_w_in(w_in),
        "w_uq": pad_heads(mla_w_uq, MLA_QK).astype(BF16),
        "w_uk": pad_heads(ukv[..., :MLA_NOPE].reshape(d, MLA_KV_RANK, -1), MLA_NOPE).astype(BF16),
        "w_uv": pad_heads(ukv[..., MLA_NOPE:].reshape(d, MLA_KV_RANK, -1), MLA_V).astype(BF16),
        "w_out": _regroup_w_out(w_out),
        "w1": mlp_w1.astype(BF16), "w2": mlp_w2.astype(BF16),
        "pool_w": pool_bd.astype(BF16), "pool_scale": pool_scale.reshape(d, 1, 256),
        "g_naq": gain(na_q_norm, NA_HEADS, HEAD_DIM ** -0.5 * LOG2E), "g_nak": gain(na_k_norm, NA_HEADS),
        "g_qa": mla_q_a_norm.reshape(d, 1, MLA_Q_RANK), "g_kv": mla_kv_norm.reshape(d, 1, MLA_KV_RANK),
        "g_mq": pad_heads(jnp.tile(mla_q_norm, (1, MLA_HEADS)) * (MLA_QK ** -0.5 * LOG2E), MLA_QK).reshape(d, 1, -1),
        "g_mk": pad_heads(jnp.tile(mla_k_norm, (1, MLA_HEADS)), MLA_QK).reshape(d, 1, -1),
        "g_sq": gain(swa_q_norm, SWA_HEADS, HEAD_DIM ** -0.5 * LOG2E), "g_sk": gain(swa_k_norm, SWA_KV_HEADS),
    }


def _constants():
    width = MLA_HEADS * MLA_PAD
    e = np.zeros((MLA_ROPE, width), np.float32)
    for h in range(MLA_HEADS):
        e[np.arange(MLA_ROPE), h * MLA_PAD + MLA_NOPE + np.arange(MLA_ROPE)] = 1.0
    v_ones = (np.arange(width) % MLA_PAD >= MLA_V).astype(np.float32)[None, :]
    return {"e_kpe": jnp.asarray(e, BF16), "v_ones": jnp.asarray(v_ones), "m256": _block_ones(256, HEAD_DIM),
            "m128": _block_ones(128, HEAD_DIM), "m512": _block_ones(width, MLA_PAD)}


def kernel(x_prompt, x_sample, cache_na_k, cache_na_v, cache_mla_ckv, cache_mla_kpe, cache_swa_k, cache_swa_v, c, c_ctx, norm1, norm2, w_ada, b_ada, w_in, w_out, na_q_norm, na_k_norm, na_rpb, pool_w, pool_scale, mla_q_a_norm, mla_w_uq, mla_kv_norm, mla_w_ukv, mla_q_norm, mla_k_norm, swa_q_norm, swa_k_norm, swa_sink, mlp_w1, mlp_w2):
    bc, nc, _ = x_prompt.shape
    bl, nl, _ = x_sample.shape
    n_past = cache_na_k.shape[2]
    wts = _prepare(norm1, norm2, w_in, w_out, na_q_norm, na_k_norm, pool_w, pool_scale, mla_q_a_norm, mla_w_uq,
                   mla_kv_norm, mla_w_ukv, mla_q_norm, mla_k_norm, swa_q_norm, swa_k_norm, mlp_w1, mlp_w2)
    consts = _constants()
    rope_tabs = _rope_tables(nl, MLA_PAD, MLA_NOPE, MLA_ROPE) + _rope_tables(nl, HEAD_DIM, 0, HEAD_DIM)

    cond = jnp.concatenate([c_ctx[None, :], c, jnp.zeros((8 - 1 - bl, D_MODEL), c.dtype)], axis=0)
    mod = _adaln(cond, w_ada, b_ada).reshape(DEPTH, 8, 1, 6 * D_MODEL)

    def cache(a):
        return jnp.transpose(a, (0, 1, 3, 4, 2)).reshape(bl, DEPTH, -1, n_past)

    c_nak, c_nav, c_sk, c_sv = cache(cache_na_k), cache(cache_na_v), cache(cache_swa_k), cache(cache_swa_v)
    c_mk, c_mv = _mla_cache(cache_mla_ckv, cache_mla_kpe, wts, consts)
    na_bias = _na_bias(na_rpb, nl // GRID_W)

    xp = x_prompt.reshape(bc * nc, D_MODEL)
    xs = x_sample.reshape(bl * nl, D_MODEL)
    states = None
    for l in range(DEPTH):
        sink = swa_sink[l]
        outs = _pre(xp, mod, l, wts, consts, None, tokens_per_mod=bc * nc, first_mod_row=0, state_seq=nc,
                    carried=states, name="pre_ctx")
        naq, nak, nav, pool_in, mq, mk, mv, sq, sk, sv = [o.reshape(bc, nc, -1) for o in outs[:10]]
        states = outs[10:]
        ctx_mixers = [_pool(pool_in, wts, l), _mla_attention(mq, mk, mv), _attention(naq, nak, nav, dl=HEAD_DIM),
                      _attention(sq, sk, sv, dl=HEAD_DIM, sink=sink)]
        outs = _pre(xs, mod, l, wts, consts, rope_tabs, tokens_per_mod=nl, first_mod_row=1, name="pre_lat")
        naq, nak, nav, pool_in, mq, mk, mv, sq, sk, sv = [o.reshape(bl, nl, -1) for o in outs]
        (o_pool,) = _run_mixers([_pool(pool_in, wts, l)], "lat_pool")
        o_mla, o_na, o_swa = _run_mixers(
            [_mla_attention(mq, mk, mv, ck=c_mk, cv=c_mv, ctx_layer=l),
             _attention(naq, nak, nav, dl=HEAD_DIM, mode="na", ck=c_nak, cv=c_nav, ctx_layer=l,
                        bias=na_bias, bias_layer=l),
             _attention(sq, sk, sv, dl=HEAD_DIM, mode="swa", ck=c_sk, cv=c_sv, ctx_layer=l, sink=sink)],
            "lat_mix")
        c_pool, c_mla, c_na, c_swa, xs = _run_mixers(
            ctx_mixers + [_post(xs, [p.reshape(bl * nl, 256) for p in (o_na, o_pool, o_mla, o_swa)], mod, l, wts,
                                tokens_per_mod=nl, first_mod_row=1, tile=bl * nl // bc)], "ctx_mix_post_lat")
        (xp,) = _run_mixers(
            [_post(xp, [p.reshape(bc * nc, 256) for p in (c_na, c_pool, c_mla, c_swa)], mod, l, wts,
                   tokens_per_mod=bc * nc, first_mod_row=0, tile=POST_TILE)], "post_ctx")

    def heads(st, n_heads):
        return jnp.transpose(st.reshape(bc, DEPTH, n_heads, HEAD_DIM, nc), (0, 1, 4, 2, 3))

    st_nak, st_nav, st_ckv, st_kpe, st_sk, st_sv = states
    return (xp.reshape(bc, nc, D_MODEL), xs.reshape(bl, nl, D_MODEL),
            heads(st_nak, NA_HEADS), heads(st_nav, NA_HEADS),
            st_ckv, jnp.swapaxes(st_kpe, 2, 3),
            heads(st_sk, SWA_KV_HEADS), heads(st_sv, SWA_KV_HEADS))
```
